```python
import jax, jax.numpy as jnp
from jax import lax
import numpy as np

D_MODEL = 2048
BATCH = 4
SEQ = 2048
DEPTH = 4
DEC_BATCH = 128
DEC_SEQ = 1
PAST_LEN = 16384
PAGE_SIZE = 128

N_EVEN = (DEPTH + 1) // 2
N_ODD = DEPTH // 2
D_A = D_MODEL // 2
CONV_A_W = 3
D_B = D_MODEL // 2
POOL_WINDOWS = (2, 4, 8, 16)
B_GROUP = D_B // len(POOL_WINDOWS)
POOL_HIST = max(POOL_WINDOWS) - 1
D_C = D_MODEL // 2
CHUNK = 128
C_HEADS = 8
C_HEAD_DIM = D_C // C_HEADS
D_D = D_MODEL // 2
CONF_W = 31
D_FF = -(-8 * D_MODEL // (3 * 256)) * 256
EPS = 1e-6

kernel_name = "hybrid_conv_pool_gmlp_conformer_decoder_step"


def rmsnorm(x, g):
    xf = x.astype(jnp.float32)
    y = xf * lax.rsqrt(jnp.mean(xf * xf, axis=-1, keepdims=True) + EPS)
    return (y * g.astype(jnp.float32)).astype(x.dtype)


def layernorm(x, g, b):
    xf = x.astype(jnp.float32)
    mu = jnp.mean(xf, axis=-1, keepdims=True)
    var = jnp.mean(jnp.square(xf - mu), axis=-1, keepdims=True)
    y = (xf - mu) * lax.rsqrt(var + EPS) * g.astype(jnp.float32) + b.astype(jnp.float32)
    return y.astype(x.dtype)


def causal_dwconv(hist, x, w):
    width = w.shape[0]
    full = jnp.concatenate([hist.astype(x.dtype), x], axis=1)
    y = lax.conv_general_dilated(full, w[:, None, :].astype(x.dtype), window_strides=(1,), padding='VALID',
                                 dimension_numbers=('NWC', 'WIO', 'NWC'), feature_group_count=x.shape[-1])
    return y, full[:, full.shape[1] - (width - 1):]


def multiscale_pool(hist, p, start_pos):
    T = p.shape[1]
    full_raw = jnp.concatenate([hist.astype(p.dtype), p], axis=1)
    full = full_raw.astype(jnp.float32)
    cs = jnp.concatenate([jnp.zeros_like(full[:, :1]), jnp.cumsum(full, axis=1)], axis=1)
    pos = start_pos + jnp.arange(T, dtype=jnp.int32)
    outs = []
    for g, w in enumerate(POOL_WINDOWS):
        sl = slice(g * B_GROUP, (g + 1) * B_GROUP)
        s = cs[:, POOL_HIST + 1:POOL_HIST + 1 + T, sl] - cs[:, POOL_HIST + 1 - w:POOL_HIST + 1 - w + T, sl]
        cnt = jnp.minimum(w, pos + 1).astype(jnp.float32)
        outs.append(s / cnt[None, :, None])
    pooled = jnp.concatenate(outs, axis=-1) - full[:, POOL_HIST:]
    return pooled.astype(p.dtype), full_raw[:, full_raw.shape[1] - POOL_HIST:]


def chunk_spatial_gate(u, v, w_s, b_s):
    B, T, _ = v.shape
    n_chunks = -(-T // CHUNK)
    pad = n_chunks * CHUNK - T
    vp = jnp.pad(v, ((0, 0), (0, pad), (0, 0))).reshape(B, n_chunks, CHUNK, C_HEADS, C_HEAD_DIM)
    mask = jnp.tril(jnp.ones((CHUNK, CHUNK), dtype=bool))
    ws = jnp.where(mask[None], w_s, 0.0).astype(v.dtype)
    mixed = jnp.einsum('hts,bnshd->bnthd', ws, vp) + b_s.T.astype(v.dtype)[None, None, :, :, None]
    mixed = mixed.reshape(B, n_chunks * CHUNK, D_C)[:, :T]
    return u * mixed, v[:, (n_chunks - 1) * CHUNK:]


def even_mixer(h, conv_hist, pool_hist, start_pos, w_in, w_conv, w_pool, pool_scale, w_out):
    proj = h @ w_in
    xin, g_pre, g_post, p = jnp.split(proj, [D_A, 2 * D_A, 3 * D_A], axis=-1)
    conv_y, new_conv = causal_dwconv(conv_hist, g_pre * xin, w_conv)
    y_a = g_post * conv_y
    pooled, new_pool = multiscale_pool(pool_hist, p, start_pos)
    B, T, _ = pooled.shape
    y_b = jnp.einsum('btgc,gcd->btgd', pooled.reshape(B, T, len(POOL_WINDOWS), B_GROUP), w_pool)
    y_b = y_b.reshape(B, T, D_B) * pool_scale
    return jnp.concatenate([y_a, y_b], axis=-1) @ w_out, new_conv, new_pool


def odd_mixer(h, conf_hist, w_in, v_norm_g, v_norm_b, w_spatial, b_spatial,
              w_conv, b_conv, conf_norm_g, conf_norm_b, w_out):
    proj = h @ w_in
    u, v, a, g = jnp.split(proj, [D_C, 2 * D_C, 2 * D_C + D_D], axis=-1)
    u = jax.nn.gelu(u)
    v = layernorm(jax.nn.gelu(v), v_norm_g, v_norm_b)
    y_c, new_v = chunk_spatial_gate(u, v, w_spatial, b_spatial)
    glu = a * jax.nn.sigmoid(g)
    conv_y, new_conf = causal_dwconv(conf_hist, glu, w_conv)
    y_d = jax.nn.silu(layernorm(conv_y + b_conv, conf_norm_g, conf_norm_b))
    return jnp.concatenate([y_c, y_d], axis=-1) @ w_out, new_v, new_conf


def swiglu(h, w_gate, w_up, w_down):
    return (jax.nn.silu(h @ w_gate) * (h @ w_up)) @ w_down


def run_trunk(x, conv_hist, pool_hist, conf_hist, start_pos,
              norm_mix, norm_ffn, w_in_even, w_conv_a, w_pool, pool_scale, w_out_even,
              w_in_odd, v_norm_g, v_norm_b, w_spatial, b_spatial, w_conv_d, b_conv_d,
              conf_norm_g, conf_norm_b, w_out_odd, w_ffn_gate, w_ffn_up, w_ffn_down, norm_final):
    conv_new, pool_new, chunk_new, conf_new = [], [], [], []
    for l in range(DEPTH):
        i = l // 2
        h = rmsnorm(x, norm_mix[l])
        if l % 2 == 0:
            y, nc, npl = even_mixer(h, conv_hist[i], pool_hist[i], start_pos,
                                    w_in_even[i], w_conv_a[i], w_pool[i], pool_scale[i], w_out_even[i])
            conv_new.append(nc)
            pool_new.append(npl)
        else:
            y, nv, ncf = odd_mixer(h, conf_hist[i], w_in_odd[i], v_norm_g[i], v_norm_b[i],
                                   w_spatial[i], b_spatial[i], w_conv_d[i], b_conv_d[i],
                                   conf_norm_g[i], conf_norm_b[i], w_out_odd[i])
            chunk_new.append(nv)
            conf_new.append(ncf)
        x = x + y
        x = x + swiglu(rmsnorm(x, norm_ffn[l]), w_ffn_gate[l], w_ffn_up[l], w_ffn_down[l])
    y = rmsnorm(x, norm_final)
    return y, jnp.stack(conv_new), jnp.stack(pool_new), jnp.stack(chunk_new), jnp.stack(conf_new)


def setup_inputs(seed: int = 0) -> dict:
    key = jax.random.key(seed)
    ks = jax.random.split(key, 32)
    f32 = jnp.float32

    def nrm(k, shape, scale):
        return jax.random.normal(k, shape, f32) * scale

    return {
        "x_prompt": nrm(ks[0], (BATCH, SEQ, D_MODEL), 1.0),
        "x_sample": nrm(ks[1], (DEC_BATCH, DEC_SEQ, D_MODEL), 1.0),
        "state_conv_a": nrm(ks[2], (N_EVEN, DEC_BATCH, CONV_A_W - 1, D_A), 0.5),
        "state_pool": nrm(ks[3], (N_EVEN, DEC_BATCH, POOL_HIST, D_B), 0.5),
        "state_conformer": nrm(ks[4], (N_ODD, DEC_BATCH, CONF_W - 1, D_D), 0.5),
        "norm_mix": 1.0 + nrm(ks[5], (DEPTH, D_MODEL), 0.05),
        "norm_ffn": 1.0 + nrm(ks[6], (DEPTH, D_MODEL), 0.05),
        "w_in_even": nrm(ks[7], (N_EVEN, D_MODEL, 3 * D_A + D_B), D_MODEL ** -0.5),
        "w_conv_a": nrm(ks[8], (N_EVEN, CONV_A_W, D_A), CONV_A_W ** -0.5),
        "w_pool": nrm(ks[9], (N_EVEN, len(POOL_WINDOWS), B_GROUP, B_GROUP), B_GROUP ** -0.5),
        "pool_scale": 1.0 + nrm(ks[10], (N_EVEN, D_B), 0.1),
        "w_out_even": nrm(ks[11], (N_EVEN, D_A + D_B, D_MODEL), (D_A + D_B) ** -0.5),
        "w_in_odd": nrm(ks[12], (N_ODD, D_MODEL, 2 * D_C + 2 * D_D), D_MODEL ** -0.5),
        "v_norm_g": 1.0 + nrm(ks[13], (N_ODD, D_C), 0.05),
        "v_norm_b": nrm(ks[14], (N_ODD, D_C), 0.02),
        "w_spatial": nrm(ks[15], (N_ODD, C_HEADS, CHUNK, CHUNK), CHUNK ** -0.5),
        "b_spatial": 1.0 + nrm(ks[16], (N_ODD, C_HEADS, CHUNK), 0.1),
        "w_conv_d": nrm(ks[17], (N_ODD, CONF_W, D_D), CONF_W ** -0.5),
        "b_conv_d": nrm(ks[18], (N_ODD, D_D), 0.02),
        "conf_norm_g": 1.0 + nrm(ks[19], (N_ODD, D_D), 0.05),
        "conf_norm_b": nrm(ks[20], (N_ODD, D_D), 0.02),
        "w_out_odd": nrm(ks[21], (N_ODD, D_C + D_D, D_MODEL), (D_C + D_D) ** -0.5),
        "w_ffn_gate": nrm(ks[22], (DEPTH, D_MODEL, D_FF), D_MODEL ** -0.5),
        "w_ffn_up": nrm(ks[23], (DEPTH, D_MODEL, D_FF), D_MODEL ** -0.5),
        "w_ffn_down": nrm(ks[24], (DEPTH, D_FF, D_MODEL), D_FF ** -0.5),
        "norm_final": 1.0 + nrm(ks[25], (D_MODEL,), 0.05),
    }


def reference(x_prompt, x_sample, state_conv_a, state_pool, state_conformer,
              norm_mix, norm_ffn, w_in_even, w_conv_a, w_pool, pool_scale, w_out_even,
              w_in_odd, v_norm_g, v_norm_b, w_spatial, b_spatial, w_conv_d, b_conv_d,
              conf_norm_g, conf_norm_b, w_out_odd, w_ffn_gate, w_ffn_up, w_ffn_down, norm_final):
    dt = x_prompt.dtype
    zero_conv = jnp.zeros((N_EVEN, BATCH, CONV_A_W - 1, D_A), dt)
    zero_pool = jnp.zeros((N_EVEN, BATCH, POOL_HIST, D_B), dt)
    zero_conf = jnp.zeros((N_ODD, BATCH, CONF_W - 1, D_D), dt)
    y_prompt, conv_a_prompt, pool_prompt, chunk_v_prompt, conformer_prompt = run_trunk(
        x_prompt, zero_conv, zero_pool, zero_conf, 0,
        norm_mix, norm_ffn, w_in_even, w_conv_a, w_pool, pool_scale, w_out_even,
        w_in_odd, v_norm_g, v_norm_b, w_spatial, b_spatial, w_conv_d, b_conv_d,
        conf_norm_g, conf_norm_b, w_out_odd, w_ffn_gate, w_ffn_up, w_ffn_down, norm_final)
    y_sample, conv_a_sample, pool_sample, chunk_v_sample, conformer_sample = run_trunk(
        x_sample, state_conv_a, state_pool, state_conformer, PAST_LEN,
        norm_mix, norm_ffn, w_in_even, w_conv_a, w_pool, pool_scale, w_out_even,
        w_in_odd, v_norm_g, v_norm_b, w_spatial, b_spatial, w_conv_d, b_conv_d,
        conf_norm_g, conf_norm_b, w_out_odd, w_ffn_gate, w_ffn_up, w_ffn_down, norm_final)
    return (y_prompt, y_sample, conv_a_prompt, conv_a_sample, pool_prompt, pool_sample,
            chunk_v_prompt, chunk_v_sample, conformer_prompt, conformer_sample)
```

```python
import functools

import jax
import jax.numpy as jnp
from jax import lax
from jax.experimental import pallas as pl
from jax.experimental.pallas import tpu as pltpu

F32 = jnp.float32
BF16 = jnp.bfloat16

EPS = 1e-6
PAST_LEN = 16384
POOL_WINDOWS = (2, 4, 8, 16)
CHUNK = 128

VMEM_LIMIT_BYTES = 56 * 1024 * 1024

ROW_TILE = 1040
MIX_TILE = 256
POOL_HALO = 16
CONF_HALO = 32


def _params(*sem):
    return pltpu.CompilerParams(dimension_semantics=sem,
                                vmem_limit_bytes=VMEM_LIMIT_BYTES)


def _rmsnorm(x, g):
    ms = jnp.mean(x * x, axis=-1, keepdims=True)
    return x * lax.rsqrt(ms + EPS) * g


def _layernorm(x, g, b):
    mu = jnp.mean(x, axis=-1, keepdims=True)
    xc = x - mu
    var = jnp.mean(xc * xc, axis=-1, keepdims=True)
    return xc * lax.rsqrt(var + EPS) * g + b


def _dot(a, b):
    return jnp.dot(a, b, preferred_element_type=F32)


def _norm_matmul_kernel(x_ref, g_ref, w_ref, o_ref, n_ref):
    @pl.when(pl.program_id(1) == 0)
    def _():
        n_ref[...] = _rmsnorm(x_ref[...], g_ref[...]).astype(BF16)

    o_ref[...] = _dot(n_ref[...], w_ref[...].astype(BF16))


def _norm_matmul(x, g, w, *, tn=512):
    m, d = x.shape
    n = w.shape[1]
    return pl.pallas_call(
        _norm_matmul_kernel,
        grid=(m // ROW_TILE, n // tn),
        in_specs=[pl.BlockSpec((ROW_TILE, d), lambda i, j: (i, 0)),
                  pl.BlockSpec((1, d), lambda i, j: (0, 0)),
                  pl.BlockSpec((d, tn), lambda i, j: (0, j))],
        out_specs=pl.BlockSpec((ROW_TILE, tn), lambda i, j: (i, j)),
        out_shape=jax.ShapeDtypeStruct((m, n), F32),
        scratch_shapes=[pltpu.VMEM((ROW_TILE, d), BF16)],
        compiler_params=_params("parallel", "arbitrary"),
        name="norm_matmul",
    )(x, g.reshape(1, d), w)


def _gate_up_kernel(x_ref, g_ref, wg_ref, wu_ref, h_ref, n_ref):
    @pl.when(pl.program_id(1) == 0)
    def _():
        n_ref[...] = _rmsnorm(x_ref[...], g_ref[...]).astype(BF16)

    n = n_ref[...]
    gate = _dot(n, wg_ref[...].astype(BF16))
    up = _dot(n, wu_ref[...].astype(BF16))
    h_ref[...] = (gate * jax.nn.sigmoid(gate) * up).astype(BF16)


def _gate_up(x, g, wg, wu, *, tn=256):
    m, d = x.shape
    f = wg.shape[1]
    return pl.pallas_call(
        _gate_up_kernel,
        grid=(m // ROW_TILE, f // tn),
        in_specs=[pl.BlockSpec((ROW_TILE, d), lambda i, j: (i, 0)),
                  pl.BlockSpec((1, d), lambda i, j: (0, 0)),
                  pl.BlockSpec((d, tn), lambda i, j: (0, j)),
                  pl.BlockSpec((d, tn), lambda i, j: (0, j))],
        out_specs=pl.BlockSpec((ROW_TILE, tn), lambda i, j: (i, j)),
        out_shape=jax.ShapeDtypeStruct((m, f), BF16),
        scratch_shapes=[pltpu.VMEM((ROW_TILE, d), BF16)],
        compiler_params=_params("parallel", "arbitrary"),
        name="ffn_gate_up",
    )(x, g.reshape(1, d), wg, wu)


def _matmul_res_kernel(a_ref, w_ref, r_ref, o_ref):
    o_ref[...] = r_ref[...] + _dot(a_ref[...], w_ref[...].astype(BF16))


def _matmul_res(a, w, r, *, tn=256, name):
    m, k = a.shape
    n = w.shape[1]
    return pl.pallas_call(
        _matmul_res_kernel,
        grid=(m // ROW_TILE, n // tn),
        in_specs=[pl.BlockSpec((ROW_TILE, k), lambda i, j: (i, 0)),
                  pl.BlockSpec((k, tn), lambda i, j: (0, j)),
                  pl.BlockSpec((ROW_TILE, tn), lambda i, j: (i, j))],
        out_specs=pl.BlockSpec((ROW_TILE, tn), lambda i, j: (i, j)),
        out_shape=jax.ShapeDtypeStruct((m, n), F32),
        compiler_params=_params("parallel", "arbitrary"),
        name=name,
    )(a, w, r)


def _final_norm_kernel(x_ref, g_ref, o_ref):
    o_ref[...] = _rmsnorm(x_ref[...], g_ref[...])


def _final_norm(x, g, *, tm=520):
    m, d = x.shape
    return pl.pallas_call(
        _final_norm_kernel,
        grid=(m // tm,),
        in_specs=[pl.BlockSpec((tm, d), lambda i: (i, 0)),
                  pl.BlockSpec((1, d), lambda i: (0, 0))],
        out_specs=pl.BlockSpec((tm, d), lambda i: (i, 0)),
        out_shape=jax.ShapeDtypeStruct((m, d), F32),
        compiler_params=_params("parallel"),
        name="final_norm",
    )(x, g.reshape(1, d))


def _pool_project(pooled, w_pool_ref, scale_ref, cat_ref, col0):
    gw = w_pool_ref.shape[1]
    for g in range(len(POOL_WINDOWS)):
        sl = slice(g * gw, (g + 1) * gw)
        yb = _dot(pooled[:, sl].astype(BF16), w_pool_ref[g].astype(BF16))
        cat_ref[:, col0 + g * gw:col0 + (g + 1) * gw] = (
            yb * scale_ref[:, sl]).astype(cat_ref.dtype)


def _even_prompt_kernel(xin_ref, gpre_ref, gpost_ref, p_ref,
                        hxin_ref, hgpre_ref, hp_ref,
                        wconv_ref, wpool_ref, scale_ref,
                        cat_ref, conv_state_ref, pool_state_ref,
                        gx_ext, p_ext):
    tt, da = xin_ref.shape
    h = POOL_HALO
    i = pl.program_id(1)
    live = (i > 0).astype(F32)

    gx = gpre_ref[...] * xin_ref[...]
    gx_ext[0:h, :] = hgpre_ref[...] * hxin_ref[...] * live
    gx_ext[h:h + tt, :] = gx
    conv = (gx_ext[pl.ds(h - 2, tt), :] * wconv_ref[0:1, :]
            + gx_ext[pl.ds(h - 1, tt), :] * wconv_ref[1:2, :]
            + gx * wconv_ref[2:3, :])
    cat_ref[:, 0:da] = (gpost_ref[...] * conv).astype(cat_ref.dtype)
    conv_state_ref[0] = gx_ext[pl.ds(h + tt - 2, 2), :]

    p = p_ref[...]
    p_ext[0:h, :] = hp_ref[...] * live
    p_ext[h:h + tt, :] = p
    pos = i * tt + lax.broadcasted_iota(jnp.int32, (tt, 1), 0)
    gw = da // len(POOL_WINDOWS)
    pooled = []
    for g, w in enumerate(POOL_WINDOWS):
        sl = slice(g * gw, (g + 1) * gw)
        s = p[:, sl]
        for k in range(1, w):
            s = s + p_ext[pl.ds(h - k, tt), sl]
        cnt = jnp.minimum(w, pos + 1).astype(F32)
        pooled.append(s / cnt - p[:, sl])
    pooled = jnp.concatenate(pooled, axis=-1)
    _pool_project(pooled, wpool_ref, scale_ref, cat_ref, da)
    nh = pool_state_ref.shape[1]
    pool_state_ref[0] = p_ext[pl.ds(h + tt - nh, nh), :]


def _even_prompt(proj, w_conv, w_pool, pool_scale, *, batch, seq):
    da = w_conv.shape[1]
    tt = MIX_TILE
    nt = seq // tt
    hb = tt // POOL_HALO

    def main(c):
        return pl.BlockSpec((tt, da), lambda b, i: (b * nt + i, c))

    def halo(c):
        return pl.BlockSpec(
            (POOL_HALO, da),
            lambda b, i: (jnp.maximum((b * nt + i) * hb - 1, 0), c))

    nwin = len(POOL_WINDOWS)
    return pl.pallas_call(
        _even_prompt_kernel,
        grid=(batch, nt),
        in_specs=[main(0), main(1), main(2), main(3), halo(0), halo(1), halo(3),
                  pl.BlockSpec((3, da), lambda b, i: (0, 0)),
                  pl.BlockSpec((nwin, da // nwin, da // nwin), lambda b, i: (0, 0, 0)),
                  pl.BlockSpec((1, da), lambda b, i: (0, 0))],
        out_specs=[pl.BlockSpec((tt, 2 * da), lambda b, i: (b * nt + i, 0)),
                   pl.BlockSpec((1, 2, da), lambda b, i: (b, 0, 0)),
                   pl.BlockSpec((1, POOL_WINDOWS[-1] - 1, da), lambda b, i: (b, 0, 0))],
        out_shape=[jax.ShapeDtypeStruct((batch * seq, 2 * da), BF16),
                   jax.ShapeDtypeStruct((batch, 2, da), F32),
                   jax.ShapeDtypeStruct((batch, POOL_WINDOWS[-1] - 1, da), F32)],
        scratch_shapes=[pltpu.VMEM((POOL_HALO + tt, da), F32),
                        pltpu.VMEM((POOL_HALO + tt, da), F32)],
        compiler_params=_params("parallel", "arbitrary"),
        name="even_mixer_prompt",
    )(proj, proj, proj, proj, proj, proj, proj,
      w_conv, w_pool, pool_scale.reshape(1, da))


def _even_sample_kernel(xin_ref, gpre_ref, gpost_ref, p_ref,
                        conv_hist_ref, pool_hist_ref,
                        wconv_ref, wpool_ref, scale_ref,
                        cat_ref, gx_ref):
    s, da = xin_ref.shape
    gx = gpre_ref[...] * xin_ref[...]
    gx_ref[...] = gx
    nc = conv_hist_ref.shape[1]
    conv = (jnp.sum(conv_hist_ref[...] * wconv_ref[0:nc, :][None], axis=1)
            + gx * wconv_ref[nc:nc + 1, :])
    cat_ref[:, 0:da] = (gpost_ref[...] * conv).astype(cat_ref.dtype)

    p = p_ref[...]
    nh = pool_hist_ref.shape[1]
    gw = da // len(POOL_WINDOWS)
    k_idx = lax.broadcasted_iota(jnp.int32, (nh, da), 0)
    col = lax.broadcasted_iota(jnp.int32, (nh, da), 1)
    window = jnp.zeros((nh, da), jnp.int32)
    for g, w in enumerate(POOL_WINDOWS):
        window = jnp.where(col // gw == g, w, window)
    in_window = (nh - k_idx <= window - 1).astype(F32)
    ssum = jnp.sum(pool_hist_ref[...] * in_window[None], axis=1) + p
    cnt = jnp.minimum(window[0:1, :], PAST_LEN + 1).astype(F32)
    pooled = ssum / cnt - p
    _pool_project(pooled, wpool_ref, scale_ref, cat_ref, da)


def _even_sample(proj, conv_hist, pool_hist, w_conv, w_pool, pool_scale, *, row0):
    s, nc, da = conv_hist.shape
    nh = pool_hist.shape[1]
    rb = row0 // s
    nwin = len(POOL_WINDOWS)

    def seg(c):
        return pl.BlockSpec((s, da), lambda i: (rb, c))

    return pl.pallas_call(
        _even_sample_kernel,
        grid=(1,),
        in_specs=[seg(0), seg(1), seg(2), seg(3),
                  pl.BlockSpec((s, nc, da), lambda i: (0, 0, 0)),
                  pl.BlockSpec((s, nh, da), lambda i: (0, 0, 0)),
                  pl.BlockSpec((nc + 1, da), lambda i: (0, 0)),
                  pl.BlockSpec((nwin, da // nwin, da // nwin), lambda i: (0, 0, 0)),
                  pl.BlockSpec((1, da), lambda i: (0, 0))],
        out_specs=[pl.BlockSpec((s, 2 * da), lambda i: (0, 0)),
                   pl.BlockSpec((s, da), lambda i: (0, 0))],
        out_shape=[jax.ShapeDtypeStruct((s, 2 * da), BF16),
                   jax.ShapeDtypeStruct((s, da), F32)],
        compiler_params=_params("arbitrary"),
        name="even_mixer_sample",
    )(proj, proj, proj, proj, conv_hist, pool_hist,
      w_conv, w_pool, pool_scale.reshape(1, da))


def _odd_prompt_kernel(u_ref, v_ref, a_ref, g_ref, ha_ref, hg_ref,
                       vng_ref, vnb_ref, ws_ref, bst_ref,
                       wconv_ref, bconv_ref, cng_ref, cnb_ref,
                       cat_ref, chunk_v_ref, conf_state_ref,
                       glu_ext):
    tt, dc = u_ref.shape
    h = CONF_HALO
    i = pl.program_id(1)
    live = (i > 0).astype(F32)

    u = jax.nn.gelu(u_ref[...])
    vn = _layernorm(jax.nn.gelu(v_ref[...]), vng_ref[...], vnb_ref[...])
    chunk_v_ref[0] = vn[tt - CHUNK:tt, :]
    nheads = ws_ref.shape[0]
    hd = dc // nheads
    row = lax.broadcasted_iota(jnp.int32, (CHUNK, CHUNK), 0)
    colm = lax.broadcasted_iota(jnp.int32, (CHUNK, CHUNK), 1)
    vb = vn.astype(BF16)
    for hh in range(nheads):
        ws = jnp.where(colm <= row, ws_ref[hh], 0.0).astype(BF16)
        bias = bst_ref[:, hh:hh + 1]
        for c in range(tt // CHUNK):
            rs = slice(c * CHUNK, (c + 1) * CHUNK)
            cs = slice(hh * hd, (hh + 1) * hd)
            mixed = _dot(ws, vb[rs, cs]) + bias
            cat_ref[rs, cs] = (u[rs, cs] * mixed).astype(cat_ref.dtype)

    glu = a_ref[...] * jax.nn.sigmoid(g_ref[...])
    glu_ext[0:h, :] = ha_ref[...] * jax.nn.sigmoid(hg_ref[...]) * live
    glu_ext[h:h + tt, :] = glu
    nw = wconv_ref.shape[0]
    acc = glu * wconv_ref[nw - 1:nw, :]
    for k in range(nw - 1):
        acc = acc + glu_ext[pl.ds(h - (nw - 1) + k, tt), :] * wconv_ref[k:k + 1, :]
    y = _layernorm(acc + bconv_ref[...], cng_ref[...], cnb_ref[...])
    cat_ref[:, dc:2 * dc] = (y * jax.nn.sigmoid(y)).astype(cat_ref.dtype)
    conf_state_ref[0] = glu_ext[pl.ds(h + tt - (nw - 1), nw - 1), :]


def _odd_prompt(proj, v_norm_g, v_norm_b, w_spatial, b_spatial, w_conv, b_conv,
                conf_norm_g, conf_norm_b, *, batch, seq):
    nw, dc = w_conv.shape
    nheads = w_spatial.shape[0]
    tt = MIX_TILE
    nt = seq // tt
    hb = tt // CONF_HALO

    def main(c):
        return pl.BlockSpec((tt, dc), lambda b, i: (b * nt + i, c))

    def halo(c):
        return pl.BlockSpec(
            (CONF_HALO, dc),
            lambda b, i: (jnp.maximum((b * nt + i) * hb - 1, 0), c))

    def vec():
        return pl.BlockSpec((1, dc), lambda b, i: (0, 0))

    return pl.pallas_call(
        _odd_prompt_kernel,
        grid=(batch, nt),
        in_specs=[main(0), main(1), main(2), main(3), halo(2), halo(3),
                  vec(), vec(),
                  pl.BlockSpec((nheads, CHUNK, CHUNK), lambda b, i: (0, 0, 0)),
                  pl.BlockSpec((CHUNK, nheads), lambda b, i: (0, 0)),
                  pl.BlockSpec((nw, dc), lambda b, i: (0, 0)),
                  vec(), vec(), vec()],
        out_specs=[pl.BlockSpec((tt, 2 * dc), lambda b, i: (b * nt + i, 0)),
                   pl.BlockSpec((1, CHUNK, dc), lambda b, i: (b, 0, 0)),
                   pl.BlockSpec((1, nw - 1, dc), lambda b, i: (b, 0, 0))],
        out_shape=[jax.ShapeDtypeStruct((batch * seq, 2 * dc), BF16),
                   jax.ShapeDtypeStruct((batch, CHUNK, dc), F32),
                   jax.ShapeDtypeStruct((batch, nw - 1, dc), F32)],
        scratch_shapes=[pltpu.VMEM((CONF_HALO + tt, dc), F32)],
        compiler_params=_params("parallel", "arbitrary"),
        name="odd_mixer_prompt",
    )(proj, proj, proj, proj, proj, proj,
      v_norm_g.reshape(1, dc), v_norm_b.reshape(1, dc), w_spatial, b_spatial.T,
      w_conv, b_conv.reshape(1, dc), conf_norm_g.reshape(1, dc),
      conf_norm_b.reshape(1, dc))


def _odd_sample_kernel(u_ref, v_ref, a_ref, g_ref, hist_ref,
                       vng_ref, vnb_ref, ws0_ref, b0_ref,
                       wconv_ref, bconv_ref, cng_ref, cnb_ref,
                       cat_ref, vn_ref, glu_ref):
    s, dc = u_ref.shape
    u = jax.nn.gelu(u_ref[...])
    vn = _layernorm(jax.nn.gelu(v_ref[...]), vng_ref[...], vnb_ref[...])
    vn_ref[...] = vn
    mixed = ws0_ref[...] * vn + b0_ref[...]
    cat_ref[:, 0:dc] = (u * mixed).astype(cat_ref.dtype)

    glu = a_ref[...] * jax.nn.sigmoid(g_ref[...])
    glu_ref[...] = glu
    nh = hist_ref.shape[1]
    acc = (jnp.sum(hist_ref[...] * wconv_ref[0:nh, :][None], axis=1)
           + glu * wconv_ref[nh:nh + 1, :])
    y = _layernorm(acc + bconv_ref[...], cng_ref[...], cnb_ref[...])
    cat_ref[:, dc:2 * dc] = (y * jax.nn.sigmoid(y)).astype(cat_ref.dtype)


def _odd_sample(proj, hist, v_norm_g, v_norm_b, w_spatial, b_spatial, w_conv, b_conv,
                conf_norm_g, conf_norm_b, *, row0):
    s, nh, dc = hist.shape
    nheads = w_spatial.shape[0]
    hd = dc // nheads
    rb = row0 // s
    ws0 = jnp.repeat(w_spatial[:, 0, 0], hd).reshape(1, dc)
    b0 = jnp.repeat(b_spatial[:, 0], hd).reshape(1, dc)

    def seg(c):
        return pl.BlockSpec((s, dc), lambda i: (rb, c))

    def vec():
        return pl.BlockSpec((1, dc), lambda i: (0, 0))

    return pl.pallas_call(
        _odd_sample_kernel,
        grid=(1,),
        in_specs=[seg(0), seg(1), seg(2), seg(3),
                  pl.BlockSpec((s, nh, dc), lambda i: (0, 0, 0)),
                  vec(), vec(), vec(), vec(),
                  pl.BlockSpec((nh + 1, dc), lambda i: (0, 0)),
                  vec(), vec(), vec()],
        out_specs=[pl.BlockSpec((s, 2 * dc), lambda i: (0, 0)),
                   pl.BlockSpec((s, dc), lambda i: (0, 0)),
                   pl.BlockSpec((s, dc), lambda i: (0, 0))],
        out_shape=[jax.ShapeDtypeStruct((s, 2 * dc), BF16),
                   jax.ShapeDtypeStruct((s, dc), F32),
                   jax.ShapeDtypeStruct((s, dc), F32)],
        compiler_params=_params("arbitrary"),
        name="odd_mixer_sample",
    )(proj, proj, proj, proj, hist,
      v_norm_g.reshape(1, dc), v_norm_b.reshape(1, dc), ws0, b0,
      w_conv, b_conv.reshape(1, dc), conf_norm_g.reshape(1, dc),
      conf_norm_b.reshape(1, dc))


def _push(hist, new_row):
    return jnp.concatenate([hist[:, 1:], new_row[:, None, :]], axis=1)


def kernel(x_prompt, x_sample, state_conv_a, state_pool, state_conformer, norm_mix, norm_ffn, w_in_even, w_conv_a, w_pool, pool_scale, w_out_even, w_in_odd, v_norm_g, v_norm_b, w_spatial, b_spatial, w_conv_d, b_conv_d, conf_norm_g, conf_norm_b, w_out_odd, w_ffn_gate, w_ffn_up, w_ffn_down, norm_final):
    batch, seq, d = x_prompt.shape
    ns = x_sample.shape[0]
    mp = batch * seq
    depth = norm_mix.shape[0]
    da = w_conv_a.shape[2]
    assert x_sample.shape[1] == 1 and mp % ns == 0 and (mp + ns) % ROW_TILE == 0
    assert seq % MIX_TILE == 0 and MIX_TILE % CHUNK == 0

    x = jnp.concatenate([x_prompt.reshape(mp, d), x_sample.reshape(ns, d)], axis=0)
    conv_p, conv_s, pool_p, pool_s = [], [], [], []
    chunk_p, chunk_s, conf_p, conf_s = [], [], [], []
    for l in range(depth):
        i = l // 2
        if l % 2 == 0:
            proj = _norm_matmul(x, norm_mix[l], w_in_even[i])
            cat_p, cs_p, ps_p = _even_prompt(proj, w_conv_a[i], w_pool[i], pool_scale[i],
                                             batch=batch, seq=seq)
            cat_s, gx_s = _even_sample(proj, state_conv_a[i], state_pool[i],
                                       w_conv_a[i], w_pool[i], pool_scale[i], row0=mp)
            conv_p.append(cs_p)
            pool_p.append(ps_p)
            conv_s.append(_push(state_conv_a[i], gx_s))
            pool_s.append(_push(state_pool[i], proj[mp:, 3 * da:]))
            w_out = w_out_even[i]
        else:
            proj = _norm_matmul(x, norm_mix[l], w_in_odd[i])
            cat_p, cv_p, cf_p = _odd_prompt(
                proj, v_norm_g[i], v_norm_b[i], w_spatial[i], b_spatial[i],
                w_conv_d[i], b_conv_d[i], conf_norm_g[i], conf_norm_b[i],
                batch=batch, seq=seq)
            cat_s, vn_s, glu_s = _odd_sample(
                proj, state_conformer[i], v_norm_g[i], v_norm_b[i], w_spatial[i],
                b_spatial[i], w_conv_d[i], b_conv_d[i], conf_norm_g[i], conf_norm_b[i],
                row0=mp)
            chunk_p.append(cv_p)
            conf_p.append(cf_p)
            chunk_s.append(vn_s[:, None, :])
            conf_s.append(_push(state_conformer[i], glu_s))
            w_out = w_out_odd[i]
        cat = jnp.concatenate([cat_p, cat_s], axis=0)
        x = _matmul_res(cat, w_out, x, name="mixer_out_proj")
        h = _gate_up(x, norm_ffn[l], w_ffn_gate[l], w_ffn_up[l])
        x = _matmul_res(h, w_ffn_down[l], x, name="ffn_down")
    y = _final_norm(x, norm_final)
    return (y[:mp].reshape(batch, seq, d), y[mp:].reshape(ns, 1, d),
            jnp.stack(conv_p), jnp.stack(conv_s), jnp.stack(pool_p), jnp.stack(pool_s),
            jnp.stack(chunk_p), jnp.stack(chunk_s), jnp.stack(conf_p), jnp.stack(conf_s))
```

```python
import jax
import jax.numpy as jnp
from jax import lax
from jax.experimental import pallas as pl
from jax.experimental.pallas import tpu as pltpu

F32 = jnp.float32
BF16 = jnp.bfloat16

EPS = 1e-6
PAST_LEN = 16384
POOL_WINDOWS = (2, 4, 8, 16)
CHUNK = 128

LANES = 128
SUBLANES = 8
VMEM_LIMIT_BYTES = 56 * 1024 * 1024

ROW_TILE = 2080
DOWN_ROW_TILE = 1040
PROMPT_ROWS = 1024
MIX_TILE = 256
POOL_HALO = 16
CONF_HALO = 32


def _params(*sem):
    return pltpu.CompilerParams(dimension_semantics=sem,
                                vmem_limit_bytes=VMEM_LIMIT_BYTES)


def _rmsnorm(x, g):
    ms = jnp.mean(x * x, axis=-1, keepdims=True)
    return x * lax.rsqrt(ms + EPS) * g


def _layernorm(x, g, b):
    mu = jnp.mean(x, axis=-1, keepdims=True)
    xc = x - mu
    var = jnp.mean(xc * xc, axis=-1, keepdims=True)
    return xc * lax.rsqrt(var + EPS) * g + b


def _dot(a, b):
    return jnp.dot(a, b, preferred_element_type=F32)


def _lane_chunks(n):
    return [slice(c * LANES, (c + 1) * LANES) for c in range(n // LANES)]


def _gain_columns(g):
    return jnp.broadcast_to(g[:, None], (g.shape[0], LANES))


def _prep_kernel(x_ref, *refs):
    x_out, xb_out, ssq_out = refs[-3:]
    x = x_ref[...]
    x_out[...] = x
    xb_out[...] = x.astype(BF16)
    ssq_out[...] = jnp.broadcast_to(jnp.sum(x * x, axis=-1, keepdims=True),
                                    ssq_out.shape)


def _prep(x2d, *, rows, block0, total_rows, prev=None):
    n, d = x2d.shape
    shapes = [jax.ShapeDtypeStruct((total_rows, d), F32),
              jax.ShapeDtypeStruct((total_rows, d), BF16),
              jax.ShapeDtypeStruct((total_rows, LANES), F32)]
    in_specs = [pl.BlockSpec((rows, d), lambda i: (i, 0))]
    args = [x2d]
    aliases = {}
    if prev is not None:
        in_specs += [pl.BlockSpec(memory_space=pl.ANY)] * 3
        args += list(prev)
        aliases = {1: 0, 2: 1, 3: 2}
    return pl.pallas_call(
        _prep_kernel,
        grid=(n // rows,),
        in_specs=in_specs,
        out_specs=[pl.BlockSpec((rows, d), lambda i: (block0 + i, 0)),
                   pl.BlockSpec((rows, d), lambda i: (block0 + i, 0)),
                   pl.BlockSpec((rows, LANES), lambda i: (block0 + i, 0))],
        out_shape=shapes,
        input_output_aliases=aliases,
        compiler_params=_params("arbitrary"),
        name="stream_entry",
    )(*args)


def _scaled_weight(w_ref, g_ref, wb_ref):
    for sl in _lane_chunks(w_ref.shape[1]):
        wb_ref[:, sl] = (w_ref[:, sl] * g_ref[...]).astype(BF16)


def _row_scale(ssq_ref, d):
    return lax.rsqrt(ssq_ref[...] / d + EPS)


def _proj_kernel(xb_ref, ssq_ref, g_ref, w_ref, o_ref, wb_ref):
    _scaled_weight(w_ref, g_ref, wb_ref)
    r = _row_scale(ssq_ref, xb_ref.shape[1])
    acc = _dot(xb_ref[...], wb_ref[...])
    for sl in _lane_chunks(o_ref.shape[1]):
        o_ref[:, sl] = acc[:, sl] * r


def _proj(xb, ssq, g, w_stack, layer, *, tn=512):
    m, d = xb.shape
    n = w_stack.shape[2]
    return pl.pallas_call(
        _proj_kernel,
        grid=(m // ROW_TILE, n // tn),
        in_specs=[pl.BlockSpec((ROW_TILE, d), lambda i, j: (i, 0)),
                  pl.BlockSpec((ROW_TILE, LANES), lambda i, j: (i, 0)),
                  pl.BlockSpec((d, LANES), lambda i, j: (0, 0)),
                  pl.BlockSpec((None, d, tn), lambda i, j: (layer, 0, j))],
        out_specs=pl.BlockSpec((ROW_TILE, tn), lambda i, j: (i, j)),
        out_shape=jax.ShapeDtypeStruct((m, n), F32),
        scratch_shapes=[pltpu.VMEM((d, tn), BF16)],
        compiler_params=_params("parallel", "arbitrary"),
        name="mixer_in_proj",
    )(xb, ssq, _gain_columns(g), w_stack)


def _gate_up_kernel(xb_ref, ssq_ref, g_ref, wg_ref, wu_ref, h_ref, wgb_ref, wub_ref):
    _scaled_weight(wg_ref, g_ref, wgb_ref)
    _scaled_weight(wu_ref, g_ref, wub_ref)
    r = _row_scale(ssq_ref, xb_ref.shape[1])
    xb = xb_ref[...]
    gate = _dot(xb, wgb_ref[...])
    up = _dot(xb, wub_ref[...])
    for sl in _lane_chunks(h_ref.shape[1]):
        gt = gate[:, sl] * r
        h_ref[:, sl] = (gt * jax.nn.sigmoid(gt) * (up[:, sl] * r)).astype(BF16)


def _gate_up(xb, ssq, g, wg_stack, wu_stack, layer, *, tn=256):
    m, d = xb.shape
    f = wg_stack.shape[2]

    def wspec():
        return pl.BlockSpec((None, d, tn), lambda i, j: (layer, 0, j))

    return pl.pallas_call(
        _gate_up_kernel,
        grid=(m // ROW_TILE, f // tn),
        in_specs=[pl.BlockSpec((ROW_TILE, d), lambda i, j: (i, 0)),
                  pl.BlockSpec((ROW_TILE, LANES), lambda i, j: (i, 0)),
                  pl.BlockSpec((d, LANES), lambda i, j: (0, 0)),
                  wspec(), wspec()],
        out_specs=pl.BlockSpec((ROW_TILE, tn), lambda i, j: (i, j)),
        out_shape=jax.ShapeDtypeStruct((m, f), BF16),
        scratch_shapes=[pltpu.VMEM((d, tn), BF16), pltpu.VMEM((d, tn), BF16)],
        compiler_params=_params("parallel", "arbitrary"),
        name="ffn_gate_up",
    )(xb, ssq, _gain_columns(g), wg_stack, wu_stack)


def _matmul_res_kernel(a_ref, w_ref, r_ref, o_ref, ob_ref, ssq_ref):
    o = r_ref[...] + _dot(a_ref[...], w_ref[...].astype(BF16))
    o_ref[...] = o
    ob_ref[...] = o.astype(BF16)
    part = jnp.broadcast_to(jnp.sum(o * o, axis=-1, keepdims=True), ssq_ref.shape)

    @pl.when(pl.program_id(1) == 0)
    def _():
        ssq_ref[...] = part

    @pl.when(pl.program_id(1) > 0)
    def _():
        ssq_ref[...] += part


def _matmul_res(a, w_stack, layer, r, *, tm, tn=256, name):
    m, k = a.shape
    n = w_stack.shape[2]
    return pl.pallas_call(
        _matmul_res_kernel,
        grid=(m // tm, n // tn),
        in_specs=[pl.BlockSpec((tm, k), lambda i, j: (i, 0)),
                  pl.BlockSpec((None, k, tn), lambda i, j: (layer, 0, j)),
                  pl.BlockSpec((tm, tn), lambda i, j: (i, j))],
        out_specs=[pl.BlockSpec((tm, tn), lambda i, j: (i, j)),
                   pl.BlockSpec((tm, tn), lambda i, j: (i, j)),
                   pl.BlockSpec((tm, LANES), lambda i, j: (i, 0))],
        out_shape=[jax.ShapeDtypeStruct((m, n), F32),
                   jax.ShapeDtypeStruct((m, n), BF16),
                   jax.ShapeDtypeStruct((m, LANES), F32)],
        compiler_params=_params("parallel", "arbitrary"),
        name=name,
    )(a, w_stack, r)


def _final_norm_kernel(x_ref, g_ref, o_ref):
    o_ref[...] = _rmsnorm(x_ref[...], g_ref[...])


def _final_norm(x, g, *, rows, block0, nblocks):
    d = x.shape[1]
    return pl.pallas_call(
        _final_norm_kernel,
        grid=(nblocks,),
        in_specs=[pl.BlockSpec((rows, d), lambda i: (block0 + i, 0)),
                  pl.BlockSpec((1, d), lambda i: (0, 0))],
        out_specs=pl.BlockSpec((rows, d), lambda i: (i, 0)),
        out_shape=jax.ShapeDtypeStruct((rows * nblocks, d), F32),
        compiler_params=_params("parallel"),
        name="final_norm",
    )(x, g.reshape(1, d))


def _pool_project(pooled, w_pool_ref, scale_ref, cat_ref, col0):
    gw = w_pool_ref.shape[1]
    for g in range(len(POOL_WINDOWS)):
        sl = slice(g * gw, (g + 1) * gw)
        yb = _dot(pooled[:, sl].astype(BF16), w_pool_ref[g].astype(BF16))
        cat_ref[:, col0 + g * gw:col0 + (g + 1) * gw] = (
            yb * scale_ref[:, sl]).astype(cat_ref.dtype)


def _even_prompt_kernel(xin_ref, gpre_ref, gpost_ref, p_ref,
                        hxin_ref, hgpre_ref, hp_ref,
                        wconv_ref, wpool_ref, scale_ref,
                        cat_ref, conv_state_ref, pool_state_ref,
                        gx_ext, p_ext):
    tt, da = xin_ref.shape
    h = POOL_HALO
    i = pl.program_id(1)
    live = (i > 0).astype(F32)

    gx = gpre_ref[...] * xin_ref[...]
    gx_ext[0:h, :] = hgpre_ref[...] * hxin_ref[...] * live
    gx_ext[h:h + tt, :] = gx
    conv = (gx_ext[pl.ds(h - 2, tt), :] * wconv_ref[0:1, :]
            + gx_ext[pl.ds(h - 1, tt), :] * wconv_ref[1:2, :]
            + gx * wconv_ref[2:3, :])
    cat_ref[:, 0:da] = (gpost_ref[...] * conv).astype(cat_ref.dtype)
    conv_state_ref[...] = gx_ext[pl.ds(h + tt - 2, 2), :]

    p = p_ref[...]
    p_ext[0:h, :] = hp_ref[...] * live
    p_ext[h:h + tt, :] = p
    pos = i * tt + lax.broadcasted_iota(jnp.int32, (tt, 1), 0)
    gw = da // len(POOL_WINDOWS)
    pooled = []
    for g, w in enumerate(POOL_WINDOWS):
        sl = slice(g * gw, (g + 1) * gw)
        s = p[:, sl]
        for k in range(1, w):
            s = s + p_ext[pl.ds(h - k, tt), sl]
        cnt = jnp.minimum(w, pos + 1).astype(F32)
        pooled.append(s / cnt - p[:, sl])
    pooled = jnp.concatenate(pooled, axis=-1)
    _pool_project(pooled, wpool_ref, scale_ref, cat_ref, da)
    nh = pool_state_ref.shape[0]
    pool_state_ref[...] = p_ext[pl.ds(h + tt - nh, nh), :]


def _even_prompt(proj, w_conv, w_pool, pool_scale, layer, *, batch, seq):
    da = w_conv.shape[2]
    total_rows = proj.shape[0]
    tt = MIX_TILE
    nt = seq // tt
    hb = tt // POOL_HALO
    nwin = len(POOL_WINDOWS)
    nh = POOL_WINDOWS[-1] - 1

    def main(c):
        return pl.BlockSpec((tt, da), lambda b, i: (b * nt + i, c))

    def halo(c):
        return pl.BlockSpec(
            (POOL_HALO, da),
            lambda b, i: (jnp.maximum((b * nt + i) * hb - 1, 0), c))

    return pl.pallas_call(
        _even_prompt_kernel,
        grid=(batch, nt),
        in_specs=[main(0), main(1), main(2), main(3), halo(0), halo(1), halo(3),
                  pl.BlockSpec((None, 3, da), lambda b, i: (layer, 0, 0)),
                  pl.BlockSpec((None, nwin, da // nwin, da // nwin),
                               lambda b, i: (layer, 0, 0, 0)),
                  pl.BlockSpec((None, 1, da), lambda b, i: (layer, 0, 0))],
        out_specs=[pl.BlockSpec((tt, 2 * da), lambda b, i: (b * nt + i, 0)),
                   pl.BlockSpec((None, 2, da), lambda b, i: (b, 0, 0)),
                   pl.BlockSpec((None, nh, da), lambda b, i: (b, 0, 0))],
        out_shape=[jax.ShapeDtypeStruct((total_rows, 2 * da), BF16),
                   jax.ShapeDtypeStruct((batch, 2, da), F32),
                   jax.ShapeDtypeStruct((batch, nh, da), F32)],
        scratch_shapes=[pltpu.VMEM((POOL_HALO + tt, da), F32),
                        pltpu.VMEM((POOL_HALO + tt, da), F32)],
        compiler_params=_params("parallel", "arbitrary"),
        name="even_mixer_prompt",
    )(proj, proj, proj, proj, proj, proj, proj,
      w_conv, w_pool, pool_scale.reshape(pool_scale.shape[0], 1, da))


def _even_sample_kernel(xin_ref, gpre_ref, gpost_ref, p_ref,
                        conv_hist_ref, pool_hist_ref,
                        wconv_ref, wpool_ref, scale_ref, cat_in_ref,
                        cat_ref, gx_ref):
    del cat_in_ref
    s, da = xin_ref.shape
    gx = gpre_ref[...] * xin_ref[...]
    gx_ref[...] = gx
    nc = conv_hist_ref.shape[1]
    conv = (jnp.sum(conv_hist_ref[...] * wconv_ref[0:nc, :][None], axis=1)
            + gx * wconv_ref[nc:nc + 1, :])
    cat_ref[:, 0:da] = (gpost_ref[...] * conv).astype(cat_ref.dtype)

    p = p_ref[...]
    nh = pool_hist_ref.shape[1]
    gw = da // len(POOL_WINDOWS)
    k_idx = lax.broadcasted_iota(jnp.int32, (nh, da), 0)
    col = lax.broadcasted_iota(jnp.int32, (nh, da), 1)
    window = jnp.zeros((nh, da), jnp.int32)
    for g, w in enumerate(POOL_WINDOWS):
        window = jnp.where(col // gw == g, w, window)
    in_window = (nh - k_idx <= window - 1).astype(F32)
    ssum = jnp.sum(pool_hist_ref[...] * in_window[None], axis=1) + p
    cnt = jnp.minimum(window[0:1, :], PAST_LEN + 1).astype(F32)
    pooled = ssum / cnt - p
    _pool_project(pooled, wpool_ref, scale_ref, cat_ref, da)


def _even_sample(proj, conv_hist, pool_hist, w_conv, w_pool, pool_scale, cat, layer,
                 *, row0):
    s, nc, da = conv_hist.shape[1:]
    nh = pool_hist.shape[2]
    rb = row0 // s
    nwin = len(POOL_WINDOWS)

    def seg(c):
        return pl.BlockSpec((s, da), lambda i: (rb, c))

    return pl.pallas_call(
        _even_sample_kernel,
        grid=(1,),
        in_specs=[seg(0), seg(1), seg(2), seg(3),
                  pl.BlockSpec((None, s, nc, da), lambda i: (layer, 0, 0, 0)),
                  pl.BlockSpec((None, s, nh, da), lambda i: (layer, 0, 0, 0)),
                  pl.BlockSpec((None, nc + 1, da), lambda i: (layer, 0, 0)),
                  pl.BlockSpec((None, nwin, da // nwin, da // nwin),
                               lambda i: (layer, 0, 0, 0)),
                  pl.BlockSpec((None, 1, da), lambda i: (layer, 0, 0)),
                  pl.BlockSpec(memory_space=pl.ANY)],
        out_specs=[pl.BlockSpec((s, 2 * da), lambda i: (rb, 0)),
                   pl.BlockSpec((s, da), lambda i: (0, 0))],
        out_shape=[jax.ShapeDtypeStruct(cat.shape, cat.dtype),
                   jax.ShapeDtypeStruct((s, da), F32)],
        input_output_aliases={9: 0},
        compiler_params=_params("arbitrary"),
        name="even_mixer_sample",
    )(proj, proj, proj, proj, conv_hist, pool_hist,
      w_conv, w_pool, pool_scale.reshape(pool_scale.shape[0], 1, da), cat)


def _odd_prompt_kernel(u_ref, v_ref, a_ref, g_ref, ha_ref, hg_ref,
                       vng_ref, vnb_ref, ws_ref, bst_ref,
                       wconv_ref, bconv_ref, cng_ref, cnb_ref,
                       cat_ref, chunk_v_ref, conf_state_ref,
                       glu_ext, shifted):
    tt, dc = u_ref.shape
    h = CONF_HALO
    i = pl.program_id(1)
    live = (i > 0).astype(F32)

    u = jax.nn.gelu(u_ref[...])
    vn = _layernorm(jax.nn.gelu(v_ref[...]), vng_ref[...], vnb_ref[...])
    chunk_v_ref[...] = vn[tt - CHUNK:tt, :]
    nheads = ws_ref.shape[0]
    hd = dc // nheads
    row = lax.broadcasted_iota(jnp.int32, (CHUNK, CHUNK), 0)
    colm = lax.broadcasted_iota(jnp.int32, (CHUNK, CHUNK), 1)
    vb = vn.astype(BF16)
    for hh in range(nheads):
        ws = jnp.where(colm <= row, ws_ref[hh], 0.0).astype(BF16)
        bias = bst_ref[:, hh:hh + 1]
        for c in range(tt // CHUNK):
            rs = slice(c * CHUNK, (c + 1) * CHUNK)
            cs = slice(hh * hd, (hh + 1) * hd)
            mixed = _dot(ws, vb[rs, cs]) + bias
            cat_ref[rs, cs] = (u[rs, cs] * mixed).astype(cat_ref.dtype)

    glu = a_ref[...] * jax.nn.sigmoid(g_ref[...])
    glu_ext[0:h, :] = ha_ref[...] * jax.nn.sigmoid(hg_ref[...]) * live
    glu_ext[h:h + tt, :] = glu
    span = shifted.shape[1]
    for s in range(1, SUBLANES):
        shifted[s - 1] = glu_ext[pl.ds(s, span), :]
    nw = wconv_ref.shape[0]
    acc = glu * wconv_ref[nw - 1:nw, :]
    for k in range(nw - 1):
        q, s = divmod(h - (nw - 1) + k, SUBLANES)
        if s == 0:
            tap = glu_ext[pl.ds(q * SUBLANES, tt), :]
        else:
            tap = shifted[s - 1, pl.ds(q * SUBLANES, tt), :]
        acc = acc + tap * wconv_ref[k:k + 1, :]
    y = _layernorm(acc + bconv_ref[...], cng_ref[...], cnb_ref[...])
    cat_ref[:, dc:2 * dc] = (y * jax.nn.sigmoid(y)).astype(cat_ref.dtype)
    conf_state_ref[...] = glu_ext[pl.ds(h + tt - (nw - 1), nw - 1), :]


def _odd_prompt(proj, v_norm_g, v_norm_b, w_spatial, b_spatial_t, w_conv, b_conv,
                conf_norm_g, conf_norm_b, layer, *, batch, seq):
    nw, dc = w_conv.shape[1:]
    nheads = w_spatial.shape[1]
    total_rows = proj.shape[0]
    tt = MIX_TILE
    nt = seq // tt
    hb = tt // CONF_HALO

    def main(c):
        return pl.BlockSpec((tt, dc), lambda b, i: (b * nt + i, c))

    def halo(c):
        return pl.BlockSpec(
            (CONF_HALO, dc),
            lambda b, i: (jnp.maximum((b * nt + i) * hb - 1, 0), c))

    def vec():
        return pl.BlockSpec((None, 1, dc), lambda b, i: (layer, 0, 0))

    def as_rows(v):
        return v.reshape(v.shape[0], 1, dc)

    return pl.pallas_call(
        _odd_prompt_kernel,
        grid=(batch, nt),
        in_specs=[main(0), main(1), main(2), main(3), halo(2), halo(3),
                  vec(), vec(),
                  pl.BlockSpec((None, nheads, CHUNK, CHUNK), lambda b, i: (layer, 0, 0, 0)),
                  pl.BlockSpec((None, CHUNK, nheads), lambda b, i: (layer, 0, 0)),
                  pl.BlockSpec((None, nw, dc), lambda b, i: (layer, 0, 0)),
                  vec(), vec(), vec()],
        out_specs=[pl.BlockSpec((tt, 2 * dc), lambda b, i: (b * nt + i, 0)),
                   pl.BlockSpec((None, CHUNK, dc), lambda b, i: (b, 0, 0)),
                   pl.BlockSpec((None, nw - 1, dc), lambda b, i: (b, 0, 0))],
        out_shape=[jax.ShapeDtypeStruct((total_rows, 2 * dc), BF16),
                   jax.ShapeDtypeStruct((batch, CHUNK, dc), F32),
                   jax.ShapeDtypeStruct((batch, nw - 1, dc), F32)],
        scratch_shapes=[pltpu.VMEM((CONF_HALO + tt, dc), F32),
                        pltpu.VMEM((SUBLANES - 1, CONF_HALO - SUBLANES + tt, dc), F32)],
        compiler_params=_params("parallel", "arbitrary"),
        name="odd_mixer_prompt",
    )(proj, proj, proj, proj, proj, proj,
      as_rows(v_norm_g), as_rows(v_norm_b), w_spatial, b_spatial_t,
      w_conv, as_rows(b_conv), as_rows(conf_norm_g), as_rows(conf_norm_b))


def _odd_sample_kernel(u_ref, v_ref, a_ref, g_ref, hist_ref,
                       vng_ref, vnb_ref, ws0_ref, b0_ref,
                       wconv_ref, bconv_ref, cng_ref, cnb_ref, cat_in_ref,
                       cat_ref, vn_ref, glu_ref):
    del cat_in_ref
    s, dc = u_ref.shape
    u = jax.nn.gelu(u_ref[...])
    vn = _layernorm(jax.nn.gelu(v_ref[...]), vng_ref[...], vnb_ref[...])
    vn_ref[...] = vn
    mixed = ws0_ref[...] * vn + b0_ref[...]
    cat_ref[:, 0:dc] = (u * mixed).astype(cat_ref.dtype)

    glu = a_ref[...] * jax.nn.sigmoid(g_ref[...])
    glu_ref[...] = glu
    nh = hist_ref.shape[1]
    acc = (jnp.sum(hist_ref[...] * wconv_ref[0:nh, :][None], axis=1)
           + glu * wconv_ref[nh:nh + 1, :])
    y = _layernorm(acc + bconv_ref[...], cng_ref[...], cnb_ref[...])
    cat_ref[:, dc:2 * dc] = (y * jax.nn.sigmoid(y)).astype(cat_ref.dtype)


def _odd_sample(proj, hist, v_norm_g, v_norm_b, w_spatial, b_spatial, w_conv, b_conv,
                conf_norm_g, conf_norm_b, cat, layer, *, row0):
    s, nh, dc = hist.shape[1:]
    nheads = w_spatial.shape[1]
    hd = dc // nheads
    rb = row0 // s
    ws0 = jnp.repeat(w_spatial[layer, :, 0, 0], hd).reshape(1, dc)
    b0 = jnp.repeat(b_spatial[layer, :, 0], hd).reshape(1, dc)

    def seg(c):
        return pl.BlockSpec((s, dc), lambda i: (rb, c))

    def vec():
        return pl.BlockSpec((None, 1, dc), lambda i: (layer, 0, 0))

    def row():
        return pl.BlockSpec((1, dc), lambda i: (0, 0))

    def as_rows(v):
        return v.reshape(v.shape[0], 1, dc)

    return pl.pallas_call(
        _odd_sample_kernel,
        grid=(1,),
        in_specs=[seg(0), seg(1), seg(2), seg(3),
                  pl.BlockSpec((None, s, nh, dc), lambda i: (layer, 0, 0, 0)),
                  vec(), vec(), row(), row(),
                  pl.BlockSpec((None, nh + 1, dc), lambda i: (layer, 0, 0)),
                  vec(), vec(), vec(),
                  pl.BlockSpec(memory_space=pl.ANY)],
        out_specs=[pl.BlockSpec((s, 2 * dc), lambda i: (rb, 0)),
                   pl.BlockSpec((s, dc), lambda i: (0, 0)),
                   pl.BlockSpec((s, dc), lambda i: (0, 0))],
        out_shape=[jax.ShapeDtypeStruct(cat.shape, cat.dtype),
                   jax.ShapeDtypeStruct((s, dc), F32),
                   jax.ShapeDtypeStruct((s, dc), F32)],
        input_output_aliases={13: 0},
        compiler_params=_params("arbitrary"),
        name="odd_mixer_sample",
    )(proj, proj, proj, proj, hist,
      as_rows(v_norm_g), as_rows(v_norm_b), ws0, b0,
      w_conv, as_rows(b_conv), as_rows(conf_norm_g), as_rows(conf_norm_b), cat)


def _push(hist, new_row):
    return jnp.concatenate([hist[:, 1:], new_row[:, None, :]], axis=1)


def kernel(x_prompt, x_sample, state_conv_a, state_pool, state_conformer, norm_mix, norm_ffn, w_in_even, w_conv_a, w_pool, pool_scale, w_out_even, w_in_odd, v_norm_g, v_norm_b, w_spatial, b_spatial, w_conv_d, b_conv_d, conf_norm_g, conf_norm_b, w_out_odd, w_ffn_gate, w_ffn_up, w_ffn_down, norm_final):
    batch, seq, d = x_prompt.shape
    ns = x_sample.shape[0]
    mp = batch * seq
    m = mp + ns
    depth = norm_mix.shape[0]
    da = w_conv_a.shape[2]
    assert x_sample.shape[1] == 1 and mp % ns == 0 and mp % PROMPT_ROWS == 0
    assert m % ROW_TILE == 0 and m % DOWN_ROW_TILE == 0
    assert seq % MIX_TILE == 0 and MIX_TILE % CHUNK == 0

    stream = _prep(x_prompt.reshape(mp, d), rows=PROMPT_ROWS, block0=0, total_rows=m)
    x, xb, ssq = _prep(x_sample.reshape(ns, d), rows=ns, block0=mp // ns, total_rows=m,
                       prev=stream)
    b_spatial_t = jnp.swapaxes(b_spatial, 1, 2)
    conv_p, conv_s, pool_p, pool_s = [], [], [], []
    chunk_p, chunk_s, conf_p, conf_s = [], [], [], []
    for l in range(depth):
        i = l // 2
        if l % 2 == 0:
            proj = _proj(xb, ssq, norm_mix[l], w_in_even, i)
            cat, cs_p, ps_p = _even_prompt(proj, w_conv_a, w_pool, pool_scale, i,
                                           batch=batch, seq=seq)
            cat, gx_s = _even_sample(proj, state_conv_a, state_pool,
                                     w_conv_a, w_pool, pool_scale, cat, i, row0=mp)
            conv_p.append(cs_p)
            pool_p.append(ps_p)
            conv_s.append(_push(state_conv_a[i], gx_s))
            pool_s.append(_push(state_pool[i], proj[mp:, 3 * da:]))
            w_out = w_out_even
        else:
            proj = _proj(xb, ssq, norm_mix[l], w_in_odd, i)
            cat, cv_p, cf_p = _odd_prompt(
                proj, v_norm_g, v_norm_b, w_spatial, b_spatial_t,
                w_conv_d, b_conv_d, conf_norm_g, conf_norm_b, i, batch=batch, seq=seq)
            cat, vn_s, glu_s = _odd_sample(
                proj, state_conformer, v_norm_g, v_norm_b, w_spatial,
                b_spatial, w_conv_d, b_conv_d, conf_norm_g, conf_norm_b, cat, i, row0=mp)
            chunk_p.append(cv_p)
            conf_p.append(cf_p)
            chunk_s.append(vn_s[:, None, :])
            conf_s.append(_push(state_conformer[i], glu_s))
            w_out = w_out_odd
        x, xb, ssq = _matmul_res(cat, w_out, i, x, tm=ROW_TILE, name="mixer_out_proj")
        h = _gate_up(xb, ssq, norm_ffn[l], w_ffn_gate, w_ffn_up, l)
        x, xb, ssq = _matmul_res(h, w_ffn_down, l, x, tm=DOWN_ROW_TILE, name="ffn_down")
    y_p = _final_norm(x, norm_final, rows=PROMPT_ROWS, block0=0, nblocks=mp // PROMPT_ROWS)
    y_s = _final_norm(x, norm_final, rows=ns, block0=mp // ns, nblocks=1)
    return (y_p.reshape(batch, seq, d), y_s.reshape(ns, 1, d),
            jnp.stack(conv_p), jnp.stack(conv_s), jnp.stack(pool_p), jnp.stack(pool_s),
            jnp.stack(chunk_p), jnp.stack(chunk_s), jnp.stack(conf_p), jnp.stack(conf_s))
```

```python
import jax
import jax.numpy as jnp
from jax import lax
from jax.experimental import pallas as pl
from jax.experimental.pallas import tpu as pltpu

F32 = jnp.float32
BF16 = jnp.bfloat16

EPS = 1e-6
PAST_LEN = 16384
POOL_WINDOWS = (2, 4, 8, 16)
CHUNK = 128

LANES = 128
SUBLANES = 8
VMEM_LIMIT_BYTES = 56 * 1024 * 1024

WIDE_ROW_TILE = 4160
ROW_TILE = 2080
DOWN_ROW_TILE = 1664
PROMPT_ROWS = 1024
MIX_TILE = 256
POOL_HALO = 16
CONF_HALO = 32


def _params(*sem):
    return pltpu.CompilerParams(dimension_semantics=sem,
                                vmem_limit_bytes=VMEM_LIMIT_BYTES)


def _rmsnorm(x, g):
    ms = jnp.mean(x * x, axis=-1, keepdims=True)
    return x * lax.rsqrt(ms + EPS) * g


def _layernorm(x, g, b):
    mu = jnp.mean(x, axis=-1, keepdims=True)
    xc = x - mu
    var = jnp.mean(xc * xc, axis=-1, keepdims=True)
    return xc * lax.rsqrt(var + EPS) * g + b


def _dot(a, b):
    return jnp.dot(a, b, preferred_element_type=F32)


def _lane_chunks(n):
    return [slice(c * LANES, (c + 1) * LANES) for c in range(n // LANES)]


def _gain_columns(g):
    return jnp.broadcast_to(g[:, None], (g.shape[0], LANES))


def _prep_kernel(x_ref, *refs):
    x_out, xb_out, ssq_out = refs[-3:]
    x = x_ref[...]
    x_out[...] = x
    xb_out[...] = x.astype(BF16)
    ssq_out[...] = jnp.broadcast_to(jnp.sum(x * x, axis=-1, keepdims=True),
                                    ssq_out.shape)


def _prep(x2d, *, rows, block0, total_rows, prev=None):
    n, d = x2d.shape
    shapes = [jax.ShapeDtypeStruct((total_rows, d), F32),
              jax.ShapeDtypeStruct((total_rows, d), BF16),
              jax.ShapeDtypeStruct((total_rows, LANES), F32)]
    in_specs = [pl.BlockSpec((rows, d), lambda i: (i, 0))]
    args = [x2d]
    aliases = {}
    if prev is not None:
        in_specs += [pl.BlockSpec(memory_space=pl.ANY)] * 3
        args += list(prev)
        aliases = {1: 0, 2: 1, 3: 2}
    return pl.pallas_call(
        _prep_kernel,
        grid=(n // rows,),
        in_specs=in_specs,
        out_specs=[pl.BlockSpec((rows, d), lambda i: (block0 + i, 0)),
                   pl.BlockSpec((rows, d), lambda i: (block0 + i, 0)),
                   pl.BlockSpec((rows, LANES), lambda i: (block0 + i, 0))],
        out_shape=shapes,
        input_output_aliases=aliases,
        compiler_params=_params("arbitrary"),
        name="stream_entry",
    )(*args)


def _resident_rows(tm, k):
    return pl.BlockSpec((tm, k), lambda i, j: (i, 0), pipeline_mode=pl.Buffered(1))


def _scaled_weight(w_ref, g_ref, wb_ref):
    for sl in _lane_chunks(w_ref.shape[1]):
        wb_ref[:, sl] = (w_ref[:, sl] * g_ref[...]).astype(BF16)


def _row_scale(ssq_ref, d):
    return lax.rsqrt(ssq_ref[...] / d + EPS)


def _proj_kernel(xb_ref, ssq_ref, g_ref, w_ref, o_ref, wb_ref):
    _scaled_weight(w_ref, g_ref, wb_ref)
    r = _row_scale(ssq_ref, xb_ref.shape[1])
    acc = _dot(xb_ref[...], wb_ref[...])
    for sl in _lane_chunks(o_ref.shape[1]):
        o_ref[:, sl] = (acc[:, sl] * r).astype(o_ref.dtype)


def _proj(xb, ssq, g, w_stack, layer, *, tn=512):
    m, d = xb.shape
    n = w_stack.shape[2]
    return pl.pallas_call(
        _proj_kernel,
        grid=(m // WIDE_ROW_TILE, n // tn),
        in_specs=[_resident_rows(WIDE_ROW_TILE, d),
                  pl.BlockSpec((WIDE_ROW_TILE, LANES), lambda i, j: (i, 0)),
                  pl.BlockSpec((d, LANES), lambda i, j: (0, 0)),
                  pl.BlockSpec((None, d, tn), lambda i, j: (layer, 0, j))],
        out_specs=pl.BlockSpec((WIDE_ROW_TILE, tn), lambda i, j: (i, j)),
        out_shape=jax.ShapeDtypeStruct((m, n), BF16),
        scratch_shapes=[pltpu.VMEM((d, tn), BF16)],
        compiler_params=_params("parallel", "arbitrary"),
        name="mixer_in_proj",
    )(xb, ssq, _gain_columns(g), w_stack)


def _gate_up_kernel(xb_ref, ssq_ref, g_ref, wg_ref, wu_ref, h_ref, wgb_ref, wub_ref):
    _scaled_weight(wg_ref, g_ref, wgb_ref)
    _scaled_weight(wu_ref, g_ref, wub_ref)
    r = _row_scale(ssq_ref, xb_ref.shape[1])
    xb = xb_ref[...]
    gate = _dot(xb, wgb_ref[...])
    up = _dot(xb, wub_ref[...])
    for sl in _lane_chunks(h_ref.shape[1]):
        gt = gate[:, sl] * r
        h_ref[:, sl] = (gt * jax.nn.sigmoid(gt) * (up[:, sl] * r)).astype(BF16)


def _gate_up(xb, ssq, g, wg_stack, wu_stack, layer, *, tn=256):
    m, d = xb.shape
    f = wg_stack.shape[2]

    def wspec():
        return pl.BlockSpec((None, d, tn), lambda i, j: (layer, 0, j))

    return pl.pallas_call(
        _gate_up_kernel,
        grid=(m // WIDE_ROW_TILE, f // tn),
        in_specs=[_resident_rows(WIDE_ROW_TILE, d),
                  pl.BlockSpec((WIDE_ROW_TILE, LANES), lambda i, j: (i, 0)),
                  pl.BlockSpec((d, LANES), lambda i, j: (0, 0)),
                  wspec(), wspec()],
        out_specs=pl.BlockSpec((WIDE_ROW_TILE, tn), lambda i, j: (i, j)),
        out_shape=jax.ShapeDtypeStruct((m, f), BF16),
        scratch_shapes=[pltpu.VMEM((d, tn), BF16), pltpu.VMEM((d, tn), BF16)],
        compiler_params=_params("parallel", "arbitrary"),
        name="ffn_gate_up",
    )(xb, ssq, _gain_columns(g), wg_stack, wu_stack)


def _matmul_res_kernel(a_ref, w_ref, r_ref, o_ref, ob_ref, ssq_ref):
    o = r_ref[...] + _dot(a_ref[...], w_ref[...].astype(BF16))
    o_ref[...] = o
    ob_ref[...] = o.astype(BF16)
    part = jnp.broadcast_to(jnp.sum(o * o, axis=-1, keepdims=True), ssq_ref.shape)

    @pl.when(pl.program_id(1) == 0)
    def _():
        ssq_ref[...] = part

    @pl.when(pl.program_id(1) > 0)
    def _():
        ssq_ref[...] += part


def _matmul_res(a, w_stack, layer, r, *, tm, tn=256, name):
    m, k = a.shape
    n = w_stack.shape[2]
    return pl.pallas_call(
        _matmul_res_kernel,
        grid=(m // tm, n // tn),
        in_specs=[_resident_rows(tm, k),
                  pl.BlockSpec((None, k, tn), lambda i, j: (layer, 0, j)),
                  pl.BlockSpec((tm, tn), lambda i, j: (i, j))],
        out_specs=[pl.BlockSpec((tm, tn), lambda i, j: (i, j)),
                   pl.BlockSpec((tm, tn), lambda i, j: (i, j)),
                   pl.BlockSpec((tm, LANES), lambda i, j: (i, 0))],
        out_shape=[jax.ShapeDtypeStruct((m, n), F32),
                   jax.ShapeDtypeStruct((m, n), BF16),
                   jax.ShapeDtypeStruct((m, LANES), F32)],
        compiler_params=_params("parallel", "arbitrary"),
        name=name,
    )(a, w_stack, r)


def _final_norm_kernel(x_ref, g_ref, o_ref):
    o_ref[...] = _rmsnorm(x_ref[...], g_ref[...])


def _final_norm(x, g, *, rows, block0, nblocks):
    d = x.shape[1]
    return pl.pallas_call(
        _final_norm_kernel,
        grid=(nblocks,),
        in_specs=[pl.BlockSpec((rows, d), lambda i: (block0 + i, 0)),
                  pl.BlockSpec((1, d), lambda i: (0, 0))],
        out_specs=pl.BlockSpec((rows, d), lambda i: (i, 0)),
        out_shape=jax.ShapeDtypeStruct((rows * nblocks, d), F32),
        compiler_params=_params("parallel"),
        name="final_norm",
    )(x, g.reshape(1, d))


def _pool_project(pooled, w_pool_ref, scale_ref, cat_ref, col0):
    gw = w_pool_ref.shape[1]
    for g in range(len(POOL_WINDOWS)):
        sl = slice(g * gw, (g + 1) * gw)
        yb = _dot(pooled[:, sl].astype(BF16), w_pool_ref[g].astype(BF16))
        cat_ref[:, col0 + g * gw:col0 + (g + 1) * gw] = (
            yb * scale_ref[:, sl]).astype(cat_ref.dtype)


def _even_prompt_kernel(xin_ref, gpre_ref, gpost_ref, p_ref,
                        hxin_ref, hgpre_ref, hp_ref,
                        wconv_ref, wpool_ref, scale_ref,
                        cat_ref, conv_state_ref, pool_state_ref,
                        gx_ext, p_ext):
    tt, da = xin_ref.shape
    h = POOL_HALO
    i = pl.program_id(1)
    live = (i > 0).astype(F32)

    gx = gpre_ref[...].astype(F32) * xin_ref[...].astype(F32)
    gx_ext[0:h, :] = hgpre_ref[...].astype(F32) * hxin_ref[...].astype(F32) * live
    gx_ext[h:h + tt, :] = gx
    conv = (gx_ext[pl.ds(h - 2, tt), :] * wconv_ref[0:1, :]
            + gx_ext[pl.ds(h - 1, tt), :] * wconv_ref[1:2, :]
            + gx * wconv_ref[2:3, :])
    cat_ref[:, 0:da] = (gpost_ref[...].astype(F32) * conv).astype(cat_ref.dtype)
    conv_state_ref[...] = gx_ext[pl.ds(h + tt - 2, 2), :]

    p = p_ref[...].astype(F32)
    p_ext[0:h, :] = hp_ref[...].astype(F32) * live
    p_ext[h:h + tt, :] = p
    pos = i * tt + lax.broadcasted_iota(jnp.int32, (tt, 1), 0)
    gw = da // len(POOL_WINDOWS)
    pooled = []
    for g, w in enumerate(POOL_WINDOWS):
        sl = slice(g * gw, (g + 1) * gw)
        s = p[:, sl]
        for k in range(1, w):
            s = s + p_ext[pl.ds(h - k, tt), sl]
        cnt = jnp.minimum(w, pos + 1).astype(F32)
        pooled.append(s / cnt - p[:, sl])
    pooled = jnp.concatenate(pooled, axis=-1)
    _pool_project(pooled, wpool_ref, scale_ref, cat_ref, da)
    nh = pool_state_ref.shape[0]
    pool_state_ref[...] = p_ext[pl.ds(h + tt - nh, nh), :]


def _even_prompt(proj, w_conv, w_pool, pool_scale, layer, *, batch, seq):
    da = w_conv.shape[2]
    total_rows = proj.shape[0]
    tt = MIX_TILE
    nt = seq // tt
    hb = tt // POOL_HALO
    nwin = len(POOL_WINDOWS)
    nh = POOL_WINDOWS[-1] - 1

    def main(c):
        return pl.BlockSpec((tt, da), lambda b, i: (b * nt + i, c))

    def halo(c):
        return pl.BlockSpec(
            (POOL_HALO, da),
            lambda b, i: (jnp.maximum((b * nt + i) * hb - 1, 0), c))

    return pl.pallas_call(
        _even_prompt_kernel,
        grid=(batch, nt),
        in_specs=[main(0), main(1), main(2), main(3), halo(0), halo(1), halo(3),
                  pl.BlockSpec((None, 3, da), lambda b, i: (layer, 0, 0)),
                  pl.BlockSpec((None, nwin, da // nwin, da // nwin),
                               lambda b, i: (layer, 0, 0, 0)),
                  pl.BlockSpec((None, 1, da), lambda b, i: (layer, 0, 0))],
        out_specs=[pl.BlockSpec((tt, 2 * da), lambda b, i: (b * nt + i, 0)),
                   pl.BlockSpec((None, 2, da), lambda b, i: (b, 0, 0)),
                   pl.BlockSpec((None, nh, da), lambda b, i: (b, 0, 0))],
        out_shape=[jax.ShapeDtypeStruct((total_rows, 2 * da), BF16),
                   jax.ShapeDtypeStruct((batch, 2, da), F32),
                   jax.ShapeDtypeStruct((batch, nh, da), F32)],
        scratch_shapes=[pltpu.VMEM((POOL_HALO + tt, da), F32),
                        pltpu.VMEM((POOL_HALO + tt, da), F32)],
        compiler_params=_params("parallel", "arbitrary"),
        name="even_mixer_prompt",
    )(proj, proj, proj, proj, proj, proj, proj,
      w_conv, w_pool, pool_scale.reshape(pool_scale.shape[0], 1, da))


def _even_sample_kernel(xin_ref, gpre_ref, gpost_ref, p_ref,
                        conv_hist_ref, pool_hist_ref,
                        wconv_ref, wpool_ref, scale_ref, cat_in_ref,
                        cat_ref, gx_ref):
    del cat_in_ref
    s, da = xin_ref.shape
    gx = gpre_ref[...].astype(F32) * xin_ref[...].astype(F32)
    gx_ref[...] = gx
    nc = conv_hist_ref.shape[1]
    conv = (jnp.sum(conv_hist_ref[...] * wconv_ref[0:nc, :][None], axis=1)
            + gx * wconv_ref[nc:nc + 1, :])
    cat_ref[:, 0:da] = (gpost_ref[...].astype(F32) * conv).astype(cat_ref.dtype)

    p = p_ref[...].astype(F32)
    nh = pool_hist_ref.shape[1]
    gw = da // len(POOL_WINDOWS)
    k_idx = lax.broadcasted_iota(jnp.int32, (nh, da), 0)
    col = lax.broadcasted_iota(jnp.int32, (nh, da), 1)
    window = jnp.zeros((nh, da), jnp.int32)
    for g, w in enumerate(POOL_WINDOWS):
        window = jnp.where(col // gw == g, w, window)
    in_window = (nh - k_idx <= window - 1).astype(F32)
    ssum = jnp.sum(pool_hist_ref[...] * in_window[None], axis=1) + p
    cnt = jnp.minimum(window[0:1, :], PAST_LEN + 1).astype(F32)
    pooled = ssum / cnt - p
    _pool_project(pooled, wpool_ref, scale_ref, cat_ref, da)


def _even_sample(proj, conv_hist, pool_hist, w_conv, w_pool, pool_scale, cat, layer,
                 *, row0):
    s, nc, da = conv_hist.shape[1:]
    nh = pool_hist.shape[2]
    rb = row0 // s
    nwin = len(POOL_WINDOWS)

    def seg(c):
        return pl.BlockSpec((s, da), lambda i: (rb, c))

    return pl.pallas_call(
        _even_sample_kernel,
        grid=(1,),
        in_specs=[seg(0), seg(1), seg(2), seg(3),
                  pl.BlockSpec((None, s, nc, da), lambda i: (layer, 0, 0, 0)),
                  pl.BlockSpec((None, s, nh, da), lambda i: (layer, 0, 0, 0)),
                  pl.BlockSpec((None, nc + 1, da), lambda i: (layer, 0, 0)),
                  pl.BlockSpec((None, nwin, da // nwin, da // nwin),
                               lambda i: (layer, 0, 0, 0)),
                  pl.BlockSpec((None, 1, da), lambda i: (layer, 0, 0)),
                  pl.BlockSpec(memory_space=pl.ANY)],
        out_specs=[pl.BlockSpec((s, 2 * da), lambda i: (rb, 0)),
                   pl.BlockSpec((s, da), lambda i: (0, 0))],
        out_shape=[jax.ShapeDtypeStruct(cat.shape, cat.dtype),
                   jax.ShapeDtypeStruct((s, da), F32)],
        input_output_aliases={9: 0},
        compiler_params=_params("arbitrary"),
        name="even_mixer_sample",
    )(proj, proj, proj, proj, conv_hist, pool_hist,
      w_conv, w_pool, pool_scale.reshape(pool_scale.shape[0], 1, da), cat)


def _odd_prompt_kernel(u_ref, v_ref, a_ref, g_ref, ha_ref, hg_ref,
                       vng_ref, vnb_ref, ws_ref, bst_ref,
                       wconv_ref, bconv_ref, cng_ref, cnb_ref,
                       cat_ref, chunk_v_ref, conf_state_ref,
                       glu_ext, shifted):
    tt, dc = u_ref.shape
    h = CONF_HALO
    i = pl.program_id(1)
    live = (i > 0).astype(F32)

    u = jax.nn.gelu(u_ref[...].astype(F32))
    vn = _layernorm(jax.nn.gelu(v_ref[...].astype(F32)), vng_ref[...], vnb_ref[...])
    chunk_v_ref[...] = vn[tt - CHUNK:tt, :]
    nheads = ws_ref.shape[0]
    hd = dc // nheads
    row = lax.broadcasted_iota(jnp.int32, (CHUNK, CHUNK), 0)
    colm = lax.broadcasted_iota(jnp.int32, (CHUNK, CHUNK), 1)
    vb = vn.astype(BF16)
    for hh in range(nheads):
        ws = jnp.where(colm <= row, ws_ref[hh], 0.0).astype(BF16)
        bias = bst_ref[:, hh:hh + 1]
        for c in range(tt // CHUNK):
            rs = slice(c * CHUNK, (c + 1) * CHUNK)
            cs = slice(hh * hd, (hh + 1) * hd)
            mixed = _dot(ws, vb[rs, cs]) + bias
            cat_ref[rs, cs] = (u[rs, cs] * mixed).astype(cat_ref.dtype)

    glu = a_ref[...].astype(F32) * jax.nn.sigmoid(g_ref[...].astype(F32))
    glu_ext[0:h, :] = (ha_ref[...].astype(F32)
                       * jax.nn.sigmoid(hg_ref[...].astype(F32)) * live)
    glu_ext[h:h + tt, :] = glu
    span = shifted.shape[1]
    for s in range(1, SUBLANES):
        shifted[s - 1] = glu_ext[pl.ds(s, span), :]
    nw = wconv_ref.shape[0]
    acc = glu * wconv_ref[nw - 1:nw, :]
    for k in range(nw - 1):
        q, s = divmod(h - (nw - 1) + k, SUBLANES)
        if s == 0:
            tap = glu_ext[pl.ds(q * SUBLANES, tt), :]
        else:
            tap = shifted[s - 1, pl.ds(q * SUBLANES, tt), :]
        acc = acc + tap * wconv_ref[k:k + 1, :]
    y = _layernorm(acc + bconv_ref[...], cng_ref[...], cnb_ref[...])
    cat_ref[:, dc:2 * dc] = (y * jax.nn.sigmoid(y)).astype(cat_ref.dtype)
    conf_state_ref[...] = glu_ext[pl.ds(h + tt - (nw - 1), nw - 1), :]


def _odd_prompt(proj, v_norm_g, v_norm_b, w_spatial, b_spatial_t, w_conv, b_conv,
                conf_norm_g, conf_norm_b, layer, *, batch, seq):
    nw, dc = w_conv.shape[1:]
    nheads = w_spatial.shape[1]
    total_rows = proj.shape[0]
    tt = MIX_TILE
    nt = seq // tt
    hb = tt // CONF_HALO

    def main(c):
        return pl.BlockSpec((tt, dc), lambda b, i: (b * nt + i, c))

    def halo(c):
        return pl.BlockSpec(
            (CONF_HALO, dc),
            lambda b, i: (jnp.maximum((b * nt + i) * hb - 1, 0), c))

    def vec():
        return pl.BlockSpec((None, 1, dc), lambda b, i: (layer, 0, 0))

    def as_rows(v):
        return v.reshape(v.shape[0], 1, dc)

    return pl.pallas_call(
        _odd_prompt_kernel,
        grid=(batch, nt),
        in_specs=[main(0), main(1), main(2), main(3), halo(2), halo(3),
                  vec(), vec(),
                  pl.BlockSpec((None, nheads, CHUNK, CHUNK), lambda b, i: (layer, 0, 0, 0)),
                  pl.BlockSpec((None, CHUNK, nheads), lambda b, i: (layer, 0, 0)),
                  pl.BlockSpec((None, nw, dc), lambda b, i: (layer, 0, 0)),
                  vec(), vec(), vec()],
        out_specs=[pl.BlockSpec((tt, 2 * dc), lambda b, i: (b * nt + i, 0)),
                   pl.BlockSpec((None, CHUNK, dc), lambda b, i: (b, 0, 0)),
                   pl.BlockSpec((None, nw - 1, dc), lambda b, i: (b, 0, 0))],
        out_shape=[jax.ShapeDtypeStruct((total_rows, 2 * dc), BF16),
                   jax.ShapeDtypeStruct((batch, CHUNK, dc), F32),
                   jax.ShapeDtypeStruct((batch, nw - 1, dc), F32)],
        scratch_shapes=[pltpu.VMEM((CONF_HALO + tt, dc), F32),
                        pltpu.VMEM((SUBLANES - 1, CONF_HALO - SUBLANES + tt, dc), F32)],
        compiler_params=_params("parallel", "arbitrary"),
        name="odd_mixer_prompt",
    )(proj, proj, proj, proj, proj, proj,
      as_rows(v_norm_g), as_rows(v_norm_b), w_spatial, b_spatial_t,
      w_conv, as_rows(b_conv), as_rows(conf_norm_g), as_rows(conf_norm_b))


def _odd_sample_kernel(u_ref, v_ref, a_ref, g_ref, hist_ref,
                       vng_ref, vnb_ref, ws0_ref, b0_ref,
                       wconv_ref, bconv_ref, cng_ref, cnb_ref, cat_in_ref,
                       cat_ref, vn_ref, glu_ref):
    del cat_in_ref
    s, dc = u_ref.shape
    u = jax.nn.gelu(u_ref[...].astype(F32))
    vn = _layernorm(jax.nn.gelu(v_ref[...].astype(F32)), vng_ref[...], vnb_ref[...])
    vn_ref[...] = vn
    mixed = ws0_ref[...] * vn + b0_ref[...]
    cat_ref[:, 0:dc] = (u * mixed).astype(cat_ref.dtype)

    glu = a_ref[...].astype(F32) * jax.nn.sigmoid(g_ref[...].astype(F32))
    glu_ref[...] = glu
    nh = hist_ref.shape[1]
    acc = (jnp.sum(hist_ref[...] * wconv_ref[0:nh, :][None], axis=1)
           + glu * wconv_ref[nh:nh + 1, :])
    y = _layernorm(acc + bconv_ref[...], cng_ref[...], cnb_ref[...])
    cat_ref[:, dc:2 * dc] = (y * jax.nn.sigmoid(y)).astype(cat_ref.dtype)


def _odd_sample(proj, hist, v_norm_g, v_norm_b, w_spatial, b_spatial, w_conv, b_conv,
                conf_norm_g, conf_norm_b, cat, layer, *, row0):
    s, nh, dc = hist.shape[1:]
    nheads = w_spatial.shape[1]
    hd = dc // nheads
    rb = row0 // s
    ws0 = jnp.repeat(w_spatial[layer, :, 0, 0], hd).reshape(1, dc)
    b0 = jnp.repeat(b_spatial[layer, :, 0], hd).reshape(1, dc)

    def seg(c):
        return pl.BlockSpec((s, dc), lambda i: (rb, c))

    def vec():
        return pl.BlockSpec((None, 1, dc), lambda i: (layer, 0, 0))

    def row():
        return pl.BlockSpec((1, dc), lambda i: (0, 0))

    def as_rows(v):
        return v.reshape(v.shape[0], 1, dc)

    return pl.pallas_call(
        _odd_sample_kernel,
        grid=(1,),
        in_specs=[seg(0), seg(1), seg(2), seg(3),
                  pl.BlockSpec((None, s, nh, dc), lambda i: (layer, 0, 0, 0)),
                  vec(), vec(), row(), row(),
                  pl.BlockSpec((None, nh + 1, dc), lambda i: (layer, 0, 0)),
                  vec(), vec(), vec(),
                  pl.BlockSpec(memory_space=pl.ANY)],
        out_specs=[pl.BlockSpec((s, 2 * dc), lambda i: (rb, 0)),
                   pl.BlockSpec((s, dc), lambda i: (0, 0)),
                   pl.BlockSpec((s, dc), lambda i: (0, 0))],
        out_shape=[jax.ShapeDtypeStruct(cat.shape, cat.dtype),
                   jax.ShapeDtypeStruct((s, dc), F32),
                   jax.ShapeDtypeStruct((s, dc), F32)],
        input_output_aliases={13: 0},
        compiler_params=_params("arbitrary"),
        name="odd_mixer_sample",
    )(proj, proj, proj, proj, hist,
      as_rows(v_norm_g), as_rows(v_norm_b), ws0, b0,
      w_conv, as_rows(b_conv), as_rows(conf_norm_g), as_rows(conf_norm_b), cat)


def _push(hist, new_row):
    return jnp.concatenate([hist[:, 1:], new_row[:, None, :]], axis=1)


def kernel(x_prompt, x_sample, state_conv_a, state_pool, state_conformer, norm_mix, norm_ffn, w_in_even, w_conv_a, w_pool, pool_scale, w_out_even, w_in_odd, v_norm_g, v_norm_b, w_spatial, b_spatial, w_conv_d, b_conv_d, conf_norm_g, conf_norm_b, w_out_odd, w_ffn_gate, w_ffn_up, w_ffn_down, norm_final):
    batch, seq, d = x_prompt.shape
    ns = x_sample.shape[0]
    mp = batch * seq
    m = mp + ns
    depth = norm_mix.shape[0]
    da = w_conv_a.shape[2]
    assert x_sample.shape[1] == 1 and mp % ns == 0 and mp % PROMPT_ROWS == 0
    assert m % ROW_TILE == 0 and m % WIDE_ROW_TILE == 0 and m % DOWN_ROW_TILE == 0
    assert seq % MIX_TILE == 0 and MIX_TILE % CHUNK == 0

    stream = _prep(x_prompt.reshape(mp, d), rows=PROMPT_ROWS, block0=0, total_rows=m)
    x, xb, ssq = _prep(x_sample.reshape(ns, d), rows=ns, block0=mp // ns, total_rows=m,
                       prev=stream)
    b_spatial_t = jnp.swapaxes(b_spatial, 1, 2)
    conv_p, conv_s, pool_p, pool_s = [], [], [], []
    chunk_p, chunk_s, conf_p, conf_s = [], [], [], []
    for l in range(depth):
        i = l // 2
        if l % 2 == 0:
            proj = _proj(xb, ssq, norm_mix[l], w_in_even, i)
            cat, cs_p, ps_p = _even_prompt(proj, w_conv_a, w_pool, pool_scale, i,
                                           batch=batch, seq=seq)
            cat, gx_s = _even_sample(proj, state_conv_a, state_pool,
                                     w_conv_a, w_pool, pool_scale, cat, i, row0=mp)
            conv_p.append(cs_p)
            pool_p.append(ps_p)
            conv_s.append(_push(state_conv_a[i], gx_s))
            pool_s.append(_push(state_pool[i], proj[mp:, 3 * da:].astype(F32)))
            w_out = w_out_even
        else:
            proj = _proj(xb, ssq, norm_mix[l], w_in_odd, i)
            cat, cv_p, cf_p = _odd_prompt(
                proj, v_norm_g, v_norm_b, w_spatial, b_spatial_t,
                w_conv_d, b_conv_d, conf_norm_g, conf_norm_b, i, batch=batch, seq=seq)
            cat, vn_s, glu_s = _odd_sample(
                proj, state_conformer, v_norm_g, v_norm_b, w_spatial,
                b_spatial, w_conv_d, b_conv_d, conf_norm_g, conf_norm_b, cat, i, row0=mp)
            chunk_p.append(cv_p)
            conf_p.append(cf_p)
            chunk_s.append(vn_s[:, None, :])
            conf_s.append(_push(state_conformer[i], glu_s))
            w_out = w_out_odd
        x, xb, ssq = _matmul_res(cat, w_out, i, x, tm=ROW_TILE, name="mixer_out_proj")
        h = _gate_up(xb, ssq, norm_ffn[l], w_ffn_gate, w_ffn_up, l)
        x, xb, ssq = _matmul_res(h, w_ffn_down, l, x, tm=DOWN_ROW_TILE, name="ffn_down")
    y_p = _final_norm(x, norm_final, rows=PROMPT_ROWS, block0=0, nblocks=mp // PROMPT_ROWS)
    y_s = _final_norm(x, norm_final, rows=ns, block0=mp // ns, nblocks=1)
    return (y_p.reshape(batch, seq, d), y_s.reshape(ns, 1, d),
            jnp.stack(conv_p), jnp.stack(conv_s), jnp.stack(pool_p), jnp.stack(pool_s),
            jnp.stack(chunk_p), jnp.stack(chunk_s), jnp.stack(conf_p), jnp.stack(conf_s))
```

```python
import functools

import jax
import jax.numpy as jnp
from jax import lax
from jax.experimental import pallas as pl
from jax.experimental.pallas import tpu as pltpu

F32 = jnp.float32
BF16 = jnp.bfloat16

EPS = 1e-6
PAST_LEN = 16384
POOL_WINDOWS = (2, 4, 8, 16)
CHUNK = 128

LANES = 128
SUBLANES = 8
VMEM_LIMIT_BYTES = 56 * 1024 * 1024

WIDE_ROW_TILE = 4160
DOWN_ROW_TILE = 1040
PROMPT_ROWS = 1024
MIX_TILE = 256
POOL_HALO = 16
CONF_HALO = 32


def _params(*sem):
    return pltpu.CompilerParams(dimension_semantics=sem,
                                vmem_limit_bytes=VMEM_LIMIT_BYTES)


def _rmsnorm(x, g):
    ms = jnp.mean(x * x, axis=-1, keepdims=True)
    return x * lax.rsqrt(ms + EPS) * g


def _layernorm(x, g, b):
    mu = jnp.mean(x, axis=-1, keepdims=True)
    xc = x - mu
    var = jnp.mean(xc * xc, axis=-1, keepdims=True)
    return xc * lax.rsqrt(var + EPS) * g + b


def _dot(a, b):
    return jnp.dot(a, b, preferred_element_type=F32)


def _lane_chunks(n):
    return [slice(c * LANES, (c + 1) * LANES) for c in range(n // LANES)]


def _gain_columns(g):
    return jnp.broadcast_to(g[:, None], (g.shape[0], LANES))


def _emit_stream(o, x_out, xb_out, ssq_out):
    x_out[...] = o
    xb_out[...] = o.astype(BF16)
    ssq_out[...] = jnp.broadcast_to(jnp.sum(o * o, axis=-1, keepdims=True),
                                    ssq_out.shape)


def _stream_shapes(total_rows, d):
    return [jax.ShapeDtypeStruct((total_rows, d), F32),
            jax.ShapeDtypeStruct((total_rows, d), BF16),
            jax.ShapeDtypeStruct((total_rows, LANES), F32)]


def _prep_kernel(x_ref, *refs):
    _emit_stream(x_ref[...], *refs[-3:])


def _prep(x2d, *, rows, block0, total_rows, prev=None):
    n, d = x2d.shape
    in_specs = [pl.BlockSpec((rows, d), lambda i: (i, 0))]
    args = [x2d]
    aliases = {}
    if prev is not None:
        in_specs += [pl.BlockSpec(memory_space=pl.ANY)] * 3
        args += list(prev)
        aliases = {1: 0, 2: 1, 3: 2}
    return pl.pallas_call(
        _prep_kernel,
        grid=(n // rows,),
        in_specs=in_specs,
        out_specs=[pl.BlockSpec((rows, d), lambda i: (block0 + i, 0)),
                   pl.BlockSpec((rows, d), lambda i: (block0 + i, 0)),
                   pl.BlockSpec((rows, LANES), lambda i: (block0 + i, 0))],
        out_shape=_stream_shapes(total_rows, d),
        input_output_aliases=aliases,
        compiler_params=_params("arbitrary"),
        name="stream_entry",
    )(*args)


def _resident_rows(tm, k):
    return pl.BlockSpec((tm, k), lambda i, j: (i, 0), pipeline_mode=pl.Buffered(1))


def _scaled_weight(w_ref, g_ref, wb_ref):
    for sl in _lane_chunks(w_ref.shape[1]):
        wb_ref[:, sl] = (w_ref[:, sl] * g_ref[...]).astype(BF16)


def _row_scale(ssq_ref, d):
    return lax.rsqrt(ssq_ref[...] / d + EPS)


def _proj_kernel(xb_ref, ssq_ref, g_ref, w_ref, o_ref, wb_ref):
    _scaled_weight(w_ref, g_ref, wb_ref)
    r = _row_scale(ssq_ref, xb_ref.shape[1])
    acc = _dot(xb_ref[...], wb_ref[...])
    for sl in _lane_chunks(o_ref.shape[1]):
        o_ref[:, sl] = (acc[:, sl] * r).astype(o_ref.dtype)


def _proj(xb, ssq, g, w_stack, layer, *, tn=512):
    m, d = xb.shape
    n = w_stack.shape[2]
    return pl.pallas_call(
        _proj_kernel,
        grid=(m // WIDE_ROW_TILE, n // tn),
        in_specs=[_resident_rows(WIDE_ROW_TILE, d),
                  pl.BlockSpec((WIDE_ROW_TILE, LANES), lambda i, j: (i, 0)),
                  pl.BlockSpec((d, LANES), lambda i, j: (0, 0)),
                  pl.BlockSpec((None, d, tn), lambda i, j: (layer, 0, j))],
        out_specs=pl.BlockSpec((WIDE_ROW_TILE, tn), lambda i, j: (i, j)),
        out_shape=jax.ShapeDtypeStruct((m, n), BF16),
        scratch_shapes=[pltpu.VMEM((d, tn), BF16)],
        compiler_params=_params("parallel", "arbitrary"),
        name="mixer_in_proj",
    )(xb, ssq, _gain_columns(g), w_stack)


def _gate_up_kernel(xb_ref, ssq_ref, g_ref, wg_ref, wu_ref, h_ref, wgb_ref, wub_ref):
    _scaled_weight(wg_ref, g_ref, wgb_ref)
    _scaled_weight(wu_ref, g_ref, wub_ref)
    r = _row_scale(ssq_ref, xb_ref.shape[1])
    xb = xb_ref[...]
    gate = _dot(xb, wgb_ref[...])
    up = _dot(xb, wub_ref[...])
    for sl in _lane_chunks(h_ref.shape[1]):
        gt = gate[:, sl] * r
        h_ref[:, sl] = (gt * jax.nn.sigmoid(gt) * (up[:, sl] * r)).astype(BF16)


def _gate_up(xb, ssq, g, wg_stack, wu_stack, layer, *, tn=256):
    m, d = xb.shape
    f = wg_stack.shape[2]

    def wspec():
        return pl.BlockSpec((None, d, tn), lambda i, j: (layer, 0, j))

    return pl.pallas_call(
        _gate_up_kernel,
        grid=(m // WIDE_ROW_TILE, f // tn),
        in_specs=[_resident_rows(WIDE_ROW_TILE, d),
                  pl.BlockSpec((WIDE_ROW_TILE, LANES), lambda i, j: (i, 0)),
                  pl.BlockSpec((d, LANES), lambda i, j: (0, 0)),
                  wspec(), wspec()],
        out_specs=pl.BlockSpec((WIDE_ROW_TILE, tn), lambda i, j: (i, j)),
        out_shape=jax.ShapeDtypeStruct((m, f), BF16),
        scratch_shapes=[pltpu.VMEM((d, tn), BF16), pltpu.VMEM((d, tn), BF16)],
        compiler_params=_params("parallel", "arbitrary"),
        name="ffn_gate_up",
    )(xb, ssq, _gain_columns(g), wg_stack, wu_stack)


def _matmul_res_kernel(a_ref, w_ref, r_ref, o_ref, ob_ref, ssq_ref):
    o = r_ref[...] + _dot(a_ref[...], w_ref[...].astype(BF16))
    o_ref[...] = o
    ob_ref[...] = o.astype(BF16)
    part = jnp.broadcast_to(jnp.sum(o * o, axis=-1, keepdims=True), ssq_ref.shape)

    @pl.when(pl.program_id(1) == 0)
    def _():
        ssq_ref[...] = part

    @pl.when(pl.program_id(1) > 0)
    def _():
        ssq_ref[...] += part


def _matmul_res(a, w_stack, layer, r, *, tm, tn=256, name):
    m, k = a.shape
    n = w_stack.shape[2]
    return pl.pallas_call(
        _matmul_res_kernel,
        grid=(m // tm, n // tn),
        in_specs=[pl.BlockSpec((tm, k), lambda i, j: (i, 0)),
                  pl.BlockSpec((None, k, tn), lambda i, j: (layer, 0, j)),
                  pl.BlockSpec((tm, tn), lambda i, j: (i, j))],
        out_specs=[pl.BlockSpec((tm, tn), lambda i, j: (i, j)),
                   pl.BlockSpec((tm, tn), lambda i, j: (i, j)),
                   pl.BlockSpec((tm, LANES), lambda i, j: (i, 0))],
        out_shape=_stream_shapes(m, n),
        compiler_params=_params("parallel", "arbitrary"),
        name=name,
    )(a, w_stack, r)


def _cast_kernel(w_ref, o_ref):
    o_ref[...] = w_ref[...].astype(o_ref.dtype)


def _cast_bf16(w_stack, layer, *, rows=512):
    k, n = w_stack.shape[1:]
    return pl.pallas_call(
        _cast_kernel,
        grid=(k // rows,),
        in_specs=[pl.BlockSpec((None, rows, n), lambda i: (layer, i, 0))],
        out_specs=pl.BlockSpec((rows, n), lambda i: (i, 0)),
        out_shape=jax.ShapeDtypeStruct((k, n), BF16),
        compiler_params=_params("parallel"),
        name="out_proj_weight_bf16",
    )(w_stack)


def _final_norm_kernel(x_ref, g_ref, o_ref):
    o_ref[...] = _rmsnorm(x_ref[...], g_ref[...])


def _final_norm(x, g, *, rows, block0, nblocks):
    d = x.shape[1]
    return pl.pallas_call(
        _final_norm_kernel,
        grid=(nblocks,),
        in_specs=[pl.BlockSpec((rows, d), lambda i: (block0 + i, 0)),
                  pl.BlockSpec((1, d), lambda i: (0, 0))],
        out_specs=pl.BlockSpec((rows, d), lambda i: (i, 0)),
        out_shape=jax.ShapeDtypeStruct((rows * nblocks, d), F32),
        compiler_params=_params("parallel"),
        name="final_norm",
    )(x, g.reshape(1, d))


PROJ_CHUNK = 256


def _pipelined_tail(mixer_stages, cat_a, cat_b, wob_ref, xres_ref, stream_outs, *,
                    interleave):
    s = pl.program_id(0)
    x_out, xb_out, ssq_out = stream_outs
    nchunks = wob_ref.shape[1] // PROJ_CHUNK

    @pl.when(s == 0)
    def _():
        cat_b[...] = jnp.zeros_like(cat_b)

    def step(cat_w, cat_r):
        stages = mixer_stages(cat_w)
        if not interleave:
            for stage in stages:
                stage()
            o = xres_ref[...] + _dot(cat_r[...], wob_ref[...])
            _emit_stream(o, *stream_outs)
            return
        ssq = jnp.zeros((x_out.shape[0], 1), F32)
        done = 0
        for k, stage in enumerate(stages):
            upto = (k + 1) * nchunks // len(stages)
            for c in range(done, upto):
                sl = slice(c * PROJ_CHUNK, (c + 1) * PROJ_CHUNK)
                o = xres_ref[:, sl] + _dot(cat_r[...], wob_ref[:, sl])
                x_out[:, sl] = o
                xb_out[:, sl] = o.astype(BF16)
                ssq = ssq + jnp.sum(o * o, axis=-1, keepdims=True)
            done = upto
            stage()
        ssq_out[...] = jnp.broadcast_to(ssq, ssq_out.shape)

    @pl.when(s % 2 == 0)
    def _():
        step(cat_a, cat_b)

    @pl.when(s % 2 == 1)
    def _():
        step(cat_b, cat_a)


def _tail_specs(ntiles, tt, d):
    def mixed(s):
        return jnp.minimum(s, ntiles - 1)

    def projected(s):
        return jnp.maximum(s - 1, 0)

    wob = pl.BlockSpec((d, d), lambda s: (0, 0), pipeline_mode=pl.Buffered(1))
    xres = pl.BlockSpec((tt, d), lambda s: (projected(s), 0))
    outs = [pl.BlockSpec((tt, d), lambda s: (projected(s), 0)),
            pl.BlockSpec((tt, d), lambda s: (projected(s), 0)),
            pl.BlockSpec((tt, LANES), lambda s: (projected(s), 0))]
    return mixed, wob, xres, outs


def _pool_project(pooled, w_pool_ref, scale_ref, cat_ref, col0):
    gw = w_pool_ref.shape[1]
    for g in range(len(POOL_WINDOWS)):
        sl = slice(g * gw, (g + 1) * gw)
        yb = _dot(pooled[:, sl].astype(BF16), w_pool_ref[g].astype(BF16))
        cat_ref[:, col0 + g * gw:col0 + (g + 1) * gw] = (
            yb * scale_ref[:, sl]).astype(cat_ref.dtype)


def _even_prompt_kernel(nt, ntiles,
                        xin_ref, gpre_ref, gpost_ref, p_ref,
                        hxin_ref, hgpre_ref, hp_ref,
                        wconv_ref, wpool_ref, scale_ref, wob_ref, xres_ref,
                        x_out, xb_out, ssq_out, conv_state_ref, pool_state_ref,
                        gx_ext, p_ext, cat_a, cat_b):
    tt, da = xin_ref.shape
    h = POOL_HALO
    i = jnp.minimum(pl.program_id(0), ntiles - 1) % nt
    live = (i > 0).astype(F32)

    def mixer_stages(cat_ref):
        def short_conv():
            gx = gpre_ref[...].astype(F32) * xin_ref[...].astype(F32)
            gx_ext[0:h, :] = (hgpre_ref[...].astype(F32) * hxin_ref[...].astype(F32)
                              * live)
            gx_ext[h:h + tt, :] = gx
            conv = (gx_ext[pl.ds(h - 2, tt), :] * wconv_ref[0:1, :]
                    + gx_ext[pl.ds(h - 1, tt), :] * wconv_ref[1:2, :]
                    + gx * wconv_ref[2:3, :])
            cat_ref[:, 0:da] = (gpost_ref[...].astype(F32) * conv).astype(cat_ref.dtype)
            conv_state_ref[...] = gx_ext[pl.ds(h + tt - 2, 2), :]
            p_ext[0:h, :] = hp_ref[...].astype(F32) * live
            p_ext[h:h + tt, :] = p_ref[...].astype(F32)
            nh = pool_state_ref.shape[0]
            pool_state_ref[...] = p_ext[pl.ds(h + tt - nh, nh), :]

        def pool_group(g, w):
            def run():
                gw = da // len(POOL_WINDOWS)
                sl = slice(g * gw, (g + 1) * gw)
                pg = p_ext[pl.ds(h, tt), sl]
                acc = pg
                for k in range(1, w):
                    acc = acc + p_ext[pl.ds(h - k, tt), sl]
                pos = i * tt + lax.broadcasted_iota(jnp.int32, (tt, 1), 0)
                cnt = jnp.minimum(w, pos + 1).astype(F32)
                pooled = acc / cnt - pg
                yb = _dot(pooled.astype(BF16), wpool_ref[g].astype(BF16))
                cat_ref[:, da + g * gw:da + (g + 1) * gw] = (
                    yb * scale_ref[:, sl]).astype(cat_ref.dtype)
            return run

        return [short_conv] + [pool_group(g, w) for g, w in enumerate(POOL_WINDOWS)]

    _pipelined_tail(mixer_stages, cat_a, cat_b, wob_ref, xres_ref,
                    (x_out, xb_out, ssq_out), interleave=True)


def _even_prompt(proj, x, w_conv, w_pool, pool_scale, wob, layer, *, batch, seq):
    da = w_conv.shape[2]
    total_rows, d = x.shape
    tt = MIX_TILE
    nt = seq // tt
    ntiles = batch * nt
    hb = tt // POOL_HALO
    nwin = len(POOL_WINDOWS)
    nh = POOL_WINDOWS[-1] - 1
    mixed, wob_spec, xres_spec, stream_specs = _tail_specs(ntiles, tt, d)

    def main(c):
        return pl.BlockSpec((tt, da), lambda s: (mixed(s), c))

    def halo(c):
        return pl.BlockSpec((POOL_HALO, da),
                            lambda s: (jnp.maximum(mixed(s) * hb - 1, 0), c))

    return pl.pallas_call(
        functools.partial(_even_prompt_kernel, nt, ntiles),
        grid=(ntiles + 1,),
        in_specs=[main(0), main(1), main(2), main(3), halo(0), halo(1), halo(3),
                  pl.BlockSpec((None, 3, da), lambda s: (layer, 0, 0)),
                  pl.BlockSpec((None, nwin, da // nwin, da // nwin),
                               lambda s: (layer, 0, 0, 0)),
                  pl.BlockSpec((None, 1, da), lambda s: (layer, 0, 0)),
                  wob_spec, xres_spec],
        out_specs=stream_specs + [
            pl.BlockSpec((None, 2, da), lambda s: (mixed(s) // nt, 0, 0)),
            pl.BlockSpec((None, nh, da), lambda s: (mixed(s) // nt, 0, 0))],
        out_shape=_stream_shapes(total_rows, d) + [
            jax.ShapeDtypeStruct((batch, 2, da), F32),
            jax.ShapeDtypeStruct((batch, nh, da), F32)],
        scratch_shapes=[pltpu.VMEM((POOL_HALO + tt, da), F32),
                        pltpu.VMEM((POOL_HALO + tt, da), F32),
                        pltpu.VMEM((tt, d), BF16),
                        pltpu.VMEM((tt, d), BF16)],
        compiler_params=_params("arbitrary"),
        name="even_mixer_prompt",
    )(proj, proj, proj, proj, proj, proj, proj,
      w_conv, w_pool, pool_scale.reshape(pool_scale.shape[0], 1, da), wob, x)


def _odd_prompt_kernel(nt, ntiles,
                       u_ref, v_ref, a_ref, g_ref, ha_ref, hg_ref,
                       vng_ref, vnb_ref, ws_ref, bst_ref,
                       wconv_ref, bconv_ref, cng_ref, cnb_ref, wob_ref, xres_ref,
                       x_out, xb_out, ssq_out, chunk_v_ref, conf_state_ref,
                       glu_ext, shifted, cat_a, cat_b):
    tt, dc = u_ref.shape
    h = CONF_HALO
    i = jnp.minimum(pl.program_id(0), ntiles - 1) % nt
    live = (i > 0).astype(F32)

    nheads = ws_ref.shape[0]
    hd = dc // nheads
    nw = wconv_ref.shape[0]

    def mixer_stages(cat_ref):
        vals = {}

        def gate_inputs():
            vals["u"] = jax.nn.gelu(u_ref[...].astype(F32))
            vn = _layernorm(jax.nn.gelu(v_ref[...].astype(F32)), vng_ref[...],
                            vnb_ref[...])
            chunk_v_ref[...] = vn[tt - CHUNK:tt, :]
            vals["vb"] = vn.astype(BF16)

        def gate_heads(heads):
            def run():
                row = lax.broadcasted_iota(jnp.int32, (CHUNK, CHUNK), 0)
                colm = lax.broadcasted_iota(jnp.int32, (CHUNK, CHUNK), 1)
                for hh in heads:
                    ws = jnp.where(colm <= row, ws_ref[hh], 0.0).astype(BF16)
                    bias = bst_ref[:, hh:hh + 1]
                    for c in range(tt // CHUNK):
                        rs = slice(c * CHUNK, (c + 1) * CHUNK)
                        cs = slice(hh * hd, (hh + 1) * hd)
                        mixed = _dot(ws, vals["vb"][rs, cs]) + bias
                        cat_ref[rs, cs] = (vals["u"][rs, cs] * mixed).astype(cat_ref.dtype)
            return run

        def glu_rows():
            glu = a_ref[...].astype(F32) * jax.nn.sigmoid(g_ref[...].astype(F32))
            glu_ext[0:h, :] = (ha_ref[...].astype(F32)
                               * jax.nn.sigmoid(hg_ref[...].astype(F32)) * live)
            glu_ext[h:h + tt, :] = glu
            conf_state_ref[...] = glu_ext[pl.ds(h + tt - (nw - 1), nw - 1), :]
            span = shifted.shape[1]
            for sh in range(1, SUBLANES):
                shifted[sh - 1] = glu_ext[pl.ds(sh, span), :]
            vals["acc"] = glu * wconv_ref[nw - 1:nw, :]

        def taps(ks):
            def run():
                acc = vals["acc"]
                for k in ks:
                    q, sh = divmod(h - (nw - 1) + k, SUBLANES)
                    if sh == 0:
                        tap = glu_ext[pl.ds(q * SUBLANES, tt), :]
                    else:
                        tap = shifted[sh - 1, pl.ds(q * SUBLANES, tt), :]
                    acc = acc + tap * wconv_ref[k:k + 1, :]
                vals["acc"] = acc
            return run

        def conv_norm():
            y = _layernorm(vals["acc"] + bconv_ref[...], cng_ref[...], cnb_ref[...])
            cat_ref[:, dc:2 * dc] = (y * jax.nn.sigmoid(y)).astype(cat_ref.dtype)

        third = (nw - 1) // 3
        return [gate_inputs,
                gate_heads(range(0, nheads // 2)), gate_heads(range(nheads // 2, nheads)),
                glu_rows,
                taps(range(0, third)), taps(range(third, 2 * third)),
                taps(range(2 * third, nw - 1)),
                conv_norm]

    _pipelined_tail(mixer_stages, cat_a, cat_b, wob_ref, xres_ref,
                    (x_out, xb_out, ssq_out), interleave=False)


def _odd_prompt(proj, x, v_norm_g, v_norm_b, w_spatial, b_spatial_t, w_conv, b_conv,
                conf_norm_g, conf_norm_b, wob, layer, *, batch, seq):
    nw, dc = w_conv.shape[1:]
    nheads = w_spatial.shape[1]
    total_rows, d = x.shape
    tt = MIX_TILE
    nt = seq // tt
    ntiles = batch * nt
    hb = tt // CONF_HALO
    mixed, wob_spec, xres_spec, stream_specs = _tail_specs(ntiles, tt, d)

    def main(c):
        return pl.BlockSpec((tt, dc), lambda s: (mixed(s), c))

    def halo(c):
        return pl.BlockSpec((CONF_HALO, dc),
                            lambda s: (jnp.maximum(mixed(s) * hb - 1, 0), c))

    def vec():
        return pl.BlockSpec((None, 1, dc), lambda s: (layer, 0, 0))

    def as_rows(v):
        return v.reshape(v.shape[0], 1, dc)

    return pl.pallas_call(
        functools.partial(_odd_prompt_kernel, nt, ntiles),
        grid=(ntiles + 1,),
        in_specs=[main(0), main(1), main(2), main(3), halo(2), halo(3),
                  vec(), vec(),
                  pl.BlockSpec((None, nheads, CHUNK, CHUNK), lambda s: (layer, 0, 0, 0)),
                  pl.BlockSpec((None, CHUNK, nheads), lambda s: (layer, 0, 0)),
                  pl.BlockSpec((None, nw, dc), lambda s: (layer, 0, 0)),
                  vec(), vec(), vec(), wob_spec, xres_spec],
        out_specs=stream_specs + [
            pl.BlockSpec((None, CHUNK, dc), lambda s: (mixed(s) // nt, 0, 0)),
            pl.BlockSpec((None, nw - 1, dc), lambda s: (mixed(s) // nt, 0, 0))],
        out_shape=_stream_shapes(total_rows, d) + [
            jax.ShapeDtypeStruct((batch, CHUNK, dc), F32),
            jax.ShapeDtypeStruct((batch, nw - 1, dc), F32)],
        scratch_shapes=[pltpu.VMEM((CONF_HALO + tt, dc), F32),
                        pltpu.VMEM((SUBLANES - 1, CONF_HALO - SUBLANES + tt, dc), F32),
                        pltpu.VMEM((tt, d), BF16),
                        pltpu.VMEM((tt, d), BF16)],
        compiler_params=_params("arbitrary"),
        name="odd_mixer_prompt",
    )(proj, proj, proj, proj, proj, proj,
      as_rows(v_norm_g), as_rows(v_norm_b), w_spatial, b_spatial_t,
      w_conv, as_rows(b_conv), as_rows(conf_norm_g), as_rows(conf_norm_b), wob, x)


def _even_sample_kernel(xin_ref, gpre_ref, gpost_ref, p_ref,
                        conv_hist_ref, pool_hist_ref,
                        wconv_ref, wpool_ref, scale_ref, wob_ref, xres_ref,
                        x_in, xb_in, ssq_in,
                        x_out, xb_out, ssq_out, gx_ref, cat_ref):
    del x_in, xb_in, ssq_in
    s, da = xin_ref.shape
    gx = gpre_ref[...].astype(F32) * xin_ref[...].astype(F32)
    gx_ref[...] = gx
    nc = conv_hist_ref.shape[1]
    conv = (jnp.sum(conv_hist_ref[...] * wconv_ref[0:nc, :][None], axis=1)
            + gx * wconv_ref[nc:nc + 1, :])
    cat_ref[:, 0:da] = (gpost_ref[...].astype(F32) * conv).astype(cat_ref.dtype)

    p = p_ref[...].astype(F32)
    nh = pool_hist_ref.shape[1]
    gw = da // len(POOL_WINDOWS)
    k_idx = lax.broadcasted_iota(jnp.int32, (nh, da), 0)
    col = lax.broadcasted_iota(jnp.int32, (nh, da), 1)
    window = jnp.zeros((nh, da), jnp.int32)
    for g, w in enumerate(POOL_WINDOWS):
        window = jnp.where(col // gw == g, w, window)
    in_window = (nh - k_idx <= window - 1).astype(F32)
    ssum = jnp.sum(pool_hist_ref[...] * in_window[None], axis=1) + p
    cnt = jnp.minimum(window[0:1, :], PAST_LEN + 1).astype(F32)
    pooled = ssum / cnt - p
    _pool_project(pooled, wpool_ref, scale_ref, cat_ref, da)
    o = xres_ref[...] + _dot(cat_ref[...], wob_ref[...])
    _emit_stream(o, x_out, xb_out, ssq_out)


def _sample_stream_specs(s, d, rb):
    wob = pl.BlockSpec((d, d), lambda i: (0, 0), pipeline_mode=pl.Buffered(1))
    xres = pl.BlockSpec((s, d), lambda i: (rb, 0))
    anys = [pl.BlockSpec(memory_space=pl.ANY)] * 3
    outs = [pl.BlockSpec((s, d), lambda i: (rb, 0)),
            pl.BlockSpec((s, d), lambda i: (rb, 0)),
            pl.BlockSpec((s, LANES), lambda i: (rb, 0))]
    return wob, xres, anys, outs


def _even_sample(proj, x, stream, conv_hist, pool_hist, w_conv, w_pool, pool_scale, wob,
                 layer, *, row0):
    s, nc, da = conv_hist.shape[1:]
    nh = pool_hist.shape[2]
    total_rows, d = x.shape
    rb = row0 // s
    nwin = len(POOL_WINDOWS)
    wob_spec, xres_spec, any_specs, stream_specs = _sample_stream_specs(s, d, rb)

    def seg(c):
        return pl.BlockSpec((s, da), lambda i: (rb, c))

    return pl.pallas_call(
        _even_sample_kernel,
        grid=(1,),
        in_specs=[seg(0), seg(1), seg(2), seg(3),
                  pl.BlockSpec((None, s, nc, da), lambda i: (layer, 0, 0, 0)),
                  pl.BlockSpec((None, s, nh, da), lambda i: (layer, 0, 0, 0)),
                  pl.BlockSpec((None, nc + 1, da), lambda i: (layer, 0, 0)),
                  pl.BlockSpec((None, nwin, da // nwin, da // nwin),
                               lambda i: (layer, 0, 0, 0)),
                  pl.BlockSpec((None, 1, da), lambda i: (layer, 0, 0)),
                  wob_spec, xres_spec] + any_specs,
        out_specs=stream_specs + [pl.BlockSpec((s, da), lambda i: (0, 0))],
        out_shape=_stream_shapes(total_rows, d) + [jax.ShapeDtypeStruct((s, da), F32)],
        scratch_shapes=[pltpu.VMEM((s, d), BF16)],
        input_output_aliases={11: 0, 12: 1, 13: 2},
        compiler_params=_params("arbitrary"),
        name="even_mixer_sample",
    )(proj, proj, proj, proj, conv_hist, pool_hist,
      w_conv, w_pool, pool_scale.reshape(pool_scale.shape[0], 1, da), wob, x, *stream)


def _odd_sample_kernel(u_ref, v_ref, a_ref, g_ref, hist_ref,
                       vng_ref, vnb_ref, ws0_ref, b0_ref,
                       wconv_ref, bconv_ref, cng_ref, cnb_ref, wob_ref, xres_ref,
                       x_in, xb_in, ssq_in,
                       x_out, xb_out, ssq_out, vn_ref, glu_ref, cat_ref):
    del x_in, xb_in, ssq_in
    s, dc = u_ref.shape
    u = jax.nn.gelu(u_ref[...].astype(F32))
    vn = _layernorm(jax.nn.gelu(v_ref[...].astype(F32)), vng_ref[...], vnb_ref[...])
    vn_ref[...] = vn
    mixed = ws0_ref[...] * vn + b0_ref[...]
    cat_ref[:, 0:dc] = (u * mixed).astype(cat_ref.dtype)

    glu = a_ref[...].astype(F32) * jax.nn.sigmoid(g_ref[...].astype(F32))
    glu_ref[...] = glu
    nh = hist_ref.shape[1]
    acc = (jnp.sum(hist_ref[...] * wconv_ref[0:nh, :][None], axis=1)
           + glu * wconv_ref[nh:nh + 1, :])
    y = _layernorm(acc + bconv_ref[...], cng_ref[...], cnb_ref[...])
    cat_ref[:, dc:2 * dc] = (y * jax.nn.sigmoid(y)).astype(cat_ref.dtype)
    o = xres_ref[...] + _dot(cat_ref[...], wob_ref[...])
    _emit_stream(o, x_out, xb_out, ssq_out)


def _odd_sample(proj, x, stream, hist, v_norm_g, v_norm_b, w_spatial, b_spatial, w_conv,
                b_conv, conf_norm_g, conf_norm_b, wob, layer, *, row0):
    s, nh, dc = hist.shape[1:]
    nheads = w_spatial.shape[1]
    hd = dc // nheads
    total_rows, d = x.shape
    rb = row0 // s
    ws0 = jnp.repeat(w_spatial[layer, :, 0, 0], hd).reshape(1, dc)
    b0 = jnp.repeat(b_spatial[layer, :, 0], hd).reshape(1, dc)
    wob_spec, xres_spec, any_specs, stream_specs = _sample_stream_specs(s, d, rb)

    def seg(c):
        return pl.BlockSpec((s, dc), lambda i: (rb, c))

    def vec():
        return pl.BlockSpec((None, 1, dc), lambda i: (layer, 0, 0))

    def row():
        return pl.BlockSpec((1, dc), lambda i: (0, 0))

    def as_rows(v):
        return v.reshape(v.shape[0], 1, dc)

    return pl.pallas_call(
        _odd_sample_kernel,
        grid=(1,),
        in_specs=[seg(0), seg(1), seg(2), seg(3),
                  pl.BlockSpec((None, s, nh, dc), lambda i: (layer, 0, 0, 0)),
                  vec(), vec(), row(), row(),
                  pl.BlockSpec((None, nh + 1, dc), lambda i: (layer, 0, 0)),
                  vec(), vec(), vec(), wob_spec, xres_spec] + any_specs,
        out_specs=stream_specs + [pl.BlockSpec((s, dc), lambda i: (0, 0)),
                                  pl.BlockSpec((s, dc), lambda i: (0, 0))],
        out_shape=_stream_shapes(total_rows, d) + [jax.ShapeDtypeStruct((s, dc), F32),
                                                   jax.ShapeDtypeStruct((s, dc), F32)],
        scratch_shapes=[pltpu.VMEM((s, d), BF16)],
        input_output_aliases={15: 0, 16: 1, 17: 2},
        compiler_params=_params("arbitrary"),
        name="odd_mixer_sample",
    )(proj, proj, proj, proj, hist,
      as_rows(v_norm_g), as_rows(v_norm_b), ws0, b0,
      w_conv, as_rows(b_conv), as_rows(conf_norm_g), as_rows(conf_norm_b), wob, x,
      *stream)


def _push(hist, new_row):
    return jnp.concatenate([hist[:, 1:], new_row[:, None, :]], axis=1)


def kernel(x_prompt, x_sample, state_conv_a, state_pool, state_conformer, norm_mix, norm_ffn, w_in_even, w_conv_a, w_pool, pool_scale, w_out_even, w_in_odd, v_norm_g, v_norm_b, w_spatial, b_spatial, w_conv_d, b_conv_d, conf_norm_g, conf_norm_b, w_out_odd, w_ffn_gate, w_ffn_up, w_ffn_down, norm_final):
    batch, seq, d = x_prompt.shape
    ns = x_sample.shape[0]
    mp = batch * seq
    m = mp + ns
    depth = norm_mix.shape[0]
    da = w_conv_a.shape[2]
    assert x_sample.shape[1] == 1 and mp % ns == 0 and mp % PROMPT_ROWS == 0
    assert m % WIDE_ROW_TILE == 0 and m % DOWN_ROW_TILE == 0
    assert seq % MIX_TILE == 0 and MIX_TILE % CHUNK == 0

    stream = _prep(x_prompt.reshape(mp, d), rows=PROMPT_ROWS, block0=0, total_rows=m)
    x, xb, ssq = _prep(x_sample.reshape(ns, d), rows=ns, block0=mp // ns, total_rows=m,
                       prev=stream)
    b_spatial_t = jnp.swapaxes(b_spatial, 1, 2)
    conv_p, conv_s, pool_p, pool_s = [], [], [], []
    chunk_p, chunk_s, conf_p, conf_s = [], [], [], []
    for l in range(depth):
        i = l // 2
        if l % 2 == 0:
            proj = _proj(xb, ssq, norm_mix[l], w_in_even, i)
            wob = _cast_bf16(w_out_even, i)
            *stream, cs_p, ps_p = _even_prompt(proj, x, w_conv_a, w_pool, pool_scale, wob,
                                               i, batch=batch, seq=seq)
            x, xb, ssq, gx_s = _even_sample(proj, x, stream, state_conv_a, state_pool,
                                            w_conv_a, w_pool, pool_scale, wob, i, row0=mp)
            conv_p.append(cs_p)
            pool_p.append(ps_p)
            conv_s.append(_push(state_conv_a[i], gx_s))
            pool_s.append(_push(state_pool[i], proj[mp:, 3 * da:].astype(F32)))
        else:
            proj = _proj(xb, ssq, norm_mix[l], w_in_odd, i)
            wob = _cast_bf16(w_out_odd, i)
            *stream, cv_p, cf_p = _odd_prompt(
                proj, x, v_norm_g, v_norm_b, w_spatial, b_spatial_t,
                w_conv_d, b_conv_d, conf_norm_g, conf_norm_b, wob, i, batch=batch, seq=seq)
            x, xb, ssq, vn_s, glu_s = _odd_sample(
                proj, x, stream, state_conformer, v_norm_g, v_norm_b, w_spatial,
                b_spatial, w_conv_d, b_conv_d, conf_norm_g, conf_norm_b, wob, i, row0=mp)
            chunk_p.append(cv_p)
            conf_p.append(cf_p)
            chunk_s.append(vn_s[:, None, :])
            conf_s.append(_push(state_conformer[i], glu_s))
        h = _gate_up(xb, ssq, norm_ffn[l], w_ffn_gate, w_ffn_up, l)
        x, xb, ssq = _matmul_res(h, w_ffn_down, l, x, tm=DOWN_ROW_TILE, name="ffn_down")
    y_p = _final_norm(x, norm_final, rows=PROMPT_ROWS, block0=0, nblocks=mp // PROMPT_ROWS)
    y_s = _final_norm(x, norm_final, rows=ns, block0=mp // ns, nblocks=1)
    return (y_p.reshape(batch, seq, d), y_s.reshape(ns, 1, d),
            jnp.stack(conv_p), jnp.stack(conv_s), jnp.stack(pool_p), jnp.stack(pool_s),
            jnp.stack(chunk_p), jnp.stack(chunk_s), jnp.stack(conf_p), jnp.stack(conf_s))
```

```python
import functools

import jax
import jax.numpy as jnp
from jax import lax
from jax.experimental import pallas as pl
from jax.experimental.pallas import tpu as pltpu

F32 = jnp.float32
BF16 = jnp.bfloat16

EPS = 1e-6
PAST_LEN = 16384
POOL_WINDOWS = (2, 4, 8, 16)
CHUNK = 128

LANES = 128
SUBLANES = 8
VMEM_LIMIT_BYTES = 56 * 1024 * 1024

WIDE_ROW_TILE = 4160
DOWN_ROW_TILE = 1040
PROMPT_ROWS = 1024
MIX_TILE = 256
POOL_HALO = 16
CONF_HALO = 32


def _params(*sem):
    return pltpu.CompilerParams(dimension_semantics=sem,
                                vmem_limit_bytes=VMEM_LIMIT_BYTES)


def _rmsnorm(x, g):
    ms = jnp.mean(x * x, axis=-1, keepdims=True)
    return x * lax.rsqrt(ms + EPS) * g


def _layernorm(x, g, b):
    mu = jnp.mean(x, axis=-1, keepdims=True)
    xc = x - mu
    var = jnp.mean(xc * xc, axis=-1, keepdims=True)
    return xc * lax.rsqrt(var + EPS) * g + b


def _dot(a, b):
    return jnp.dot(a, b, preferred_element_type=F32)


def _lane_chunks(n):
    return [slice(c * LANES, (c + 1) * LANES) for c in range(n // LANES)]


def _gain_columns(g):
    return jnp.broadcast_to(g[:, None], (g.shape[0], LANES))


def _emit_stream(o, x_out, xb_out, ssq_out):
    x_out[...] = o
    xb_out[...] = o.astype(BF16)
    ssq_out[...] = jnp.broadcast_to(jnp.sum(o * o, axis=-1, keepdims=True),
                                    ssq_out.shape)


def _stream_shapes(total_rows, d):
    return [jax.ShapeDtypeStruct((total_rows, d), F32),
            jax.ShapeDtypeStruct((total_rows, d), BF16),
            jax.ShapeDtypeStruct((total_rows, LANES), F32)]


def _prep_kernel(x_ref, *refs):
    xb_out, ssq_out = refs[-2:]
    x = x_ref[...]
    xb_out[...] = x.astype(BF16)
    ssq_out[...] = jnp.broadcast_to(jnp.sum(x * x, axis=-1, keepdims=True),
                                    ssq_out.shape)


def _prep(x2d, *, rows, block0, total_rows, prev=None):
    n, d = x2d.shape
    in_specs = [pl.BlockSpec((rows, d), lambda i: (i, 0))]
    args = [x2d]
    aliases = {}
    if prev is not None:
        in_specs += [pl.BlockSpec(memory_space=pl.ANY)] * 2
        args += list(prev)
        aliases = {1: 0, 2: 1}
    return pl.pallas_call(
        _prep_kernel,
        grid=(n // rows,),
        in_specs=in_specs,
        out_specs=[pl.BlockSpec((rows, d), lambda i: (block0 + i, 0)),
                   pl.BlockSpec((rows, LANES), lambda i: (block0 + i, 0))],
        out_shape=_stream_shapes(total_rows, d)[1:],
        input_output_aliases=aliases,
        compiler_params=_params("arbitrary"),
        name="stream_entry",
    )(*args)


def _resident_rows(tm, k):
    return pl.BlockSpec((tm, k), lambda i, j: (i, 0), pipeline_mode=pl.Buffered(1))


def _scaled_weight(w_ref, g_ref, wb_ref):
    for sl in _lane_chunks(w_ref.shape[1]):
        wb_ref[:, sl] = (w_ref[:, sl] * g_ref[...]).astype(BF16)


def _row_scale(ssq_ref, d):
    return lax.rsqrt(ssq_ref[...] / d + EPS)


def _proj_kernel(xb_ref, ssq_ref, g_ref, w_ref, o_ref, wb_ref):
    _scaled_weight(w_ref, g_ref, wb_ref)
    r = _row_scale(ssq_ref, xb_ref.shape[1])
    acc = _dot(xb_ref[...], wb_ref[...])
    for sl in _lane_chunks(o_ref.shape[1]):
        o_ref[:, sl] = (acc[:, sl] * r).astype(o_ref.dtype)


def _proj(xb, ssq, g, w_stack, layer, *, tn=512):
    m, d = xb.shape
    n = w_stack.shape[2]
    return pl.pallas_call(
        _proj_kernel,
        grid=(m // WIDE_ROW_TILE, n // tn),
        in_specs=[_resident_rows(WIDE_ROW_TILE, d),
                  pl.BlockSpec((WIDE_ROW_TILE, LANES), lambda i, j: (i, 0)),
                  pl.BlockSpec((d, LANES), lambda i, j: (0, 0)),
                  pl.BlockSpec((None, d, tn), lambda i, j: (layer, 0, j))],
        out_specs=pl.BlockSpec((WIDE_ROW_TILE, tn), lambda i, j: (i, j)),
        out_shape=jax.ShapeDtypeStruct((m, n), BF16),
        scratch_shapes=[pltpu.VMEM((d, tn), BF16)],
        compiler_params=_params("parallel", "arbitrary"),
        name="mixer_in_proj",
    )(xb, ssq, _gain_columns(g), w_stack)


def _gate_up_kernel(xb_ref, ssq_ref, g_ref, wg_ref, wu_ref, h_ref, wgb_ref, wub_ref):
    _scaled_weight(wg_ref, g_ref, wgb_ref)
    _scaled_weight(wu_ref, g_ref, wub_ref)
    r = _row_scale(ssq_ref, xb_ref.shape[1])
    xb = xb_ref[...]
    gate = _dot(xb, wgb_ref[...])
    up = _dot(xb, wub_ref[...])
    for sl in _lane_chunks(h_ref.shape[1]):
        gt = gate[:, sl] * r
        h_ref[:, sl] = (gt * jax.nn.sigmoid(gt) * (up[:, sl] * r)).astype(BF16)


def _gate_up(xb, ssq, g, wg_stack, wu_stack, layer, *, tn=256):
    m, d = xb.shape
    f = wg_stack.shape[2]

    def wspec():
        return pl.BlockSpec((None, d, tn), lambda i, j: (layer, 0, j))

    return pl.pallas_call(
        _gate_up_kernel,
        grid=(m // WIDE_ROW_TILE, f // tn),
        in_specs=[_resident_rows(WIDE_ROW_TILE, d),
                  pl.BlockSpec((WIDE_ROW_TILE, LANES), lambda i, j: (i, 0)),
                  pl.BlockSpec((d, LANES), lambda i, j: (0, 0)),
                  wspec(), wspec()],
        out_specs=pl.BlockSpec((WIDE_ROW_TILE, tn), lambda i, j: (i, j)),
        out_shape=jax.ShapeDtypeStruct((m, f), BF16),
        scratch_shapes=[pltpu.VMEM((d, tn), BF16), pltpu.VMEM((d, tn), BF16)],
        compiler_params=_params("parallel", "arbitrary"),
        name="ffn_gate_up",
    )(xb, ssq, _gain_columns(g), wg_stack, wu_stack)


def _matmul_res_kernel(a_ref, w_ref, r_ref, o_ref, ob_ref, ssq_ref):
    o = r_ref[...] + _dot(a_ref[...], w_ref[...].astype(BF16))
    o_ref[...] = o
    ob_ref[...] = o.astype(BF16)
    part = jnp.broadcast_to(jnp.sum(o * o, axis=-1, keepdims=True), ssq_ref.shape)

    @pl.when(pl.program_id(1) == 0)
    def _():
        ssq_ref[...] = part

    @pl.when(pl.program_id(1) > 0)
    def _():
        ssq_ref[...] += part


def _matmul_res(a, w_stack, layer, r, *, tm, tn=256, name):
    m, k = a.shape
    n = w_stack.shape[2]
    return pl.pallas_call(
        _matmul_res_kernel,
        grid=(m // tm, n // tn),
        in_specs=[pl.BlockSpec((tm, k), lambda i, j: (i, 0)),
                  pl.BlockSpec((None, k, tn), lambda i, j: (layer, 0, j)),
                  pl.BlockSpec((tm, tn), lambda i, j: (i, j))],
        out_specs=[pl.BlockSpec((tm, tn), lambda i, j: (i, j)),
                   pl.BlockSpec((tm, tn), lambda i, j: (i, j)),
                   pl.BlockSpec((tm, LANES), lambda i, j: (i, 0))],
        out_shape=_stream_shapes(m, n),
        compiler_params=_params("parallel", "arbitrary"),
        name=name,
    )(a, w_stack, r)


def _cast_kernel(w_ref, o_ref):
    o_ref[...] = w_ref[...].astype(o_ref.dtype)


def _cast_bf16(w_stack, layer, *, rows=512):
    k, n = w_stack.shape[1:]
    return pl.pallas_call(
        _cast_kernel,
        grid=(k // rows,),
        in_specs=[pl.BlockSpec((None, rows, n), lambda i: (layer, i, 0))],
        out_specs=pl.BlockSpec((rows, n), lambda i: (i, 0)),
        out_shape=jax.ShapeDtypeStruct((k, n), BF16),
        compiler_params=_params("parallel"),
        name="out_proj_weight_bf16",
    )(w_stack)


def _final_norm_kernel(x_ref, g_ref, o_ref):
    o_ref[...] = _rmsnorm(x_ref[...], g_ref[...])


def _final_norm(x, g, *, rows, block0, nblocks):
    d = x.shape[1]
    return pl.pallas_call(
        _final_norm_kernel,
        grid=(nblocks,),
        in_specs=[pl.BlockSpec((rows, d), lambda i: (block0 + i, 0)),
                  pl.BlockSpec((1, d), lambda i: (0, 0))],
        out_specs=pl.BlockSpec((rows, d), lambda i: (i, 0)),
        out_shape=jax.ShapeDtypeStruct((rows * nblocks, d), F32),
        compiler_params=_params("parallel"),
        name="final_norm",
    )(x, g.reshape(1, d))


PROJ_CHUNK = 256


def _pipelined_tail(mixer_stages, cat_a, cat_b, wob_ref, xres_ref, stream_outs, *,
                    interleave):
    s = pl.program_id(0)
    x_out, xb_out, ssq_out = stream_outs
    nchunks = wob_ref.shape[1] // PROJ_CHUNK

    @pl.when(s == 0)
    def _():
        cat_b[...] = jnp.zeros_like(cat_b)

    def step(cat_w, cat_r):
        stages = mixer_stages(cat_w)
        if not interleave:
            for stage in stages:
                stage()
            o = xres_ref[...] + _dot(cat_r[...], wob_ref[...])
            _emit_stream(o, *stream_outs)
            return
        ssq = jnp.zeros((x_out.shape[0], 1), F32)
        done = 0
        for k, stage in enumerate(stages):
            upto = (k + 1) * nchunks // len(stages)
            for c in range(done, upto):
                sl = slice(c * PROJ_CHUNK, (c + 1) * PROJ_CHUNK)
                o = xres_ref[:, sl] + _dot(cat_r[...], wob_ref[:, sl])
                x_out[:, sl] = o
                xb_out[:, sl] = o.astype(BF16)
                ssq = ssq + jnp.sum(o * o, axis=-1, keepdims=True)
            done = upto
            stage()
        ssq_out[...] = jnp.broadcast_to(ssq, ssq_out.shape)

    @pl.when(s % 2 == 0)
    def _():
        step(cat_a, cat_b)

    @pl.when(s % 2 == 1)
    def _():
        step(cat_b, cat_a)


def _tail_specs(ntiles, tt, d):
    def mixed(s):
        return jnp.minimum(s, ntiles - 1)

    def projected(s):
        return jnp.maximum(s - 1, 0)

    wob = pl.BlockSpec((d, d), lambda s: (0, 0), pipeline_mode=pl.Buffered(1))
    xres = pl.BlockSpec((tt, d), lambda s: (projected(s), 0))
    outs = [pl.BlockSpec((tt, d), lambda s: (projected(s), 0)),
            pl.BlockSpec((tt, d), lambda s: (projected(s), 0)),
            pl.BlockSpec((tt, LANES), lambda s: (projected(s), 0))]
    return mixed, wob, xres, outs


def _pool_project(pooled, w_pool_ref, scale_ref, cat_ref, col0):
    gw = w_pool_ref.shape[1]
    for g in range(len(POOL_WINDOWS)):
        sl = slice(g * gw, (g + 1) * gw)
        yb = _dot(pooled[:, sl].astype(BF16), w_pool_ref[g].astype(BF16))
        cat_ref[:, col0 + g * gw:col0 + (g + 1) * gw] = (
            yb * scale_ref[:, sl]).astype(cat_ref.dtype)


def _even_prompt_kernel(nt, ntiles,
                        xin_ref, gpre_ref, gpost_ref, p_ref,
                        hxin_ref, hgpre_ref, hp_ref,
                        wconv_ref, wpool_ref, scale_ref, wob_ref, xres_ref,
                        x_out, xb_out, ssq_out, conv_state_ref, pool_state_ref,
                        gx_ext, p_ext, cat_a, cat_b):
    tt, da = xin_ref.shape
    h = POOL_HALO
    i = jnp.minimum(pl.program_id(0), ntiles - 1) % nt
    live = (i > 0).astype(F32)

    def mixer_stages(cat_ref):
        def short_conv():
            gx = gpre_ref[...].astype(F32) * xin_ref[...].astype(F32)
            gx_ext[0:h, :] = (hgpre_ref[...].astype(F32) * hxin_ref[...].astype(F32)
                              * live)
            gx_ext[h:h + tt, :] = gx
            conv = (gx_ext[pl.ds(h - 2, tt), :] * wconv_ref[0:1, :]
                    + gx_ext[pl.ds(h - 1, tt), :] * wconv_ref[1:2, :]
                    + gx * wconv_ref[2:3, :])
            cat_ref[:, 0:da] = (gpost_ref[...].astype(F32) * conv).astype(cat_ref.dtype)
            conv_state_ref[...] = gx_ext[pl.ds(h + tt - 2, 2), :]
            p_ext[0:h, :] = hp_ref[...].astype(F32) * live
            p_ext[h:h + tt, :] = p_ref[...].astype(F32)
            nh = pool_state_ref.shape[0]
            pool_state_ref[...] = p_ext[pl.ds(h + tt - nh, nh), :]

        def pool_group(g, w):
            def run():
                gw = da // len(POOL_WINDOWS)
                sl = slice(g * gw, (g + 1) * gw)
                pg = p_ext[pl.ds(h, tt), sl]
                acc = pg
                for k in range(1, w):
                    acc = acc + p_ext[pl.ds(h - k, tt), sl]
                pos = i * tt + lax.broadcasted_iota(jnp.int32, (tt, 1), 0)
                cnt = jnp.minimum(w, pos + 1).astype(F32)
                pooled = acc / cnt - pg
                yb = _dot(pooled.astype(BF16), wpool_ref[g].astype(BF16))
                cat_ref[:, da + g * gw:da + (g + 1) * gw] = (
                    yb * scale_ref[:, sl]).astype(cat_ref.dtype)
            return run

        return [short_conv] + [pool_group(g, w) for g, w in enumerate(POOL_WINDOWS)]

    _pipelined_tail(mixer_stages, cat_a, cat_b, wob_ref, xres_ref,
                    (x_out, xb_out, ssq_out), interleave=True)


def _even_prompt(proj, x, w_conv, w_pool, pool_scale, wob, layer, *, batch, seq,
                 total_rows):
    da = w_conv.shape[2]
    d = x.shape[1]
    tt = MIX_TILE
    nt = seq // tt
    ntiles = batch * nt
    hb = tt // POOL_HALO
    nwin = len(POOL_WINDOWS)
    nh = POOL_WINDOWS[-1] - 1
    mixed, wob_spec, xres_spec, stream_specs = _tail_specs(ntiles, tt, d)

    def main(c):
        return pl.BlockSpec((tt, da), lambda s: (mixed(s), c))

    def halo(c):
        return pl.BlockSpec((POOL_HALO, da),
                            lambda s: (jnp.maximum(mixed(s) * hb - 1, 0), c))

    return pl.pallas_call(
        functools.partial(_even_prompt_kernel, nt, ntiles),
        grid=(ntiles + 1,),
        in_specs=[main(0), main(1), main(2), main(3), halo(0), halo(1), halo(3),
                  pl.BlockSpec((None, 3, da), lambda s: (layer, 0, 0)),
                  pl.BlockSpec((None, nwin, da // nwin, da // nwin),
                               lambda s: (layer, 0, 0, 0)),
                  pl.BlockSpec((None, 1, da), lambda s: (layer, 0, 0)),
                  wob_spec, xres_spec],
        out_specs=stream_specs + [
            pl.BlockSpec((None, 2, da), lambda s: (mixed(s) // nt, 0, 0)),
            pl.BlockSpec((None, nh, da), lambda s: (mixed(s) // nt, 0, 0))],
        out_shape=_stream_shapes(total_rows, d) + [
            jax.ShapeDtypeStruct((batch, 2, da), F32),
            jax.ShapeDtypeStruct((batch, nh, da), F32)],
        scratch_shapes=[pltpu.VMEM((POOL_HALO + tt, da), F32),
                        pltpu.VMEM((POOL_HALO + tt, da), F32),
                        pltpu.VMEM((tt, d), BF16),
                        pltpu.VMEM((tt, d), BF16)],
        compiler_params=_params("arbitrary"),
        name="even_mixer_prompt",
    )(proj, proj, proj, proj, proj, proj, proj,
      w_conv, w_pool, pool_scale.reshape(pool_scale.shape[0], 1, da), wob, x)


def _odd_prompt_kernel(nt, ntiles,
                       u_ref, v_ref, a_ref, g_ref, ha_ref, hg_ref,
                       vng_ref, vnb_ref, ws_ref, bst_ref,
                       wconv_ref, bconv_ref, cng_ref, cnb_ref, wob_ref, xres_ref,
                       x_out, xb_out, ssq_out, chunk_v_ref, conf_state_ref,
                       glu_ext, shifted, cat_a, cat_b):
    tt, dc = u_ref.shape
    h = CONF_HALO
    i = jnp.minimum(pl.program_id(0), ntiles - 1) % nt
    live = (i > 0).astype(F32)

    nheads = ws_ref.shape[0]
    hd = dc // nheads
    nw = wconv_ref.shape[0]

    def mixer_stages(cat_ref):
        vals = {}

        def gate_inputs():
            vals["u"] = jax.nn.gelu(u_ref[...].astype(F32))
            vn = _layernorm(jax.nn.gelu(v_ref[...].astype(F32)), vng_ref[...],
                            vnb_ref[...])
            chunk_v_ref[...] = vn[tt - CHUNK:tt, :]
            vals["vb"] = vn.astype(BF16)

        def gate_heads(heads):
            def run():
                row = lax.broadcasted_iota(jnp.int32, (CHUNK, CHUNK), 0)
                colm = lax.broadcasted_iota(jnp.int32, (CHUNK, CHUNK), 1)
                for hh in heads:
                    ws = jnp.where(colm <= row, ws_ref[hh], 0.0).astype(BF16)
                    bias = bst_ref[:, hh:hh + 1]
                    for c in range(tt // CHUNK):
                        rs = slice(c * CHUNK, (c + 1) * CHUNK)
                        cs = slice(hh * hd, (hh + 1) * hd)
                        mixed = _dot(ws, vals["vb"][rs, cs]) + bias
                        cat_ref[rs, cs] = (vals["u"][rs, cs] * mixed).astype(cat_ref.dtype)
            return run

        def glu_rows():
            glu = a_ref[...].astype(F32) * jax.nn.sigmoid(g_ref[...].astype(F32))
            glu_ext[0:h, :] = (ha_ref[...].astype(F32)
                               * jax.nn.sigmoid(hg_ref[...].astype(F32)) * live)
            glu_ext[h:h + tt, :] = glu
            conf_state_ref[...] = glu_ext[pl.ds(h + tt - (nw - 1), nw - 1), :]
            span = shifted.shape[1]
            for sh in range(1, SUBLANES):
                shifted[sh - 1] = glu_ext[pl.ds(sh, span), :]
            vals["acc"] = glu * wconv_ref[nw - 1:nw, :]

        def taps(ks):
            def run():
                acc = vals["acc"]
                for k in ks:
                    q, sh = divmod(h - (nw - 1) + k, SUBLANES)
                    if sh == 0:
                        tap = glu_ext[pl.ds(q * SUBLANES, tt), :]
                    else:
                        tap = shifted[sh - 1, pl.ds(q * SUBLANES, tt), :]
                    acc = acc + tap * wconv_ref[k:k + 1, :]
                vals["acc"] = acc
            return run

        def conv_norm():
            y = _layernorm(vals["acc"] + bconv_ref[...], cng_ref[...], cnb_ref[...])
            cat_ref[:, dc:2 * dc] = (y * jax.nn.sigmoid(y)).astype(cat_ref.dtype)

        third = (nw - 1) // 3
        return [gate_inputs,
                gate_heads(range(0, nheads // 2)), gate_heads(range(nheads // 2, nheads)),
                glu_rows,
                taps(range(0, third)), taps(range(third, 2 * third)),
                taps(range(2 * third, nw - 1)),
                conv_norm]

    _pipelined_tail(mixer_stages, cat_a, cat_b, wob_ref, xres_ref,
                    (x_out, xb_out, ssq_out), interleave=False)


def _odd_prompt(proj, x, v_norm_g, v_norm_b, w_spatial, b_spatial_t, w_conv, b_conv,
                conf_norm_g, conf_norm_b, wob, layer, *, batch, seq, total_rows):
    nw, dc = w_conv.shape[1:]
    nheads = w_spatial.shape[1]
    d = x.shape[1]
    tt = MIX_TILE
    nt = seq // tt
    ntiles = batch * nt
    hb = tt // CONF_HALO
    mixed, wob_spec, xres_spec, stream_specs = _tail_specs(ntiles, tt, d)

    def main(c):
        return pl.BlockSpec((tt, dc), lambda s: (mixed(s), c))

    def halo(c):
        return pl.BlockSpec((CONF_HALO, dc),
                            lambda s: (jnp.maximum(mixed(s) * hb - 1, 0), c))

    def vec():
        return pl.BlockSpec((None, 1, dc), lambda s: (layer, 0, 0))

    def as_rows(v):
        return v.reshape(v.shape[0], 1, dc)

    return pl.pallas_call(
        functools.partial(_odd_prompt_kernel, nt, ntiles),
        grid=(ntiles + 1,),
        in_specs=[main(0), main(1), main(2), main(3), halo(2), halo(3),
                  vec(), vec(),
                  pl.BlockSpec((None, nheads, CHUNK, CHUNK), lambda s: (layer, 0, 0, 0)),
                  pl.BlockSpec((None, CHUNK, nheads), lambda s: (layer, 0, 0)),
                  pl.BlockSpec((None, nw, dc), lambda s: (layer, 0, 0)),
                  vec(), vec(), vec(), wob_spec, xres_spec],
        out_specs=stream_specs + [
            pl.BlockSpec((None, CHUNK, dc), lambda s: (mixed(s) // nt, 0, 0)),
            pl.BlockSpec((None, nw - 1, dc), lambda s: (mixed(s) // nt, 0, 0))],
        out_shape=_stream_shapes(total_rows, d) + [
            jax.ShapeDtypeStruct((batch, CHUNK, dc), F32),
            jax.ShapeDtypeStruct((batch, nw - 1, dc), F32)],
        scratch_shapes=[pltpu.VMEM((CONF_HALO + tt, dc), F32),
                        pltpu.VMEM((SUBLANES - 1, CONF_HALO - SUBLANES + tt, dc), F32),
                        pltpu.VMEM((tt, d), BF16),
                        pltpu.VMEM((tt, d), BF16)],
        compiler_params=_params("arbitrary"),
        name="odd_mixer_prompt",
    )(proj, proj, proj, proj, proj, proj,
      as_rows(v_norm_g), as_rows(v_norm_b), w_spatial, b_spatial_t,
      w_conv, as_rows(b_conv), as_rows(conf_norm_g), as_rows(conf_norm_b), wob, x)


def _push_copies(hist_any, new_any, layer, new_row, sems, slot):
    k = hist_any.shape[1]
    shift = pltpu.make_async_copy(hist_any.at[layer, pl.ds(1, k - 1)],
                                  new_any.at[layer, pl.ds(0, k - 1)], sems.at[slot])
    last = pltpu.make_async_copy(new_row, new_any.at[layer, k - 1], sems.at[slot + 1])
    return shift, last


def _even_sample_kernel(layer, nprev,
                        xin_ref, gpre_ref, gpost_ref, p_ref,
                        conv_hist_ref, pool_hist_ref,
                        wconv_ref, wpool_ref, scale_ref, wob_ref, xres_ref,
                        x_in, xb_in, ssq_in, conv_any, pool_any, *refs):
    del x_in, xb_in, ssq_in
    x_out, xb_out, ssq_out, new_conv, new_pool, cat_ref, gx_row, p_row, sems = refs[nprev:]
    s, da = xin_ref.shape
    conv_shift, conv_last = _push_copies(conv_any, new_conv, layer, gx_row, sems, 0)
    pool_shift, pool_last = _push_copies(pool_any, new_pool, layer, p_row, sems, 2)
    conv_shift.start()
    pool_shift.start()

    gx = gpre_ref[...].astype(F32) * xin_ref[...].astype(F32)
    gx_row[...] = gx
    conv_last.start()
    nc = conv_hist_ref.shape[0]
    conv = gx * wconv_ref[nc:nc + 1, :]
    for k in range(nc):
        conv = conv + conv_hist_ref[k] * wconv_ref[k:k + 1, :]
    cat_ref[:, 0:da] = (gpost_ref[...].astype(F32) * conv).astype(cat_ref.dtype)

    p = p_ref[...].astype(F32)
    p_row[...] = p
    pool_last.start()
    nh = pool_hist_ref.shape[0]
    gw = da // len(POOL_WINDOWS)
    pooled = []
    for g, w in enumerate(POOL_WINDOWS):
        sl = slice(g * gw, (g + 1) * gw)
        acc = p[:, sl]
        for back in range(1, w):
            acc = acc + pool_hist_ref[nh - back, :, sl]
        cnt = float(min(w, PAST_LEN + 1))
        pooled.append(acc / cnt - p[:, sl])
    pooled = jnp.concatenate(pooled, axis=-1)
    _pool_project(pooled, wpool_ref, scale_ref, cat_ref, da)
    o = xres_ref[...] + _dot(cat_ref[...], wob_ref[...])
    _emit_stream(o, x_out, xb_out, ssq_out)
    conv_shift.wait()
    pool_shift.wait()
    conv_last.wait()
    pool_last.wait()


def _sample_stream_specs(s, d, xres_block, out_block):
    wob = pl.BlockSpec((d, d), lambda i: (0, 0), pipeline_mode=pl.Buffered(1))
    xres = pl.BlockSpec((s, d), lambda i: (xres_block, 0))
    anys = [pl.BlockSpec(memory_space=pl.ANY)] * 3
    outs = [pl.BlockSpec((s, d), lambda i: (out_block, 0)),
            pl.BlockSpec((s, d), lambda i: (out_block, 0)),
            pl.BlockSpec((s, LANES), lambda i: (out_block, 0))]
    return wob, xres, anys, outs


def _history_block(hist_t, layer):
    return pl.BlockSpec((None,) + hist_t.shape[1:], lambda i: (layer, 0, 0, 0),
                        pipeline_mode=pl.Buffered(1))


def _even_sample(proj, xres, xres_block, stream, conv_hist_t, pool_hist_t, new_states,
                 w_conv, w_pool, pool_scale, wob, layer, *, row0):
    nc, s, da = conv_hist_t.shape[1:]
    total_rows, d = stream[0].shape
    rb = row0 // s
    nwin = len(POOL_WINDOWS)
    wob_spec, xres_spec, any_specs, stream_specs = _sample_stream_specs(
        s, d, xres_block, rb)
    nprev = len(new_states)
    any_spec = pl.BlockSpec(memory_space=pl.ANY)

    def seg(c):
        return pl.BlockSpec((s, da), lambda i: (rb, c))

    in_specs = [seg(0), seg(1), seg(2), seg(3),
                _history_block(conv_hist_t, layer), _history_block(pool_hist_t, layer),
                pl.BlockSpec((None, nc + 1, da), lambda i: (layer, 0, 0)),
                pl.BlockSpec((None, nwin, da // nwin, da // nwin),
                             lambda i: (layer, 0, 0, 0)),
                pl.BlockSpec((None, 1, da), lambda i: (layer, 0, 0)),
                wob_spec, xres_spec] + any_specs + [any_spec] * (2 + nprev)
    first_prev = len(in_specs) - nprev
    aliases = {11: 0, 12: 1, 13: 2}
    aliases.update({first_prev + k: 3 + k for k in range(nprev)})
    return pl.pallas_call(
        functools.partial(_even_sample_kernel, layer, nprev),
        grid=(1,),
        in_specs=in_specs,
        out_specs=stream_specs + [any_spec, any_spec],
        out_shape=_stream_shapes(total_rows, d) + [
            jax.ShapeDtypeStruct(conv_hist_t.shape, F32),
            jax.ShapeDtypeStruct(pool_hist_t.shape, F32)],
        scratch_shapes=[pltpu.VMEM((s, d), BF16), pltpu.VMEM((s, da), F32),
                        pltpu.VMEM((s, da), F32), pltpu.SemaphoreType.DMA((4,))],
        input_output_aliases=aliases,
        compiler_params=_params("arbitrary"),
        name="even_mixer_sample",
    )(proj, proj, proj, proj, conv_hist_t, pool_hist_t,
      w_conv, w_pool, pool_scale.reshape(pool_scale.shape[0], 1, da), wob, xres, *stream,
      conv_hist_t, pool_hist_t, *new_states)


def _odd_sample_kernel(layer, nprev,
                       u_ref, v_ref, a_ref, g_ref, hist_ref,
                       vng_ref, vnb_ref, ws0_ref, b0_ref,
                       wconv_ref, bconv_ref, cng_ref, cnb_ref, wob_ref, xres_ref,
                       x_in, xb_in, ssq_in, hist_any, *refs):
    del x_in, xb_in, ssq_in
    x_out, xb_out, ssq_out, vn_ref, new_hist, cat_ref, glu_row, sems = refs[nprev:]
    s, dc = u_ref.shape
    shift, last = _push_copies(hist_any, new_hist, layer, glu_row, sems, 0)
    shift.start()
    glu = a_ref[...].astype(F32) * jax.nn.sigmoid(g_ref[...].astype(F32))
    glu_row[...] = glu
    last.start()

    u = jax.nn.gelu(u_ref[...].astype(F32))
    vn = _layernorm(jax.nn.gelu(v_ref[...].astype(F32)), vng_ref[...], vnb_ref[...])
    vn_ref[...] = vn
    mixed = ws0_ref[...] * vn + b0_ref[...]
    cat_ref[:, 0:dc] = (u * mixed).astype(cat_ref.dtype)

    nh = hist_ref.shape[0]
    acc = glu * wconv_ref[nh:nh + 1, :]
    for k in range(nh):
        acc = acc + hist_ref[k] * wconv_ref[k:k + 1, :]
    y = _layernorm(acc + bconv_ref[...], cng_ref[...], cnb_ref[...])
    cat_ref[:, dc:2 * dc] = (y * jax.nn.sigmoid(y)).astype(cat_ref.dtype)
    o = xres_ref[...] + _dot(cat_ref[...], wob_ref[...])
    _emit_stream(o, x_out, xb_out, ssq_out)
    shift.wait()
    last.wait()


def _odd_sample(proj, xres, xres_block, stream, hist_t, new_states, v_norm_g, v_norm_b,
                w_spatial, b_spatial, w_conv, b_conv, conf_norm_g, conf_norm_b, wob, layer,
                *, row0):
    nh, s, dc = hist_t.shape[1:]
    nheads = w_spatial.shape[1]
    hd = dc // nheads
    total_rows, d = stream[0].shape
    rb = row0 // s
    ws0 = jnp.repeat(w_spatial[layer, :, 0, 0], hd).reshape(1, dc)
    b0 = jnp.repeat(b_spatial[layer, :, 0], hd).reshape(1, dc)
    wob_spec, xres_spec, any_specs, stream_specs = _sample_stream_specs(
        s, d, xres_block, rb)
    nprev = len(new_states)
    any_spec = pl.BlockSpec(memory_space=pl.ANY)

    def seg(c):
        return pl.BlockSpec((s, dc), lambda i: (rb, c))

    def vec():
        return pl.BlockSpec((None, 1, dc), lambda i: (layer, 0, 0))

    def row():
        return pl.BlockSpec((1, dc), lambda i: (0, 0))

    def as_rows(v):
        return v.reshape(v.shape[0], 1, dc)

    in_specs = [seg(0), seg(1), seg(2), seg(3), _history_block(hist_t, layer),
                vec(), vec(), row(), row(),
                pl.BlockSpec((None, nh + 1, dc), lambda i: (layer, 0, 0)),
                vec(), vec(), vec(), wob_spec, xres_spec] + any_specs + (
                    [any_spec] * (1 + nprev))
    first_prev = len(in_specs) - nprev
    aliases = {15: 0, 16: 1, 17: 2}
    aliases.update({first_prev + k: 4 + k for k in range(nprev)})
    return pl.pallas_call(
        functools.partial(_odd_sample_kernel, layer, nprev),
        grid=(1,),
        in_specs=in_specs,
        out_specs=stream_specs + [pl.BlockSpec((s, dc), lambda i: (0, 0)), any_spec],
        out_shape=_stream_shapes(total_rows, d) + [
            jax.ShapeDtypeStruct((s, dc), F32),
            jax.ShapeDtypeStruct(hist_t.shape, F32)],
        scratch_shapes=[pltpu.VMEM((s, d), BF16), pltpu.VMEM((s, dc), F32),
                        pltpu.SemaphoreType.DMA((2,))],
        input_output_aliases=aliases,
        compiler_params=_params("arbitrary"),
        name="odd_mixer_sample",
    )(proj, proj, proj, proj, hist_t,
      as_rows(v_norm_g), as_rows(v_norm_b), ws0, b0,
      w_conv, as_rows(b_conv), as_rows(conf_norm_g), as_rows(conf_norm_b), wob, xres,
      *stream, hist_t, *new_states)


def kernel(x_prompt, x_sample, state_conv_a, state_pool, state_conformer, norm_mix, norm_ffn, w_in_even, w_conv_a, w_pool, pool_scale, w_out_even, w_in_odd, v_norm_g, v_norm_b, w_spatial, b_spatial, w_conv_d, b_conv_d, conf_norm_g, conf_norm_b, w_out_odd, w_ffn_gate, w_ffn_up, w_ffn_down, norm_final):
    batch, seq, d = x_prompt.shape
    ns = x_sample.shape[0]
    mp = batch * seq
    m = mp + ns
    depth = norm_mix.shape[0]
    da = w_conv_a.shape[2]
    assert x_sample.shape[1] == 1 and mp % ns == 0 and mp % PROMPT_ROWS == 0
    assert m % WIDE_ROW_TILE == 0 and m % DOWN_ROW_TILE == 0
    assert seq % MIX_TILE == 0 and MIX_TILE % CHUNK == 0

    x_prompt2d = x_prompt.reshape(mp, d)
    x_sample2d = x_sample.reshape(ns, d)
    entry = _prep(x_prompt2d, rows=PROMPT_ROWS, block0=0, total_rows=m)
    xb, ssq = _prep(x_sample2d, rows=ns, block0=mp // ns, total_rows=m, prev=entry)
    xres_p, xres_s, xres_s_block = x_prompt2d, x_sample2d, 0
    b_spatial_t = jnp.swapaxes(b_spatial, 1, 2)
    conv_hist_t = jnp.swapaxes(state_conv_a, 1, 2)
    pool_hist_t = jnp.swapaxes(state_pool, 1, 2)
    conf_hist_t = jnp.swapaxes(state_conformer, 1, 2)
    even_new, odd_new = [], []
    conv_p, pool_p, chunk_p, chunk_s, conf_p = [], [], [], [], []
    for l in range(depth):
        i = l // 2
        if l % 2 == 0:
            proj = _proj(xb, ssq, norm_mix[l], w_in_even, i)
            wob = _cast_bf16(w_out_even, i)
            *stream, cs_p, ps_p = _even_prompt(proj, xres_p, w_conv_a, w_pool, pool_scale,
                                               wob, i, batch=batch, seq=seq, total_rows=m)
            x, xb, ssq, *even_new = _even_sample(
                proj, xres_s, xres_s_block, stream, conv_hist_t, pool_hist_t, even_new,
                w_conv_a, w_pool, pool_scale, wob, i, row0=mp)
            conv_p.append(cs_p)
            pool_p.append(ps_p)
        else:
            proj = _proj(xb, ssq, norm_mix[l], w_in_odd, i)
            wob = _cast_bf16(w_out_odd, i)
            *stream, cv_p, cf_p = _odd_prompt(
                proj, xres_p, v_norm_g, v_norm_b, w_spatial, b_spatial_t, w_conv_d, b_conv_d,
                conf_norm_g, conf_norm_b, wob, i, batch=batch, seq=seq, total_rows=m)
            x, xb, ssq, vn_s, *odd_new = _odd_sample(
                proj, xres_s, xres_s_block, stream, conf_hist_t, odd_new, v_norm_g, v_norm_b,
                w_spatial, b_spatial, w_conv_d, b_conv_d, conf_norm_g, conf_norm_b, wob, i,
                row0=mp)
            chunk_p.append(cv_p)
            conf_p.append(cf_p)
            chunk_s.append(vn_s[:, None, :])
        h = _gate_up(xb, ssq, norm_ffn[l], w_ffn_gate, w_ffn_up, l)
        x, xb, ssq = _matmul_res(h, w_ffn_down, l, x, tm=DOWN_ROW_TILE, name="ffn_down")
        xres_p, xres_s, xres_s_block = x, x, mp // ns
    y_p = _final_norm(x, norm_final, rows=PROMPT_ROWS, block0=0, nblocks=mp // PROMPT_ROWS)
    y_s = _final_norm(x, norm_final, rows=ns, block0=mp // ns, nblocks=1)
    conv_s, pool_s = (jnp.swapaxes(t, 1, 2) for t in even_new)
    (conf_s,) = (jnp.swapaxes(t, 1, 2) for t in odd_new)
    return (y_p.reshape(batch, seq, d), y_s.reshape(ns, 1, d),
            jnp.stack(conv_p), conv_s, jnp.stack(pool_p), pool_s,
            jnp.stack(chunk_p), jnp.stack(chunk_s), jnp.stack(conf_p), conf_s)
```

```python
import functools

import jax
import jax.numpy as jnp
from jax import lax
from jax.experimental import pallas as pl
from jax.experimental.pallas import tpu as pltpu

F32 = jnp.float32
BF16 = jnp.bfloat16

EPS = 1e-6
PAST_LEN = 16384
POOL_WINDOWS = (2, 4, 8, 16)
CHUNK = 128

LANES = 128
SUBLANES = 8
VMEM_LIMIT_BYTES = 56 * 1024 * 1024

WIDE_ROW_TILE = 4160
DOWN_ROW_TILE = 1040
PROMPT_ROWS = 1024
MIX_TILE = 256
POOL_HALO = 16
CONF_HALO = 32


def _params(*sem):
    return pltpu.CompilerParams(dimension_semantics=sem,
                                vmem_limit_bytes=VMEM_LIMIT_BYTES)


def _rmsnorm(x, g):
    ms = jnp.mean(x * x, axis=-1, keepdims=True)
    return x * lax.rsqrt(ms + EPS) * g


def _layernorm(x, g, b):
    mu = jnp.mean(x, axis=-1, keepdims=True)
    xc = x - mu
    var = jnp.mean(xc * xc, axis=-1, keepdims=True)
    return xc * lax.rsqrt(var + EPS) * g + b


def _dot(a, b):
    return jnp.dot(a, b, preferred_element_type=F32)


def _lane_chunks(n):
    return [slice(c * LANES, (c + 1) * LANES) for c in range(n // LANES)]


def _gain_columns(g):
    return jnp.broadcast_to(g[:, None], (g.shape[0], LANES))


def _emit_stream(o, x_out, xb_out, ssq_out):
    x_out[...] = o
    xb_out[...] = o.astype(BF16)
    ssq_out[...] = jnp.broadcast_to(jnp.sum(o * o, axis=-1, keepdims=True),
                                    ssq_out.shape)


def _stream_shapes(total_rows, d):
    return [jax.ShapeDtypeStruct((total_rows, d), F32),
            jax.ShapeDtypeStruct((total_rows, d), BF16),
            jax.ShapeDtypeStruct((total_rows, LANES), F32)]


def _prep_kernel(x_ref, *refs):
    xb_out, ssq_out = refs[-2:]
    x = x_ref[...]
    xb_out[...] = x.astype(BF16)
    ssq_out[...] = jnp.broadcast_to(jnp.sum(x * x, axis=-1, keepdims=True),
                                    ssq_out.shape)


def _prep(x2d, *, rows, block0, total_rows, prev=None):
    n, d = x2d.shape
    in_specs = [pl.BlockSpec((rows, d), lambda i: (i, 0))]
    args = [x2d]
    aliases = {}
    if prev is not None:
        in_specs += [pl.BlockSpec(memory_space=pl.ANY)] * 2
        args += list(prev)
        aliases = {1: 0, 2: 1}
    return pl.pallas_call(
        _prep_kernel,
        grid=(n // rows,),
        in_specs=in_specs,
        out_specs=[pl.BlockSpec((rows, d), lambda i: (block0 + i, 0)),
                   pl.BlockSpec((rows, LANES), lambda i: (block0 + i, 0))],
        out_shape=_stream_shapes(total_rows, d)[1:],
        input_output_aliases=aliases,
        compiler_params=_params("arbitrary"),
        name="stream_entry",
    )(*args)


def _resident_rows(tm, k):
    return pl.BlockSpec((tm, k), lambda i, j: (i, 0), pipeline_mode=pl.Buffered(1))


def _scaled_weight(w_ref, g_ref, wb_ref):
    for sl in _lane_chunks(w_ref.shape[1]):
        wb_ref[:, sl] = (w_ref[:, sl] * g_ref[...]).astype(BF16)


def _row_scale(ssq_ref, d):
    return lax.rsqrt(ssq_ref[...] / d + EPS)


def _proj_kernel(xb_ref, ssq_ref, g_ref, w_ref, o_ref, wb_ref):
    _scaled_weight(w_ref, g_ref, wb_ref)
    r = _row_scale(ssq_ref, xb_ref.shape[1])
    acc = _dot(xb_ref[...], wb_ref[...])
    for sl in _lane_chunks(o_ref.shape[1]):
        o_ref[:, sl] = (acc[:, sl] * r).astype(o_ref.dtype)


def _proj(xb, ssq, g, w_stack, layer, *, tn=512):
    m, d = xb.shape
    n = w_stack.shape[2]
    return pl.pallas_call(
        _proj_kernel,
        grid=(m // WIDE_ROW_TILE, n // tn),
        in_specs=[_resident_rows(WIDE_ROW_TILE, d),
                  pl.BlockSpec((WIDE_ROW_TILE, LANES), lambda i, j: (i, 0)),
                  pl.BlockSpec((d, LANES), lambda i, j: (0, 0)),
                  pl.BlockSpec((None, d, tn), lambda i, j: (layer, 0, j))],
        out_specs=pl.BlockSpec((WIDE_ROW_TILE, tn), lambda i, j: (i, j)),
        out_shape=jax.ShapeDtypeStruct((m, n), BF16),
        scratch_shapes=[pltpu.VMEM((d, tn), BF16)],
        compiler_params=_params("parallel", "arbitrary"),
        name="mixer_in_proj",
    )(xb, ssq, _gain_columns(g), w_stack)


def _gate_up_kernel(xb_ref, ssq_ref, g_ref, wg_ref, wu_ref, h_ref, wgb_ref, wub_ref):
    _scaled_weight(wg_ref, g_ref, wgb_ref)
    _scaled_weight(wu_ref, g_ref, wub_ref)
    r = _row_scale(ssq_ref, xb_ref.shape[1])
    xb = xb_ref[...]
    gate = _dot(xb, wgb_ref[...])
    up = _dot(xb, wub_ref[...])
    for sl in _lane_chunks(h_ref.shape[1]):
        gt = gate[:, sl] * r
        h_ref[:, sl] = (gt * jax.nn.sigmoid(gt) * (up[:, sl] * r)).astype(BF16)


def _gate_up(xb, ssq, g, wg_stack, wu_stack, layer, *, tn=256):
    m, d = xb.shape
    f = wg_stack.shape[2]

    def wspec():
        return pl.BlockSpec((None, d, tn), lambda i, j: (layer, 0, j))

    return pl.pallas_call(
        _gate_up_kernel,
        grid=(m // WIDE_ROW_TILE, f // tn),
        in_specs=[_resident_rows(WIDE_ROW_TILE, d),
                  pl.BlockSpec((WIDE_ROW_TILE, LANES), lambda i, j: (i, 0)),
                  pl.BlockSpec((d, LANES), lambda i, j: (0, 0)),
                  wspec(), wspec()],
        out_specs=pl.BlockSpec((WIDE_ROW_TILE, tn), lambda i, j: (i, j)),
        out_shape=jax.ShapeDtypeStruct((m, f), BF16),
        scratch_shapes=[pltpu.VMEM((d, tn), BF16), pltpu.VMEM((d, tn), BF16)],
        compiler_params=_params("parallel", "arbitrary"),
        name="ffn_gate_up",
    )(xb, ssq, _gain_columns(g), wg_stack, wu_stack)


def _matmul_res_kernel(a_ref, w_ref, r_ref, o_ref, ob_ref, ssq_ref):
    o = r_ref[...] + _dot(a_ref[...], w_ref[...].astype(BF16))
    o_ref[...] = o
    ob_ref[...] = o.astype(BF16)
    part = jnp.broadcast_to(jnp.sum(o * o, axis=-1, keepdims=True), ssq_ref.shape)

    @pl.when(pl.program_id(1) == 0)
    def _():
        ssq_ref[...] = part

    @pl.when(pl.program_id(1) > 0)
    def _():
        ssq_ref[...] += part


def _matmul_res(a, w_stack, layer, r, *, tm, tn=256, name):
    m, k = a.shape
    n = w_stack.shape[2]
    return pl.pallas_call(
        _matmul_res_kernel,
        grid=(m // tm, n // tn),
        in_specs=[pl.BlockSpec((tm, k), lambda i, j: (i, 0)),
                  pl.BlockSpec((None, k, tn), lambda i, j: (layer, 0, j)),
                  pl.BlockSpec((tm, tn), lambda i, j: (i, j))],
        out_specs=[pl.BlockSpec((tm, tn), lambda i, j: (i, j)),
                   pl.BlockSpec((tm, tn), lambda i, j: (i, j)),
                   pl.BlockSpec((tm, LANES), lambda i, j: (i, 0))],
        out_shape=_stream_shapes(m, n),
        compiler_params=_params("parallel", "arbitrary"),
        name=name,
    )(a, w_stack, r)


def _cast_kernel(w_ref, o_ref):
    o_ref[...] = w_ref[...].astype(o_ref.dtype)


def _cast_bf16(w_stack, layer, *, rows=512):
    k, n = w_stack.shape[1:]
    return pl.pallas_call(
        _cast_kernel,
        grid=(k // rows,),
        in_specs=[pl.BlockSpec((None, rows, n), lambda i: (layer, i, 0))],
        out_specs=pl.BlockSpec((rows, n), lambda i: (i, 0)),
        out_shape=jax.ShapeDtypeStruct((k, n), BF16),
        compiler_params=_params("parallel"),
        name="out_proj_weight_bf16",
    )(w_stack)


def _final_norm_kernel(x_ref, g_ref, o_ref):
    o_ref[...] = _rmsnorm(x_ref[...], g_ref[...])


def _final_norm(x, g, *, rows, block0, nblocks):
    d = x.shape[1]
    return pl.pallas_call(
        _final_norm_kernel,
        grid=(nblocks,),
        in_specs=[pl.BlockSpec((rows, d), lambda i: (block0 + i, 0)),
                  pl.BlockSpec((1, d), lambda i: (0, 0))],
        out_specs=pl.BlockSpec((rows, d), lambda i: (i, 0)),
        out_shape=jax.ShapeDtypeStruct((rows * nblocks, d), F32),
        compiler_params=_params("parallel"),
        name="final_norm",
    )(x, g.reshape(1, d))


PROJ_CHUNK = 256


def _pipelined_tail(mixer_stages, cat_a, cat_b, wob_ref, xres_ref, stream_outs, *,
                    interleave):
    s = pl.program_id(0)
    x_out, xb_out, ssq_out = stream_outs
    nchunks = wob_ref.shape[1] // PROJ_CHUNK

    @pl.when(s == 0)
    def _():
        cat_b[...] = jnp.zeros_like(cat_b)

    def step(cat_w, cat_r):
        stages = mixer_stages(cat_w)
        if not interleave:
            for stage in stages:
                stage()
            o = xres_ref[...] + _dot(cat_r[...], wob_ref[...])
            _emit_stream(o, *stream_outs)
            return
        ssq = jnp.zeros((x_out.shape[0], 1), F32)
        done = 0
        for k, stage in enumerate(stages):
            upto = (k + 1) * nchunks // len(stages)
            for c in range(done, upto):
                sl = slice(c * PROJ_CHUNK, (c + 1) * PROJ_CHUNK)
                o = xres_ref[:, sl] + _dot(cat_r[...], wob_ref[:, sl])
                x_out[:, sl] = o
                xb_out[:, sl] = o.astype(BF16)
                ssq = ssq + jnp.sum(o * o, axis=-1, keepdims=True)
            done = upto
            stage()
        ssq_out[...] = jnp.broadcast_to(ssq, ssq_out.shape)

    @pl.when(s % 2 == 0)
    def _():
        step(cat_a, cat_b)

    @pl.when(s % 2 == 1)
    def _():
        step(cat_b, cat_a)


def _tail_specs(ntiles, tt, d):
    def mixed(s):
        return jnp.minimum(s, ntiles - 1)

    def projected(s):
        return jnp.maximum(s - 1, 0)

    wob = pl.BlockSpec((d, d), lambda s: (0, 0), pipeline_mode=pl.Buffered(1))
    xres = pl.BlockSpec((tt, d), lambda s: (projected(s), 0))
    outs = [pl.BlockSpec((tt, d), lambda s: (projected(s), 0)),
            pl.BlockSpec((tt, d), lambda s: (projected(s), 0)),
            pl.BlockSpec((tt, LANES), lambda s: (projected(s), 0))]
    return mixed, wob, xres, outs


def _pool_project(pooled, w_pool_ref, scale_ref, cat_ref, col0):
    gw = w_pool_ref.shape[1]
    for g in range(len(POOL_WINDOWS)):
        sl = slice(g * gw, (g + 1) * gw)
        yb = _dot(pooled[:, sl].astype(BF16), w_pool_ref[g].astype(BF16))
        cat_ref[:, col0 + g * gw:col0 + (g + 1) * gw] = (
            yb * scale_ref[:, sl]).astype(cat_ref.dtype)


def _even_prompt_kernel(nt, ntiles,
                        xin_ref, gpre_ref, gpost_ref, p_ref,
                        hxin_ref, hgpre_ref, hp_ref,
                        wconv_ref, wpool_ref, scale_ref, wob_ref, xres_ref,
                        x_out, xb_out, ssq_out, conv_state_ref, pool_state_ref,
                        gx_ext, p_ext, cat_a, cat_b):
    tt, da = xin_ref.shape
    h = POOL_HALO
    i = jnp.minimum(pl.program_id(0), ntiles - 1) % nt
    live = (i > 0).astype(F32)

    def mixer_stages(cat_ref):
        def short_conv():
            gx = gpre_ref[...].astype(F32) * xin_ref[...].astype(F32)
            gx_ext[0:h, :] = (hgpre_ref[...].astype(F32) * hxin_ref[...].astype(F32)
                              * live)
            gx_ext[h:h + tt, :] = gx
            conv = (gx_ext[pl.ds(h - 2, tt), :] * wconv_ref[0:1, :]
                    + gx_ext[pl.ds(h - 1, tt), :] * wconv_ref[1:2, :]
                    + gx * wconv_ref[2:3, :])
            cat_ref[:, 0:da] = (gpost_ref[...].astype(F32) * conv).astype(cat_ref.dtype)
            conv_state_ref[...] = gx_ext[pl.ds(h + tt - 2, 2), :]
            p_ext[0:h, :] = hp_ref[...].astype(F32) * live
            p_ext[h:h + tt, :] = p_ref[...].astype(F32)
            nh = pool_state_ref.shape[0]
            pool_state_ref[...] = p_ext[pl.ds(h + tt - nh, nh), :]

        def pool_group(g, w):
            def run():
                gw = da // len(POOL_WINDOWS)
                sl = slice(g * gw, (g + 1) * gw)
                pg = p_ext[pl.ds(h, tt), sl]
                acc = pg
                for k in range(1, w):
                    acc = acc + p_ext[pl.ds(h - k, tt), sl]
                pos = i * tt + lax.broadcasted_iota(jnp.int32, (tt, 1), 0)
                cnt = jnp.minimum(w, pos + 1).astype(F32)
                pooled = acc / cnt - pg
                yb = _dot(pooled.astype(BF16), wpool_ref[g].astype(BF16))
                cat_ref[:, da + g * gw:da + (g + 1) * gw] = (
                    yb * scale_ref[:, sl]).astype(cat_ref.dtype)
            return run

        return [short_conv] + [pool_group(g, w) for g, w in enumerate(POOL_WINDOWS)]

    _pipelined_tail(mixer_stages, cat_a, cat_b, wob_ref, xres_ref,
                    (x_out, xb_out, ssq_out), interleave=True)


def _even_prompt(proj, x, w_conv, w_pool, pool_scale, wob, layer, *, batch, seq,
                 total_rows):
    da = w_conv.shape[2]
    d = x.shape[1]
    tt = MIX_TILE
    nt = seq // tt
    ntiles = batch * nt
    hb = tt // POOL_HALO
    nwin = len(POOL_WINDOWS)
    nh = POOL_WINDOWS[-1] - 1
    mixed, wob_spec, xres_spec, stream_specs = _tail_specs(ntiles, tt, d)

    def main(c):
        return pl.BlockSpec((tt, da), lambda s: (mixed(s), c))

    def halo(c):
        return pl.BlockSpec((POOL_HALO, da),
                            lambda s: (jnp.maximum(mixed(s) * hb - 1, 0), c))

    return pl.pallas_call(
        functools.partial(_even_prompt_kernel, nt, ntiles),
        grid=(ntiles + 1,),
        in_specs=[main(0), main(1), main(2), main(3), halo(0), halo(1), halo(3),
                  pl.BlockSpec((None, 3, da), lambda s: (layer, 0, 0)),
                  pl.BlockSpec((None, nwin, da // nwin, da // nwin),
                               lambda s: (layer, 0, 0, 0)),
                  pl.BlockSpec((None, 1, da), lambda s: (layer, 0, 0)),
                  wob_spec, xres_spec],
        out_specs=stream_specs + [
            pl.BlockSpec((None, 2, da), lambda s: (mixed(s) // nt, 0, 0)),
            pl.BlockSpec((None, nh, da), lambda s: (mixed(s) // nt, 0, 0))],
        out_shape=_stream_shapes(total_rows, d) + [
            jax.ShapeDtypeStruct((batch, 2, da), F32),
            jax.ShapeDtypeStruct((batch, nh, da), F32)],
        scratch_shapes=[pltpu.VMEM((POOL_HALO + tt, da), F32),
                        pltpu.VMEM((POOL_HALO + tt, da), F32),
                        pltpu.VMEM((tt, d), BF16),
                        pltpu.VMEM((tt, d), BF16)],
        compiler_params=_params("arbitrary"),
        name="even_mixer_prompt",
    )(proj, proj, proj, proj, proj, proj, proj,
      w_conv, w_pool, pool_scale.reshape(pool_scale.shape[0], 1, da), wob, x)


def _odd_prompt_kernel(nt, ntiles,
                       u_ref, v_ref, a_ref, g_ref, ha_ref, hg_ref,
                       vng_ref, vnb_ref, ws_ref, bst_ref,
                       wconv_ref, bconv_ref, cng_ref, cnb_ref, wob_ref, xres_ref,
                       x_out, xb_out, ssq_out, chunk_v_ref, conf_state_ref,
                       glu_ext, shifted, cat_a, cat_b):
    tt, dc = u_ref.shape
    h = CONF_HALO
    i = jnp.minimum(pl.program_id(0), ntiles - 1) % nt
    live = (i > 0).astype(F32)

    nheads = ws_ref.shape[0]
    hd = dc // nheads
    nw = wconv_ref.shape[0]

    def mixer_stages(cat_ref):
        vals = {}

        def gate_inputs():
            vals["u"] = jax.nn.gelu(u_ref[...].astype(F32))
            vn = _layernorm(jax.nn.gelu(v_ref[...].astype(F32)), vng_ref[...],
                            vnb_ref[...])
            chunk_v_ref[...] = vn[tt - CHUNK:tt, :]
            vals["vb"] = vn.astype(BF16)

        def gate_heads(heads):
            def run():
                row = lax.broadcasted_iota(jnp.int32, (CHUNK, CHUNK), 0)
                colm = lax.broadcasted_iota(jnp.int32, (CHUNK, CHUNK), 1)
                for hh in heads:
                    ws = jnp.where(colm <= row, ws_ref[hh], 0.0).astype(BF16)
                    bias = bst_ref[:, hh:hh + 1]
                    for c in range(tt // CHUNK):
                        rs = slice(c * CHUNK, (c + 1) * CHUNK)
                        cs = slice(hh * hd, (hh + 1) * hd)
                        mixed = _dot(ws, vals["vb"][rs, cs]) + bias
                        cat_ref[rs, cs] = (vals["u"][rs, cs] * mixed).astype(cat_ref.dtype)
            return run

        def glu_rows():
            glu = a_ref[...].astype(F32) * jax.nn.sigmoid(g_ref[...].astype(F32))
            glu_ext[0:h, :] = (ha_ref[...].astype(F32)
                               * jax.nn.sigmoid(hg_ref[...].astype(F32)) * live)
            glu_ext[h:h + tt, :] = glu
            conf_state_ref[...] = glu_ext[pl.ds(h + tt - (nw - 1), nw - 1), :]
            span = shifted.shape[1]
            for sh in range(1, SUBLANES):
                shifted[sh - 1] = glu_ext[pl.ds(sh, span), :]
            vals["acc"] = glu * wconv_ref[nw - 1:nw, :]

        def taps(ks):
            def run():
                acc = vals["acc"]
                for k in ks:
                    q, sh = divmod(h - (nw - 1) + k, SUBLANES)
                    if sh == 0:
                        tap = glu_ext[pl.ds(q * SUBLANES, tt), :]
                    else:
                        tap = shifted[sh - 1, pl.ds(q * SUBLANES, tt), :]
                    acc = acc + tap * wconv_ref[k:k + 1, :]
                vals["acc"] = acc
            return run

        def conv_norm():
            y = _layernorm(vals["acc"] + bconv_ref[...], cng_ref[...], cnb_ref[...])
            cat_ref[:, dc:2 * dc] = (y * jax.nn.sigmoid(y)).astype(cat_ref.dtype)

        third = (nw - 1) // 3
        return [gate_inputs,
                gate_heads(range(0, nheads // 2)), gate_heads(range(nheads // 2, nheads)),
                glu_rows,
                taps(range(0, third)), taps(range(third, 2 * third)),
                taps(range(2 * third, nw - 1)),
                conv_norm]

    _pipelined_tail(mixer_stages, cat_a, cat_b, wob_ref, xres_ref,
                    (x_out, xb_out, ssq_out), interleave=False)


def _odd_prompt(proj, x, v_norm_g, v_norm_b, w_spatial, b_spatial_t, w_conv, b_conv,
                conf_norm_g, conf_norm_b, wob, layer, *, batch, seq, total_rows):
    nw, dc = w_conv.shape[1:]
    nheads = w_spatial.shape[1]
    d = x.shape[1]
    tt = MIX_TILE
    nt = seq // tt
    ntiles = batch * nt
    hb = tt // CONF_HALO
    mixed, wob_spec, xres_spec, stream_specs = _tail_specs(ntiles, tt, d)

    def main(c):
        return pl.BlockSpec((tt, dc), lambda s: (mixed(s), c))

    def halo(c):
        return pl.BlockSpec((CONF_HALO, dc),
                            lambda s: (jnp.maximum(mixed(s) * hb - 1, 0), c))

    def vec():
        return pl.BlockSpec((None, 1, dc), lambda s: (layer, 0, 0))

    def as_rows(v):
        return v.reshape(v.shape[0], 1, dc)

    return pl.pallas_call(
        functools.partial(_odd_prompt_kernel, nt, ntiles),
        grid=(ntiles + 1,),
        in_specs=[main(0), main(1), main(2), main(3), halo(2), halo(3),
                  vec(), vec(),
                  pl.BlockSpec((None, nheads, CHUNK, CHUNK), lambda s: (layer, 0, 0, 0)),
                  pl.BlockSpec((None, CHUNK, nheads), lambda s: (layer, 0, 0)),
                  pl.BlockSpec((None, nw, dc), lambda s: (layer, 0, 0)),
                  vec(), vec(), vec(), wob_spec, xres_spec],
        out_specs=stream_specs + [
            pl.BlockSpec((None, CHUNK, dc), lambda s: (mixed(s) // nt, 0, 0)),
            pl.BlockSpec((None, nw - 1, dc), lambda s: (mixed(s) // nt, 0, 0))],
        out_shape=_stream_shapes(total_rows, d) + [
            jax.ShapeDtypeStruct((batch, CHUNK, dc), F32),
            jax.ShapeDtypeStruct((batch, nw - 1, dc), F32)],
        scratch_shapes=[pltpu.VMEM((CONF_HALO + tt, dc), F32),
                        pltpu.VMEM((SUBLANES - 1, CONF_HALO - SUBLANES + tt, dc), F32),
                        pltpu.VMEM((tt, d), BF16),
                        pltpu.VMEM((tt, d), BF16)],
        compiler_params=_params("arbitrary"),
        name="odd_mixer_prompt",
    )(proj, proj, proj, proj, proj, proj,
      as_rows(v_norm_g), as_rows(v_norm_b), w_spatial, b_spatial_t,
      w_conv, as_rows(b_conv), as_rows(conf_norm_g), as_rows(conf_norm_b), wob, x)


def _push_copies(hist_ref, new_any, layer, new_row, sems, slot):
    k = hist_ref.shape[1]
    shift = pltpu.make_async_copy(hist_ref.at[0, pl.ds(1, k - 1)],
                                  new_any.at[layer, pl.ds(0, k - 1)], sems.at[slot])
    last = pltpu.make_async_copy(new_row, new_any.at[layer, k - 1], sems.at[slot + 1])
    return shift, last


def _even_sample_kernel(layer, nprev,
                        xin_ref, gpre_ref, gpost_ref, p_ref,
                        conv_hist_ref, pool_hist_ref,
                        wconv_ref, wpool_ref, scale_ref, wob_ref, xres_ref,
                        x_in, xb_in, ssq_in, *refs):
    del x_in, xb_in, ssq_in
    x_out, xb_out, ssq_out, new_conv, new_pool, cat_ref, gx_row, p_row, sems = refs[nprev:]
    s, da = xin_ref.shape
    conv_shift, conv_last = _push_copies(conv_hist_ref, new_conv, layer, gx_row, sems, 0)
    pool_shift, pool_last = _push_copies(pool_hist_ref, new_pool, layer, p_row, sems, 2)
    conv_shift.start()
    pool_shift.start()

    gx = gpre_ref[...].astype(F32) * xin_ref[...].astype(F32)
    gx_row[...] = gx
    conv_last.start()
    nc = conv_hist_ref.shape[1]
    conv = gx * wconv_ref[nc:nc + 1, :]
    for k in range(nc):
        conv = conv + conv_hist_ref[0, k] * wconv_ref[k:k + 1, :]
    cat_ref[:, 0:da] = (gpost_ref[...].astype(F32) * conv).astype(cat_ref.dtype)

    p = p_ref[...].astype(F32)
    p_row[...] = p
    pool_last.start()
    nh = pool_hist_ref.shape[1]
    gw = da // len(POOL_WINDOWS)
    pooled = []
    for g, w in enumerate(POOL_WINDOWS):
        sl = slice(g * gw, (g + 1) * gw)
        acc = p[:, sl]
        for back in range(1, w):
            acc = acc + pool_hist_ref[0, nh - back, :, sl]
        cnt = float(min(w, PAST_LEN + 1))
        pooled.append(acc / cnt - p[:, sl])
    pooled = jnp.concatenate(pooled, axis=-1)
    _pool_project(pooled, wpool_ref, scale_ref, cat_ref, da)
    o = xres_ref[...] + _dot(cat_ref[...], wob_ref[...])
    _emit_stream(o, x_out, xb_out, ssq_out)
    conv_shift.wait()
    pool_shift.wait()
    conv_last.wait()
    pool_last.wait()


def _sample_stream_specs(s, d, xres_block, out_block):
    wob = pl.BlockSpec((d, d), lambda i: (0, 0), pipeline_mode=pl.Buffered(1))
    xres = pl.BlockSpec((s, d), lambda i: (xres_block, 0))
    anys = [pl.BlockSpec(memory_space=pl.ANY)] * 3
    outs = [pl.BlockSpec((s, d), lambda i: (out_block, 0)),
            pl.BlockSpec((s, d), lambda i: (out_block, 0)),
            pl.BlockSpec((s, LANES), lambda i: (out_block, 0))]
    return wob, xres, anys, outs


def _history_block(hist_t, layer):
    return pl.BlockSpec((1,) + hist_t.shape[1:], lambda i: (layer, 0, 0, 0),
                        pipeline_mode=pl.Buffered(1))


def _even_sample(proj, xres, xres_block, stream, conv_hist_t, pool_hist_t, new_states,
                 w_conv, w_pool, pool_scale, wob, layer, *, row0):
    nc, s, da = conv_hist_t.shape[1:]
    total_rows, d = stream[0].shape
    rb = row0 // s
    nwin = len(POOL_WINDOWS)
    wob_spec, xres_spec, any_specs, stream_specs = _sample_stream_specs(
        s, d, xres_block, rb)
    nprev = len(new_states)
    any_spec = pl.BlockSpec(memory_space=pl.ANY)

    def seg(c):
        return pl.BlockSpec((s, da), lambda i: (rb, c))

    in_specs = [seg(0), seg(1), seg(2), seg(3),
                _history_block(conv_hist_t, layer), _history_block(pool_hist_t, layer),
                pl.BlockSpec((None, nc + 1, da), lambda i: (layer, 0, 0)),
                pl.BlockSpec((None, nwin, da // nwin, da // nwin),
                             lambda i: (layer, 0, 0, 0)),
                pl.BlockSpec((None, 1, da), lambda i: (layer, 0, 0)),
                wob_spec, xres_spec] + any_specs + [any_spec] * nprev
    first_prev = len(in_specs) - nprev
    aliases = {11: 0, 12: 1, 13: 2}
    aliases.update({first_prev + k: 3 + k for k in range(nprev)})
    return pl.pallas_call(
        functools.partial(_even_sample_kernel, layer, nprev),
        grid=(1,),
        in_specs=in_specs,
        out_specs=stream_specs + [any_spec, any_spec],
        out_shape=_stream_shapes(total_rows, d) + [
            jax.ShapeDtypeStruct(conv_hist_t.shape, F32),
            jax.ShapeDtypeStruct(pool_hist_t.shape, F32)],
        scratch_shapes=[pltpu.VMEM((s, d), BF16), pltpu.VMEM((s, da), F32),
                        pltpu.VMEM((s, da), F32), pltpu.SemaphoreType.DMA((4,))],
        input_output_aliases=aliases,
        compiler_params=_params("arbitrary"),
        name="even_mixer_sample",
    )(proj, proj, proj, proj, conv_hist_t, pool_hist_t,
      w_conv, w_pool, pool_scale.reshape(pool_scale.shape[0], 1, da), wob, xres, *stream,
      *new_states)


def _odd_sample_kernel(layer, nprev,
                       u_ref, v_ref, a_ref, g_ref, hist_ref,
                       vng_ref, vnb_ref, ws0_ref, b0_ref,
                       wconv_ref, bconv_ref, cng_ref, cnb_ref, wob_ref, xres_ref,
                       x_in, xb_in, ssq_in, *refs):
    del x_in, xb_in, ssq_in
    x_out, xb_out, ssq_out, vn_ref, new_hist, cat_ref, glu_row, sems = refs[nprev:]
    s, dc = u_ref.shape
    shift, last = _push_copies(hist_ref, new_hist, layer, glu_row, sems, 0)
    shift.start()
    glu = a_ref[...].astype(F32) * jax.nn.sigmoid(g_ref[...].astype(F32))
    glu_row[...] = glu
    last.start()

    u = jax.nn.gelu(u_ref[...].astype(F32))
    vn = _layernorm(jax.nn.gelu(v_ref[...].astype(F32)), vng_ref[...], vnb_ref[...])
    vn_ref[...] = vn
    mixed = ws0_ref[...] * vn + b0_ref[...]
    cat_ref[:, 0:dc] = (u * mixed).astype(cat_ref.dtype)

    nh = hist_ref.shape[1]
    acc = glu * wconv_ref[nh:nh + 1, :]
    for k in range(nh):
        acc = acc + hist_ref[0, k] * wconv_ref[k:k + 1, :]
    y = _layernorm(acc + bconv_ref[...], cng_ref[...], cnb_ref[...])
    cat_ref[:, dc:2 * dc] = (y * jax.nn.sigmoid(y)).astype(cat_ref.dtype)
    o = xres_ref[...] + _dot(cat_ref[...], wob_ref[...])
    _emit_stream(o, x_out, xb_out, ssq_out)
    shift.wait()
    last.wait()


def _odd_sample(proj, xres, xres_block, stream, hist_t, new_states, v_norm_g, v_norm_b,
                w_spatial, b_spatial, w_conv, b_conv, conf_norm_g, conf_norm_b, wob, layer,
                *, row0):
    nh, s, dc = hist_t.shape[1:]
    nheads = w_spatial.shape[1]
    hd = dc // nheads
    total_rows, d = stream[0].shape
    rb = row0 // s
    ws0 = jnp.repeat(w_spatial[layer, :, 0, 0], hd).reshape(1, dc)
    b0 = jnp.repeat(b_spatial[layer, :, 0], hd).reshape(1, dc)
    wob_spec, xres_spec, any_specs, stream_specs = _sample_stream_specs(
        s, d, xres_block, rb)
    nprev = len(new_states)
    any_spec = pl.BlockSpec(memory_space=pl.ANY)

    def seg(c):
        return pl.BlockSpec((s, dc), lambda i: (rb, c))

    def vec():
        return pl.BlockSpec((None, 1, dc), lambda i: (layer, 0, 0))

    def row():
        return pl.BlockSpec((1, dc), lambda i: (0, 0))

    def as_rows(v):
        return v.reshape(v.shape[0], 1, dc)

    in_specs = [seg(0), seg(1), seg(2), seg(3), _history_block(hist_t, layer),
                vec(), vec(), row(), row(),
                pl.BlockSpec((None, nh + 1, dc), lambda i: (layer, 0, 0)),
                vec(), vec(), vec(), wob_spec, xres_spec] + any_specs + (
                    [any_spec] * nprev)
    first_prev = len(in_specs) - nprev
    aliases = {15: 0, 16: 1, 17: 2}
    aliases.update({first_prev + k: 4 + k for k in range(nprev)})
    return pl.pallas_call(
        functools.partial(_odd_sample_kernel, layer, nprev),
        grid=(1,),
        in_specs=in_specs,
        out_specs=stream_specs + [pl.BlockSpec((s, dc), lambda i: (0, 0)), any_spec],
        out_shape=_stream_shapes(total_rows, d) + [
            jax.ShapeDtypeStruct((s, dc), F32),
            jax.ShapeDtypeStruct(hist_t.shape, F32)],
        scratch_shapes=[pltpu.VMEM((s, d), BF16), pltpu.VMEM((s, dc), F32),
                        pltpu.SemaphoreType.DMA((2,))],
        input_output_aliases=aliases,
        compiler_params=_params("arbitrary"),
        name="odd_mixer_sample",
    )(proj, proj, proj, proj, hist_t,
      as_rows(v_norm_g), as_rows(v_norm_b), ws0, b0,
      w_conv, as_rows(b_conv), as_rows(conf_norm_g), as_rows(conf_norm_b), wob, xres,
      *stream, *new_states)


def kernel(x_prompt, x_sample, state_conv_a, state_pool, state_conformer, norm_mix, norm_ffn, w_in_even, w_conv_a, w_pool, pool_scale, w_out_even, w_in_odd, v_norm_g, v_norm_b, w_spatial, b_spatial, w_conv_d, b_conv_d, conf_norm_g, conf_norm_b, w_out_odd, w_ffn_gate, w_ffn_up, w_ffn_down, norm_final):
    batch, seq, d = x_prompt.shape
    ns = x_sample.shape[0]
    mp = batch * seq
    m = mp + ns
    depth = norm_mix.shape[0]
    da = w_conv_a.shape[2]
    assert x_sample.shape[1] == 1 and mp % ns == 0 and mp % PROMPT_ROWS == 0
    assert m % WIDE_ROW_TILE == 0 and m % DOWN_ROW_TILE == 0
    assert seq % MIX_TILE == 0 and MIX_TILE % CHUNK == 0

    x_prompt2d = x_prompt.reshape(mp, d)
    x_sample2d = x_sample.reshape(ns, d)
    entry = _prep(x_prompt2d, rows=PROMPT_ROWS, block0=0, total_rows=m)
    xb, ssq = _prep(x_sample2d, rows=ns, block0=mp // ns, total_rows=m, prev=entry)
    xres_p, xres_s, xres_s_block = x_prompt2d, x_sample2d, 0
    b_spatial_t = jnp.swapaxes(b_spatial, 1, 2)
    conv_hist_t = jnp.swapaxes(state_conv_a, 1, 2)
    pool_hist_t = jnp.swapaxes(state_pool, 1, 2)
    conf_hist_t = jnp.swapaxes(state_conformer, 1, 2)
    even_new, odd_new = [], []
    conv_p, pool_p, chunk_p, chunk_s, conf_p = [], [], [], [], []
    for l in range(depth):
        i = l // 2
        if l % 2 == 0:
            proj = _proj(xb, ssq, norm_mix[l], w_in_even, i)
            wob = _cast_bf16(w_out_even, i)
            *stream, cs_p, ps_p = _even_prompt(proj, xres_p, w_conv_a, w_pool, pool_scale,
                                               wob, i, batch=batch, seq=seq, total_rows=m)
            x, xb, ssq, *even_new = _even_sample(
                proj, xres_s, xres_s_block, stream, conv_hist_t, pool_hist_t, even_new,
                w_conv_a, w_pool, pool_scale, wob, i, row0=mp)
            conv_p.append(cs_p)
            pool_p.append(ps_p)
        else:
            proj = _proj(xb, ssq, norm_mix[l], w_in_odd, i)
            wob = _cast_bf16(w_out_odd, i)
            *stream, cv_p, cf_p = _odd_prompt(
                proj, xres_p, v_norm_g, v_norm_b, w_spatial, b_spatial_t, w_conv_d, b_conv_d,
                conf_norm_g, conf_norm_b, wob, i, batch=batch, seq=seq, total_rows=m)
            x, xb, ssq, vn_s, *odd_new = _odd_sample(
                proj, xres_s, xres_s_block, stream, conf_hist_t, odd_new, v_norm_g, v_norm_b,
                w_spatial, b_spatial, w_conv_d, b_conv_d, conf_norm_g, conf_norm_b, wob, i,
                row0=mp)
            chunk_p.append(cv_p)
            conf_p.append(cf_p)
            chunk_s.append(vn_s[:, None, :])
        h = _gate_up(xb, ssq, norm_ffn[l], w_ffn_gate, w_ffn_up, l)
        x, xb, ssq = _matmul_res(h, w_ffn_down, l, x, tm=DOWN_ROW_TILE, name="ffn_down")
        xres_p, xres_s, xres_s_block = x, x, mp // ns
    y_p = _final_norm(x, norm_final, rows=PROMPT_ROWS, block0=0, nblocks=mp // PROMPT_ROWS)
    y_s = _final_norm(x, norm_final, rows=ns, block0=mp // ns, nblocks=1)
    conv_s, pool_s = (jnp.swapaxes(t, 1, 2) for t in even_new)
    (conf_s,) = (jnp.swapaxes(t, 1, 2) for t in odd_new)
    return (y_p.reshape(batch, seq, d), y_s.reshape(ns, 1, d),
            jnp.stack(conv_p), conv_s, jnp.stack(pool_p), pool_s,
            jnp.stack(chunk_p), jnp.stack(chunk_s), jnp.stack(conf_p), conf_s)
```

```python
import functools

import jax
import jax.numpy as jnp
from jax import lax
from jax.experimental import pallas as pl
from jax.experimental.pallas import tpu as pltpu

F32 = jnp.float32
BF16 = jnp.bfloat16

EPS = 1e-6
PAST_LEN = 16384
POOL_WINDOWS = (2, 4, 8, 16)
CHUNK = 128

LANES = 128
SUBLANES = 8
VMEM_LIMIT_BYTES = 56 * 1024 * 1024

WIDE_ROW_TILE = 4160
DOWN_ROW_TILE = 1040
PROMPT_ROWS = 1024
MIX_TILE = 256
EVEN_MIX_TILE = 256
POOL_HALO = 16
CONF_HALO = 32


def _params(*sem):
    return pltpu.CompilerParams(dimension_semantics=sem,
                                vmem_limit_bytes=VMEM_LIMIT_BYTES)


def _rmsnorm(x, g):
    ms = jnp.mean(x * x, axis=-1, keepdims=True)
    return x * lax.rsqrt(ms + EPS) * g


def _layernorm(x, g, b):
    mu = jnp.mean(x, axis=-1, keepdims=True)
    xc = x - mu
    var = jnp.mean(xc * xc, axis=-1, keepdims=True)
    return xc * lax.rsqrt(var + EPS) * g + b


def _dot(a, b):
    return jnp.dot(a, b, preferred_element_type=F32)


def _lane_chunks(n):
    return [slice(c * LANES, (c + 1) * LANES) for c in range(n // LANES)]


def _gain_columns(g):
    return jnp.broadcast_to(g[:, None], (g.shape[0], LANES))


def _emit_stream(o, x_out, xb_out, ssq_out):
    x_out[...] = o
    xb_out[...] = o.astype(BF16)
    ssq_out[...] = jnp.broadcast_to(jnp.sum(o * o, axis=-1, keepdims=True),
                                    ssq_out.shape)


def _stream_shapes(total_rows, d):
    return [jax.ShapeDtypeStruct((total_rows, d), F32),
            jax.ShapeDtypeStruct((total_rows, d), BF16),
            jax.ShapeDtypeStruct((total_rows, LANES), F32)]


def _prep_kernel(x_ref, *refs):
    xb_out, ssq_out = refs[-2:]
    x = x_ref[...]
    xb_out[...] = x.astype(BF16)
    ssq_out[...] = jnp.broadcast_to(jnp.sum(x * x, axis=-1, keepdims=True),
                                    ssq_out.shape)


def _prep(x2d, *, rows, block0, total_rows, prev=None):
    n, d = x2d.shape
    in_specs = [pl.BlockSpec((rows, d), lambda i: (i, 0))]
    args = [x2d]
    aliases = {}
    if prev is not None:
        in_specs += [pl.BlockSpec(memory_space=pl.ANY)] * 2
        args += list(prev)
        aliases = {1: 0, 2: 1}
    return pl.pallas_call(
        _prep_kernel,
        grid=(n // rows,),
        in_specs=in_specs,
        out_specs=[pl.BlockSpec((rows, d), lambda i: (block0 + i, 0)),
                   pl.BlockSpec((rows, LANES), lambda i: (block0 + i, 0))],
        out_shape=_stream_shapes(total_rows, d)[1:],
        input_output_aliases=aliases,
        compiler_params=_params("arbitrary"),
        name="stream_entry",
    )(*args)


def _resident_rows(tm, k):
    return pl.BlockSpec((tm, k), lambda i, j: (i, 0), pipeline_mode=pl.Buffered(1))


def _scaled_weight(w_ref, g_ref, wb_ref):
    for sl in _lane_chunks(w_ref.shape[1]):
        wb_ref[:, sl] = (w_ref[:, sl] * g_ref[...]).astype(BF16)


def _row_scale(ssq_ref, d):
    return lax.rsqrt(ssq_ref[...] / d + EPS)


def _proj_kernel(xb_ref, ssq_ref, g_ref, w_ref, o_ref, wb_ref):
    _scaled_weight(w_ref, g_ref, wb_ref)
    r = _row_scale(ssq_ref, xb_ref.shape[1])
    acc = _dot(xb_ref[...], wb_ref[...])
    for sl in _lane_chunks(o_ref.shape[1]):
        o_ref[:, sl] = (acc[:, sl] * r).astype(o_ref.dtype)


def _proj(xb, ssq, g, w_stack, layer, *, tn=512):
    m, d = xb.shape
    n = w_stack.shape[2]
    return pl.pallas_call(
        _proj_kernel,
        grid=(m // WIDE_ROW_TILE, n // tn),
        in_specs=[_resident_rows(WIDE_ROW_TILE, d),
                  pl.BlockSpec((WIDE_ROW_TILE, LANES), lambda i, j: (i, 0)),
                  pl.BlockSpec((d, LANES), lambda i, j: (0, 0)),
                  pl.BlockSpec((None, d, tn), lambda i, j: (layer, 0, j))],
        out_specs=pl.BlockSpec((WIDE_ROW_TILE, tn), lambda i, j: (i, j)),
        out_shape=jax.ShapeDtypeStruct((m, n), BF16),
        scratch_shapes=[pltpu.VMEM((d, tn), BF16)],
        compiler_params=_params("parallel", "arbitrary"),
        name="mixer_in_proj",
    )(xb, ssq, _gain_columns(g), w_stack)


DOWN_COLS = 256


def _gate_up_kernel(xb_ref, ssq_ref, g_ref, wg_ref, wu_ref, wd_ref, h_ref, wdb_ref,
                    wgb_ref, wub_ref):
    _scaled_weight(wg_ref, g_ref, wgb_ref)
    _scaled_weight(wu_ref, g_ref, wub_ref)
    r = _row_scale(ssq_ref, xb_ref.shape[1])
    xb = xb_ref[...]
    gate = _dot(xb, wgb_ref[...])
    up = _dot(xb, wub_ref[...])
    for sl in _lane_chunks(h_ref.shape[1]):
        gt = gate[:, sl] * r
        h_ref[:, sl] = (gt * jax.nn.sigmoid(gt) * (up[:, sl] * r)).astype(BF16)
    @pl.when(pl.program_id(0) == 0)
    def _():
        for c in range(wdb_ref.shape[0]):
            wdb_ref[c] = wd_ref[:, c * DOWN_COLS:(c + 1) * DOWN_COLS].astype(BF16)


def _gate_up(xb, ssq, g, wg_stack, wu_stack, wd_stack, layer, *, tn=256):
    m, d = xb.shape
    f = wg_stack.shape[2]
    n_down = wd_stack.shape[2]

    def wspec():
        return pl.BlockSpec((None, d, tn), lambda i, j: (layer, 0, j))

    def down_rows(i, j):
        return jnp.where(i == 0, j, f // tn - 1)

    return pl.pallas_call(
        _gate_up_kernel,
        grid=(m // WIDE_ROW_TILE, f // tn),
        in_specs=[_resident_rows(WIDE_ROW_TILE, d),
                  pl.BlockSpec((WIDE_ROW_TILE, LANES), lambda i, j: (i, 0)),
                  pl.BlockSpec((d, LANES), lambda i, j: (0, 0)),
                  wspec(), wspec(),
                  pl.BlockSpec((None, tn, n_down),
                               lambda i, j: (layer, down_rows(i, j), 0))],
        out_specs=[pl.BlockSpec((WIDE_ROW_TILE, tn), lambda i, j: (i, j)),
                   pl.BlockSpec((n_down // DOWN_COLS, tn, DOWN_COLS),
                                lambda i, j: (0, down_rows(i, j), 0))],
        out_shape=[jax.ShapeDtypeStruct((m, f), BF16),
                   jax.ShapeDtypeStruct((n_down // DOWN_COLS, f, DOWN_COLS), BF16)],
        scratch_shapes=[pltpu.VMEM((d, tn), BF16), pltpu.VMEM((d, tn), BF16)],
        compiler_params=_params("arbitrary", "arbitrary"),
        name="ffn_gate_up",
    )(xb, ssq, _gain_columns(g), wg_stack, wu_stack, wd_stack)


def _matmul_res_kernel(a_ref, w_ref, r_ref, o_ref, ob_ref, ssq_ref):
    o = r_ref[...] + _dot(a_ref[...], w_ref[...])
    o_ref[...] = o
    ob_ref[...] = o.astype(BF16)
    part = jnp.broadcast_to(jnp.sum(o * o, axis=-1, keepdims=True), ssq_ref.shape)

    @pl.when(pl.program_id(1) == 0)
    def _():
        ssq_ref[...] = part

    @pl.when(pl.program_id(1) > 0)
    def _():
        ssq_ref[...] += part


def _matmul_res(a, w_tiles, r, *, tm, name):
    m, k = a.shape
    ntiles, _, tn = w_tiles.shape
    n = ntiles * tn
    return pl.pallas_call(
        _matmul_res_kernel,
        grid=(m // tm, ntiles),
        in_specs=[pl.BlockSpec((tm, k), lambda i, j: (i, 0)),
                  pl.BlockSpec((None, k, tn), lambda i, j: (j, 0, 0)),
                  pl.BlockSpec((tm, tn), lambda i, j: (i, j))],
        out_specs=[pl.BlockSpec((tm, tn), lambda i, j: (i, j)),
                   pl.BlockSpec((tm, tn), lambda i, j: (i, j)),
                   pl.BlockSpec((tm, LANES), lambda i, j: (i, 0))],
        out_shape=_stream_shapes(m, n),
        compiler_params=_params("parallel", "arbitrary"),
        name=name,
    )(a, w_tiles, r)


def _cast_kernel(w_ref, o_ref):
    o_ref[...] = w_ref[...].astype(o_ref.dtype)


def _cast_bf16(w_stack, layer, *, rows=512):
    k, n = w_stack.shape[1:]
    return pl.pallas_call(
        _cast_kernel,
        grid=(k // rows,),
        in_specs=[pl.BlockSpec((None, rows, n), lambda i: (layer, i, 0))],
        out_specs=pl.BlockSpec((rows, n), lambda i: (i, 0)),
        out_shape=jax.ShapeDtypeStruct((k, n), BF16),
        compiler_params=_params("parallel"),
        name="out_proj_weight_bf16",
    )(w_stack)


def _final_norm_kernel(x_ref, g_ref, o_ref):
    o_ref[...] = _rmsnorm(x_ref[...], g_ref[...])


def _final_norm(x, g, *, rows, block0, nblocks):
    d = x.shape[1]
    return pl.pallas_call(
        _final_norm_kernel,
        grid=(nblocks,),
        in_specs=[pl.BlockSpec((rows, d), lambda i: (block0 + i, 0)),
                  pl.BlockSpec((1, d), lambda i: (0, 0))],
        out_specs=pl.BlockSpec((rows, d), lambda i: (i, 0)),
        out_shape=jax.ShapeDtypeStruct((rows * nblocks, d), F32),
        compiler_params=_params("parallel"),
        name="final_norm",
    )(x, g.reshape(1, d))


PROJ_CHUNK = 256


def _pipelined_tail(mixer_stages, cat_a, cat_b, wob_ref, xres_ref, stream_outs, *,
                    interleave):
    s = pl.program_id(0)
    x_out, xb_out, ssq_out = stream_outs
    nchunks = wob_ref.shape[1] // PROJ_CHUNK

    @pl.when(s == 0)
    def _():
        cat_b[...] = jnp.zeros_like(cat_b)

    def step(cat_w, cat_r):
        stages = mixer_stages(cat_w)
        if not interleave:
            for stage in stages:
                stage()
            o = xres_ref[...] + _dot(cat_r[...], wob_ref[...])
            _emit_stream(o, *stream_outs)
            return
        ssq = jnp.zeros((x_out.shape[0], 1), F32)
        done = 0
        for k, stage in enumerate(stages):
            upto = (k + 1) * nchunks // len(stages)
            for c in range(done, upto):
                sl = slice(c * PROJ_CHUNK, (c + 1) * PROJ_CHUNK)
                o = xres_ref[:, sl] + _dot(cat_r[...], wob_ref[:, sl])
                x_out[:, sl] = o
                xb_out[:, sl] = o.astype(BF16)
                ssq = ssq + jnp.sum(o * o, axis=-1, keepdims=True)
            done = upto
            stage()
        ssq_out[...] = jnp.broadcast_to(ssq, ssq_out.shape)

    @pl.when(s % 2 == 0)
    def _():
        step(cat_a, cat_b)

    @pl.when(s % 2 == 1)
    def _():
        step(cat_b, cat_a)


def _tail_specs(ntiles, tt, d):
    def mixed(s):
        return jnp.minimum(s, ntiles - 1)

    def projected(s):
        return jnp.maximum(s - 1, 0)

    wob = pl.BlockSpec((d, d), lambda s: (0, 0), pipeline_mode=pl.Buffered(1))
    xres = pl.BlockSpec((tt, d), lambda s: (projected(s), 0))
    outs = [pl.BlockSpec((tt, d), lambda s: (projected(s), 0)),
            pl.BlockSpec((tt, d), lambda s: (projected(s), 0)),
            pl.BlockSpec((tt, LANES), lambda s: (projected(s), 0))]
    return mixed, wob, xres, outs


def _pool_project(pooled, w_pool_ref, scale_ref, cat_ref, col0):
    gw = w_pool_ref.shape[1]
    for g in range(len(POOL_WINDOWS)):
        sl = slice(g * gw, (g + 1) * gw)
        yb = _dot(pooled[:, sl].astype(BF16), w_pool_ref[g].astype(BF16))
        cat_ref[:, col0 + g * gw:col0 + (g + 1) * gw] = (
            yb * scale_ref[:, sl]).astype(cat_ref.dtype)


def _even_prompt_kernel(nt, ntiles,
                        xin_ref, gpre_ref, gpost_ref, p_ref,
                        hxin_ref, hgpre_ref, hp_ref,
                        wconv_ref, wpool_ref, scale_ref, wob_ref, xres_ref,
                        x_out, xb_out, ssq_out, conv_state_ref, pool_state_ref,
                        gx_ext, p_ext, cat_a, cat_b):
    tt, da = xin_ref.shape
    h = POOL_HALO
    i = jnp.minimum(pl.program_id(0), ntiles - 1) % nt
    live = (i > 0).astype(F32)

    def mixer_stages(cat_ref):
        def short_conv():
            gx = gpre_ref[...].astype(F32) * xin_ref[...].astype(F32)
            gx_ext[0:h, :] = (hgpre_ref[...].astype(F32) * hxin_ref[...].astype(F32)
                              * live)
            gx_ext[h:h + tt, :] = gx
            conv = (gx_ext[pl.ds(h - 2, tt), :] * wconv_ref[0:1, :]
                    + gx_ext[pl.ds(h - 1, tt), :] * wconv_ref[1:2, :]
                    + gx * wconv_ref[2:3, :])
            cat_ref[:, 0:da] = (gpost_ref[...].astype(F32) * conv).astype(cat_ref.dtype)
            conv_state_ref[...] = gx_ext[pl.ds(h + tt - 2, 2), :]
            p_ext[0:h, :] = hp_ref[...].astype(F32) * live
            p_ext[h:h + tt, :] = p_ref[...].astype(F32)
            nh = pool_state_ref.shape[0]
            pool_state_ref[...] = p_ext[pl.ds(h + tt - nh, nh), :]

        def pool_group(g, w):
            def run():
                gw = da // len(POOL_WINDOWS)
                sl = slice(g * gw, (g + 1) * gw)
                pg = p_ext[pl.ds(h, tt), sl]
                acc = pg
                for k in range(1, w):
                    acc = acc + p_ext[pl.ds(h - k, tt), sl]
                pos = i * tt + lax.broadcasted_iota(jnp.int32, (tt, 1), 0)
                cnt = jnp.minimum(w, pos + 1).astype(F32)
                pooled = acc / cnt - pg
                yb = _dot(pooled.astype(BF16), wpool_ref[g].astype(BF16))
                cat_ref[:, da + g * gw:da + (g + 1) * gw] = (
                    yb * scale_ref[:, sl]).astype(cat_ref.dtype)
            return run

        return [short_conv] + [pool_group(g, w) for g, w in enumerate(POOL_WINDOWS)]

    _pipelined_tail(mixer_stages, cat_a, cat_b, wob_ref, xres_ref,
                    (x_out, xb_out, ssq_out), interleave=True)


def _even_prompt(proj, x, w_conv, w_pool, pool_scale, wob, layer, *, batch, seq,
                 total_rows):
    da = w_conv.shape[2]
    d = x.shape[1]
    tt = EVEN_MIX_TILE
    nt = seq // tt
    ntiles = batch * nt
    hb = tt // POOL_HALO
    nwin = len(POOL_WINDOWS)
    nh = POOL_WINDOWS[-1] - 1
    mixed, wob_spec, xres_spec, stream_specs = _tail_specs(ntiles, tt, d)

    def main(c):
        return pl.BlockSpec((tt, da), lambda s: (mixed(s), c))

    def halo(c):
        return pl.BlockSpec((POOL_HALO, da),
                            lambda s: (jnp.maximum(mixed(s) * hb - 1, 0), c))

    return pl.pallas_call(
        functools.partial(_even_prompt_kernel, nt, ntiles),
        grid=(ntiles + 1,),
        in_specs=[main(0), main(1), main(2), main(3), halo(0), halo(1), halo(3),
                  pl.BlockSpec((None, 3, da), lambda s: (layer, 0, 0)),
                  pl.BlockSpec((None, nwin, da // nwin, da // nwin),
                               lambda s: (layer, 0, 0, 0)),
                  pl.BlockSpec((None, 1, da), lambda s: (layer, 0, 0)),
                  wob_spec, xres_spec],
        out_specs=stream_specs + [
            pl.BlockSpec((None, 2, da), lambda s: (mixed(s) // nt, 0, 0)),
            pl.BlockSpec((None, nh, da), lambda s: (mixed(s) // nt, 0, 0))],
        out_shape=_stream_shapes(total_rows, d) + [
            jax.ShapeDtypeStruct((batch, 2, da), F32),
            jax.ShapeDtypeStruct((batch, nh, da), F32)],
        scratch_shapes=[pltpu.VMEM((POOL_HALO + tt, da), F32),
                        pltpu.VMEM((POOL_HALO + tt, da), F32),
                        pltpu.VMEM((tt, d), BF16),
                        pltpu.VMEM((tt, d), BF16)],
        compiler_params=_params("arbitrary"),
        name="even_mixer_prompt",
    )(proj, proj, proj, proj, proj, proj, proj,
      w_conv, w_pool, pool_scale.reshape(pool_scale.shape[0], 1, da), wob, x)


def _odd_prompt_kernel(nt, ntiles,
                       u_ref, v_ref, a_ref, g_ref, ha_ref, hg_ref,
                       vng_ref, vnb_ref, ws_ref, bst_ref,
                       wconv_ref, bconv_ref, cng_ref, cnb_ref, wob_ref, xres_ref,
                       x_out, xb_out, ssq_out, chunk_v_ref, conf_state_ref,
                       glu_ext, shifted, cat_a, cat_b):
    tt, dc = u_ref.shape
    h = CONF_HALO
    i = jnp.minimum(pl.program_id(0), ntiles - 1) % nt
    live = (i > 0).astype(F32)

    nheads = ws_ref.shape[0]
    hd = dc // nheads
    nw = wconv_ref.shape[0]

    def mixer_stages(cat_ref):
        vals = {}

        def gate_inputs():
            vals["u"] = jax.nn.gelu(u_ref[...].astype(F32))
            vn = _layernorm(jax.nn.gelu(v_ref[...].astype(F32)), vng_ref[...],
                            vnb_ref[...])
            chunk_v_ref[...] = vn[tt - CHUNK:tt, :]
            vals["vb"] = vn.astype(BF16)

        def gate_heads(heads):
            def run():
                row = lax.broadcasted_iota(jnp.int32, (CHUNK, CHUNK), 0)
                colm = lax.broadcasted_iota(jnp.int32, (CHUNK, CHUNK), 1)
                for hh in heads:
                    ws = jnp.where(colm <= row, ws_ref[hh], 0.0).astype(BF16)
                    bias = bst_ref[:, hh:hh + 1]
                    for c in range(tt // CHUNK):
                        rs = slice(c * CHUNK, (c + 1) * CHUNK)
                        cs = slice(hh * hd, (hh + 1) * hd)
                        mixed = _dot(ws, vals["vb"][rs, cs]) + bias
                        cat_ref[rs, cs] = (vals["u"][rs, cs] * mixed).astype(cat_ref.dtype)
            return run

        def glu_rows():
            glu = a_ref[...].astype(F32) * jax.nn.sigmoid(g_ref[...].astype(F32))
            glu_ext[0:h, :] = (ha_ref[...].astype(F32)
                               * jax.nn.sigmoid(hg_ref[...].astype(F32)) * live)
            glu_ext[h:h + tt, :] = glu
            conf_state_ref[...] = glu_ext[pl.ds(h + tt - (nw - 1), nw - 1), :]
            span = shifted.shape[1]
            for sh in range(1, SUBLANES):
                shifted[sh - 1] = glu_ext[pl.ds(sh, span), :]
            vals["acc"] = glu * wconv_ref[nw - 1:nw, :]

        def taps(ks):
            def run():
                acc = vals["acc"]
                for k in ks:
                    q, sh = divmod(h - (nw - 1) + k, SUBLANES)
                    if sh == 0:
                        tap = glu_ext[pl.ds(q * SUBLANES, tt), :]
                    else:
                        tap = shifted[sh - 1, pl.ds(q * SUBLANES, tt), :]
                    acc = acc + tap * wconv_ref[k:k + 1, :]
                vals["acc"] = acc
            return run

        def conv_norm():
            y = _layernorm(vals["acc"] + bconv_ref[...], cng_ref[...], cnb_ref[...])
            cat_ref[:, dc:2 * dc] = (y * jax.nn.sigmoid(y)).astype(cat_ref.dtype)

        return [gate_inputs, gate_heads(range(nheads)), glu_rows, taps(range(nw - 1)),
                conv_norm]

    _pipelined_tail(mixer_stages, cat_a, cat_b, wob_ref, xres_ref,
                    (x_out, xb_out, ssq_out), interleave=False)


def _odd_prompt(proj, x, v_norm_g, v_norm_b, w_spatial, b_spatial_t, w_conv, b_conv,
                conf_norm_g, conf_norm_b, wob, layer, *, batch, seq, total_rows):
    nw, dc = w_conv.shape[1:]
    nheads = w_spatial.shape[1]
    d = x.shape[1]
    tt = MIX_TILE
    nt = seq // tt
    ntiles = batch * nt
    hb = tt // CONF_HALO
    mixed, wob_spec, xres_spec, stream_specs = _tail_specs(ntiles, tt, d)

    def main(c):
        return pl.BlockSpec((tt, dc), lambda s: (mixed(s), c))

    def halo(c):
        return pl.BlockSpec((CONF_HALO, dc),
                            lambda s: (jnp.maximum(mixed(s) * hb - 1, 0), c))

    def vec():
        return pl.BlockSpec((None, 1, dc), lambda s: (layer, 0, 0))

    def as_rows(v):
        return v.reshape(v.shape[0], 1, dc)

    return pl.pallas_call(
        functools.partial(_odd_prompt_kernel, nt, ntiles),
        grid=(ntiles + 1,),
        in_specs=[main(0), main(1), main(2), main(3), halo(2), halo(3),
                  vec(), vec(),
                  pl.BlockSpec((None, nheads, CHUNK, CHUNK), lambda s: (layer, 0, 0, 0)),
                  pl.BlockSpec((None, CHUNK, nheads), lambda s: (layer, 0, 0)),
                  pl.BlockSpec((None, nw, dc), lambda s: (layer, 0, 0)),
                  vec(), vec(), vec(), wob_spec, xres_spec],
        out_specs=stream_specs + [
            pl.BlockSpec((None, CHUNK, dc), lambda s: (mixed(s) // nt, 0, 0)),
            pl.BlockSpec((None, nw - 1, dc), lambda s: (mixed(s) // nt, 0, 0))],
        out_shape=_stream_shapes(total_rows, d) + [
            jax.ShapeDtypeStruct((batch, CHUNK, dc), F32),
            jax.ShapeDtypeStruct((batch, nw - 1, dc), F32)],
        scratch_shapes=[pltpu.VMEM((CONF_HALO + tt, dc), F32),
                        pltpu.VMEM((SUBLANES - 1, CONF_HALO - SUBLANES + tt, dc), F32),
                        pltpu.VMEM((tt, d), BF16),
                        pltpu.VMEM((tt, d), BF16)],
        compiler_params=_params("arbitrary"),
        name="odd_mixer_prompt",
    )(proj, proj, proj, proj, proj, proj,
      as_rows(v_norm_g), as_rows(v_norm_b), w_spatial, b_spatial_t,
      w_conv, as_rows(b_conv), as_rows(conf_norm_g), as_rows(conf_norm_b), wob, x)


def _push_copies(hist_ref, new_any, layer, new_row, sems, slot):
    k = hist_ref.shape[1]
    shift = pltpu.make_async_copy(hist_ref.at[0, pl.ds(1, k - 1)],
                                  new_any.at[layer, pl.ds(0, k - 1)], sems.at[slot])
    last = pltpu.make_async_copy(new_row, new_any.at[layer, k - 1], sems.at[slot + 1])
    return shift, last


def _even_sample_kernel(layer, nprev,
                        xin_ref, gpre_ref, gpost_ref, p_ref,
                        conv_hist_ref, pool_hist_ref,
                        wconv_ref, wpool_ref, scale_ref, wob_ref, xres_ref,
                        x_in, xb_in, ssq_in, *refs):
    del x_in, xb_in, ssq_in
    x_out, xb_out, ssq_out, new_conv, new_pool, cat_ref, gx_row, p_row, sems = refs[nprev:]
    s, da = xin_ref.shape
    conv_shift, conv_last = _push_copies(conv_hist_ref, new_conv, layer, gx_row, sems, 0)
    pool_shift, pool_last = _push_copies(pool_hist_ref, new_pool, layer, p_row, sems, 2)
    conv_shift.start()
    pool_shift.start()

    gx = gpre_ref[...].astype(F32) * xin_ref[...].astype(F32)
    gx_row[...] = gx
    conv_last.start()
    nc = conv_hist_ref.shape[1]
    conv = gx * wconv_ref[nc:nc + 1, :]
    for k in range(nc):
        conv = conv + conv_hist_ref[0, k] * wconv_ref[k:k + 1, :]
    cat_ref[:, 0:da] = (gpost_ref[...].astype(F32) * conv).astype(cat_ref.dtype)

    p = p_ref[...].astype(F32)
    p_row[...] = p
    pool_last.start()
    nh = pool_hist_ref.shape[1]
    gw = da // len(POOL_WINDOWS)
    pooled = []
    for g, w in enumerate(POOL_WINDOWS):
        sl = slice(g * gw, (g + 1) * gw)
        acc = p[:, sl]
        for back in range(1, w):
            acc = acc + pool_hist_ref[0, nh - back, :, sl]
        cnt = float(min(w, PAST_LEN + 1))
        pooled.append(acc / cnt - p[:, sl])
    pooled = jnp.concatenate(pooled, axis=-1)
    _pool_project(pooled, wpool_ref, scale_ref, cat_ref, da)
    o = xres_ref[...] + _dot(cat_ref[...], wob_ref[...])
    _emit_stream(o, x_out, xb_out, ssq_out)
    conv_shift.wait()
    pool_shift.wait()
    conv_last.wait()
    pool_last.wait()


def _sample_stream_specs(s, d, xres_block, out_block):
    wob = pl.BlockSpec((d, d), lambda i: (0, 0), pipeline_mode=pl.Buffered(1))
    xres = pl.BlockSpec((s, d), lambda i: (xres_block, 0))
    anys = [pl.BlockSpec(memory_space=pl.ANY)] * 3
    outs = [pl.BlockSpec((s, d), lambda i: (out_block, 0)),
            pl.BlockSpec((s, d), lambda i: (out_block, 0)),
            pl.BlockSpec((s, LANES), lambda i: (out_block, 0))]
    return wob, xres, anys, outs


def _history_block(hist_t, layer):
    return pl.BlockSpec((1,) + hist_t.shape[1:], lambda i: (layer, 0, 0, 0),
                        pipeline_mode=pl.Buffered(1))


def _even_sample(proj, xres, xres_block, stream, conv_hist_t, pool_hist_t, new_states,
                 w_conv, w_pool, pool_scale, wob, layer, *, row0):
    nc, s, da = conv_hist_t.shape[1:]
    total_rows, d = stream[0].shape
    rb = row0 // s
    nwin = len(POOL_WINDOWS)
    wob_spec, xres_spec, any_specs, stream_specs = _sample_stream_specs(
        s, d, xres_block, rb)
    nprev = len(new_states)
    any_spec = pl.BlockSpec(memory_space=pl.ANY)

    def seg(c):
        return pl.BlockSpec((s, da), lambda i: (rb, c))

    in_specs = [seg(0), seg(1), seg(2), seg(3),
                _history_block(conv_hist_t, layer), _history_block(pool_hist_t, layer),
                pl.BlockSpec((None, nc + 1, da), lambda i: (layer, 0, 0)),
                pl.BlockSpec((None, nwin, da // nwin, da // nwin),
                             lambda i: (layer, 0, 0, 0)),
                pl.BlockSpec((None, 1, da), lambda i: (layer, 0, 0)),
                wob_spec, xres_spec] + any_specs + [any_spec] * nprev
    first_prev = len(in_specs) - nprev
    aliases = {11: 0, 12: 1, 13: 2}
    aliases.update({first_prev + k: 3 + k for k in range(nprev)})
    return pl.pallas_call(
        functools.partial(_even_sample_kernel, layer, nprev),
        grid=(1,),
        in_specs=in_specs,
        out_specs=stream_specs + [any_spec, any_spec],
        out_shape=_stream_shapes(total_rows, d) + [
            jax.ShapeDtypeStruct(conv_hist_t.shape, F32),
            jax.ShapeDtypeStruct(pool_hist_t.shape, F32)],
        scratch_shapes=[pltpu.VMEM((s, d), BF16), pltpu.VMEM((s, da), F32),
                        pltpu.VMEM((s, da), F32), pltpu.SemaphoreType.DMA((4,))],
        input_output_aliases=aliases,
        compiler_params=_params("arbitrary"),
        name="even_mixer_sample",
    )(proj, proj, proj, proj, conv_hist_t, pool_hist_t,
      w_conv, w_pool, pool_scale.reshape(pool_scale.shape[0], 1, da), wob, xres, *stream,
      *new_states)


def _odd_sample_kernel(layer, nprev,
                       u_ref, v_ref, a_ref, g_ref, hist_ref,
                       vng_ref, vnb_ref, ws0_ref, b0_ref,
                       wconv_ref, bconv_ref, cng_ref, cnb_ref, wob_ref, xres_ref,
                       x_in, xb_in, ssq_in, *refs):
    del x_in, xb_in, ssq_in
    x_out, xb_out, ssq_out, vn_ref, new_hist, cat_ref, glu_row, sems = refs[nprev:]
    s, dc = u_ref.shape
    shift, last = _push_copies(hist_ref, new_hist, layer, glu_row, sems, 0)
    shift.start()
    glu = a_ref[...].astype(F32) * jax.nn.sigmoid(g_ref[...].astype(F32))
    glu_row[...] = glu
    last.start()

    u = jax.nn.gelu(u_ref[...].astype(F32))
    vn = _layernorm(jax.nn.gelu(v_ref[...].astype(F32)), vng_ref[...], vnb_ref[...])
    vn_ref[...] = vn
    mixed = ws0_ref[...] * vn + b0_ref[...]
    cat_ref[:, 0:dc] = (u * mixed).astype(cat_ref.dtype)

    nh = hist_ref.shape[1]
    acc = glu * wconv_ref[nh:nh + 1, :]
    for k in range(nh):
        acc = acc + hist_ref[0, k] * wconv_ref[k:k + 1, :]
    y = _layernorm(acc + bconv_ref[...], cng_ref[...], cnb_ref[...])
    cat_ref[:, dc:2 * dc] = (y * jax.nn.sigmoid(y)).astype(cat_ref.dtype)
    o = xres_ref[...] + _dot(cat_ref[...], wob_ref[...])
    _emit_stream(o, x_out, xb_out, ssq_out)
    shift.wait()
    last.wait()


def _odd_sample(proj, xres, xres_block, stream, hist_t, new_states, v_norm_g, v_norm_b,
                w_spatial, b_spatial, w_conv, b_conv, conf_norm_g, conf_norm_b, wob, layer,
                *, row0):
    nh, s, dc = hist_t.shape[1:]
    nheads = w_spatial.shape[1]
    hd = dc // nheads
    total_rows, d = stream[0].shape
    rb = row0 // s
    ws0 = jnp.repeat(w_spatial[layer, :, 0, 0], hd).reshape(1, dc)
    b0 = jnp.repeat(b_spatial[layer, :, 0], hd).reshape(1, dc)
    wob_spec, xres_spec, any_specs, stream_specs = _sample_stream_specs(
        s, d, xres_block, rb)
    nprev = len(new_states)
    any_spec = pl.BlockSpec(memory_space=pl.ANY)

    def seg(c):
        return pl.BlockSpec((s, dc), lambda i: (rb, c))

    def vec():
        return pl.BlockSpec((None, 1, dc), lambda i: (layer, 0, 0))

    def row():
        return pl.BlockSpec((1, dc), lambda i: (0, 0))

    def as_rows(v):
        return v.reshape(v.shape[0], 1, dc)

    in_specs = [seg(0), seg(1), seg(2), seg(3), _history_block(hist_t, layer),
                vec(), vec(), row(), row(),
                pl.BlockSpec((None, nh + 1, dc), lambda i: (layer, 0, 0)),
                vec(), vec(), vec(), wob_spec, xres_spec] + any_specs + (
                    [any_spec] * nprev)
    first_prev = len(in_specs) - nprev
    aliases = {15: 0, 16: 1, 17: 2}
    aliases.update({first_prev + k: 4 + k for k in range(nprev)})
    return pl.pallas_call(
        functools.partial(_odd_sample_kernel, layer, nprev),
        grid=(1,),
        in_specs=in_specs,
        out_specs=stream_specs + [pl.BlockSpec((s, dc), lambda i: (0, 0)), any_spec],
        out_shape=_stream_shapes(total_rows, d) + [
            jax.ShapeDtypeStruct((s, dc), F32),
            jax.ShapeDtypeStruct(hist_t.shape, F32)],
        scratch_shapes=[pltpu.VMEM((s, d), BF16), pltpu.VMEM((s, dc), F32),
                        pltpu.SemaphoreType.DMA((2,))],
        input_output_aliases=aliases,
        compiler_params=_params("arbitrary"),
        name="odd_mixer_sample",
    )(proj, proj, proj, proj, hist_t,
      as_rows(v_norm_g), as_rows(v_norm_b), ws0, b0,
      w_conv, as_rows(b_conv), as_rows(conf_norm_g), as_rows(conf_norm_b), wob, xres,
      *stream, *new_states)


def kernel(x_prompt, x_sample, state_conv_a, state_pool, state_conformer, norm_mix, norm_ffn, w_in_even, w_conv_a, w_pool, pool_scale, w_out_even, w_in_odd, v_norm_g, v_norm_b, w_spatial, b_spatial, w_conv_d, b_conv_d, conf_norm_g, conf_norm_b, w_out_odd, w_ffn_gate, w_ffn_up, w_ffn_down, norm_final):
    batch, seq, d = x_prompt.shape
    ns = x_sample.shape[0]
    mp = batch * seq
    m = mp + ns
    depth = norm_mix.shape[0]
    da = w_conv_a.shape[2]
    assert x_sample.shape[1] == 1 and mp % ns == 0 and mp % PROMPT_ROWS == 0
    assert m % WIDE_ROW_TILE == 0 and m % DOWN_ROW_TILE == 0
    assert seq % MIX_TILE == 0 and MIX_TILE % CHUNK == 0 and seq % EVEN_MIX_TILE == 0

    x_prompt2d = x_prompt.reshape(mp, d)
    x_sample2d = x_sample.reshape(ns, d)
    entry = _prep(x_prompt2d, rows=PROMPT_ROWS, block0=0, total_rows=m)
    xb, ssq = _prep(x_sample2d, rows=ns, block0=mp // ns, total_rows=m, prev=entry)
    xres_p, xres_s, xres_s_block = x_prompt2d, x_sample2d, 0
    b_spatial_t = jnp.swapaxes(b_spatial, 1, 2)
    conv_hist_t = jnp.swapaxes(state_conv_a, 1, 2)
    pool_hist_t = jnp.swapaxes(state_pool, 1, 2)
    conf_hist_t = jnp.swapaxes(state_conformer, 1, 2)
    even_new, odd_new = [], []
    conv_p, pool_p, chunk_p, chunk_s, conf_p = [], [], [], [], []
    for l in range(depth):
        i = l // 2
        if l % 2 == 0:
            proj = _proj(xb, ssq, norm_mix[l], w_in_even, i)
            wob = _cast_bf16(w_out_even, i)
            *stream, cs_p, ps_p = _even_prompt(proj, xres_p, w_conv_a, w_pool, pool_scale,
                                               wob, i, batch=batch, seq=seq, total_rows=m)
            x, xb, ssq, *even_new = _even_sample(
                proj, xres_s, xres_s_block, stream, conv_hist_t, pool_hist_t, even_new,
                w_conv_a, w_pool, pool_scale, wob, i, row0=mp)
            conv_p.append(cs_p)
            pool_p.append(ps_p)
        else:
            proj = _proj(xb, ssq, norm_mix[l], w_in_odd, i)
            wob = _cast_bf16(w_out_odd, i)
            *stream, cv_p, cf_p = _odd_prompt(
                proj, xres_p, v_norm_g, v_norm_b, w_spatial, b_spatial_t, w_conv_d, b_conv_d,
                conf_norm_g, conf_norm_b, wob, i, batch=batch, seq=seq, total_rows=m)
            x, xb, ssq, vn_s, *odd_new = _odd_sample(
                proj, xres_s, xres_s_block, stream, conf_hist_t, odd_new, v_norm_g, v_norm_b,
                w_spatial, b_spatial, w_conv_d, b_conv_d, conf_norm_g, conf_norm_b, wob, i,
                row0=mp)
            chunk_p.append(cv_p)
            conf_p.append(cf_p)
            chunk_s.append(vn_s[:, None, :])
        h, wd_tiles = _gate_up(xb, ssq, norm_ffn[l], w_ffn_gate, w_ffn_up, w_ffn_down, l)
        x, xb, ssq = _matmul_res(h, wd_tiles, x, tm=DOWN_ROW_TILE, name="ffn_down")
        xres_p, xres_s, xres_s_block = x, x, mp // ns
    y_p = _final_norm(x, norm_final, rows=PROMPT_ROWS, block0=0, nblocks=mp // PROMPT_ROWS)
    y_s = _final_norm(x, norm_final, rows=ns, block0=mp // ns, nblocks=1)
    conv_s, pool_s = (jnp.swapaxes(t, 1, 2) for t in even_new)
    (conf_s,) = (jnp.swapaxes(t, 1, 2) for t in odd_new)
    return (y_p.reshape(batch, seq, d), y_s.reshape(ns, 1, d),
            jnp.stack(conv_p), conv_s, jnp.stack(pool_p), pool_s,
            jnp.stack(chunk_p), jnp.stack(chunk_s), jnp.stack(conf_p), conf_s)
```

```python
import functools

import jax
import jax.numpy as jnp
from jax import lax
from jax.experimental import pallas as pl
from jax.experimental.pallas import tpu as pltpu

F32 = jnp.float32
BF16 = jnp.bfloat16

EPS = 1e-6
PAST_LEN = 16384
POOL_WINDOWS = (2, 4, 8, 16)
CHUNK = 128

LANES = 128
SUBLANES = 8
VMEM_LIMIT_BYTES = 56 * 1024 * 1024

WIDE_ROW_TILE = 4160
DOWN_ROW_TILE = 1040
PROMPT_ROWS = 1024
MIX_TILE = 256
EVEN_MIX_TILE = 256
POOL_HALO = 16
CONF_HALO = 32


def _params(*sem):
    return pltpu.CompilerParams(dimension_semantics=sem,
                                vmem_limit_bytes=VMEM_LIMIT_BYTES)


def _rmsnorm(x, g):
    ms = jnp.mean(x * x, axis=-1, keepdims=True)
    return x * lax.rsqrt(ms + EPS) * g


def _layernorm(x, g, b):
    mu = jnp.mean(x, axis=-1, keepdims=True)
    xc = x - mu
    var = jnp.mean(xc * xc, axis=-1, keepdims=True)
    return xc * lax.rsqrt(var + EPS) * g + b


def _dot(a, b):
    return jnp.dot(a, b, preferred_element_type=F32)


def _lane_chunks(n):
    return [slice(c * LANES, (c + 1) * LANES) for c in range(n // LANES)]


def _gain_columns(g):
    return jnp.broadcast_to(g[:, None], (g.shape[0], LANES))


def _emit_stream(o, x_out, xb_out, ssq_out):
    x_out[...] = o
    xb_out[...] = o.astype(BF16)
    ssq_out[...] = jnp.broadcast_to(jnp.sum(o * o, axis=-1, keepdims=True),
                                    ssq_out.shape)


def _stream_shapes(total_rows, d):
    return [jax.ShapeDtypeStruct((total_rows, d), F32),
            jax.ShapeDtypeStruct((total_rows, d), BF16),
            jax.ShapeDtypeStruct((total_rows, LANES), F32)]


def _prep_kernel(x_ref, *refs):
    xb_out, ssq_out = refs[-2:]
    x = x_ref[...]
    xb_out[...] = x.astype(BF16)
    ssq_out[...] = jnp.broadcast_to(jnp.sum(x * x, axis=-1, keepdims=True),
                                    ssq_out.shape)


def _prep(x2d, *, rows, block0, total_rows, prev=None):
    n, d = x2d.shape
    in_specs = [pl.BlockSpec((rows, d), lambda i: (i, 0))]
    args = [x2d]
    aliases = {}
    if prev is not None:
        in_specs += [pl.BlockSpec(memory_space=pl.ANY)] * 2
        args += list(prev)
        aliases = {1: 0, 2: 1}
    return pl.pallas_call(
        _prep_kernel,
        grid=(n // rows,),
        in_specs=in_specs,
        out_specs=[pl.BlockSpec((rows, d), lambda i: (block0 + i, 0)),
                   pl.BlockSpec((rows, LANES), lambda i: (block0 + i, 0))],
        out_shape=_stream_shapes(total_rows, d)[1:],
        input_output_aliases=aliases,
        compiler_params=_params("arbitrary"),
        name="stream_entry",
    )(*args)


def _resident_rows(tm, k):
    return pl.BlockSpec((tm, k), lambda i, j: (i, 0), pipeline_mode=pl.Buffered(1))


def _scaled_weight(w_ref, g_ref, wb_ref):
    for sl in _lane_chunks(w_ref.shape[1]):
        wb_ref[:, sl] = (w_ref[:, sl] * g_ref[...]).astype(BF16)


def _row_scale(ssq_ref, d):
    return lax.rsqrt(ssq_ref[...] / d + EPS)


def _proj_kernel(xb_ref, ssq_ref, g_ref, w_ref, o_ref, wb_ref):
    _scaled_weight(w_ref, g_ref, wb_ref)
    r = _row_scale(ssq_ref, xb_ref.shape[1])
    acc = _dot(xb_ref[...], wb_ref[...])
    for sl in _lane_chunks(o_ref.shape[1]):
        o_ref[:, sl] = (acc[:, sl] * r).astype(o_ref.dtype)


def _proj(xb, ssq, g, w_stack, layer, *, tn=512):
    m, d = xb.shape
    n = w_stack.shape[2]
    return pl.pallas_call(
        _proj_kernel,
        grid=(m // WIDE_ROW_TILE, n // tn),
        in_specs=[_resident_rows(WIDE_ROW_TILE, d),
                  pl.BlockSpec((WIDE_ROW_TILE, LANES), lambda i, j: (i, 0)),
                  pl.BlockSpec((d, LANES), lambda i, j: (0, 0)),
                  pl.BlockSpec((None, d, tn), lambda i, j: (layer, 0, j))],
        out_specs=pl.BlockSpec((WIDE_ROW_TILE, tn), lambda i, j: (i, j)),
        out_shape=jax.ShapeDtypeStruct((m, n), BF16),
        scratch_shapes=[pltpu.VMEM((d, tn), BF16)],
        compiler_params=_params("parallel", "arbitrary"),
        name="mixer_in_proj",
    )(xb, ssq, _gain_columns(g), w_stack)


DOWN_COLS = 512


def _gate_up_kernel(xb_ref, ssq_ref, g_ref, wg_ref, wu_ref, wd_ref, h_ref, wdb_ref,
                    wgb_ref, wub_ref):
    _scaled_weight(wg_ref, g_ref, wgb_ref)
    _scaled_weight(wu_ref, g_ref, wub_ref)
    r = _row_scale(ssq_ref, xb_ref.shape[1])
    xb = xb_ref[...]
    gate = _dot(xb, wgb_ref[...])
    up = _dot(xb, wub_ref[...])
    for sl in _lane_chunks(h_ref.shape[1]):
        gt = gate[:, sl] * r
        h_ref[:, sl] = (gt * jax.nn.sigmoid(gt) * (up[:, sl] * r)).astype(BF16)
    @pl.when(pl.program_id(0) == 0)
    def _():
        for c in range(wdb_ref.shape[0]):
            wdb_ref[c] = wd_ref[:, c * DOWN_COLS:(c + 1) * DOWN_COLS].astype(BF16)


def _gate_up(xb, ssq, g, wg_stack, wu_stack, wd_stack, layer, *, tn=256):
    m, d = xb.shape
    f = wg_stack.shape[2]
    n_down = wd_stack.shape[2]

    def wspec():
        return pl.BlockSpec((None, d, tn), lambda i, j: (layer, 0, j))

    def down_rows(i, j):
        return jnp.where(i == 0, j, f // tn - 1)

    return pl.pallas_call(
        _gate_up_kernel,
        grid=(m // WIDE_ROW_TILE, f // tn),
        in_specs=[_resident_rows(WIDE_ROW_TILE, d),
                  pl.BlockSpec((WIDE_ROW_TILE, LANES), lambda i, j: (i, 0)),
                  pl.BlockSpec((d, LANES), lambda i, j: (0, 0)),
                  wspec(), wspec(),
                  pl.BlockSpec((None, tn, n_down),
                               lambda i, j: (layer, down_rows(i, j), 0))],
        out_specs=[pl.BlockSpec((WIDE_ROW_TILE, tn), lambda i, j: (i, j)),
                   pl.BlockSpec((n_down // DOWN_COLS, tn, DOWN_COLS),
                                lambda i, j: (0, down_rows(i, j), 0))],
        out_shape=[jax.ShapeDtypeStruct((m, f), BF16),
                   jax.ShapeDtypeStruct((n_down // DOWN_COLS, f, DOWN_COLS), BF16)],
        scratch_shapes=[pltpu.VMEM((d, tn), BF16), pltpu.VMEM((d, tn), BF16)],
        compiler_params=_params("arbitrary", "arbitrary"),
        name="ffn_gate_up",
    )(xb, ssq, _gain_columns(g), wg_stack, wu_stack, wd_stack)


def _matmul_res_kernel(a_ref, w_ref, r_ref, o_ref, ob_ref, ssq_ref):
    o = r_ref[...] + _dot(a_ref[...], w_ref[...])
    o_ref[...] = o
    ob_ref[...] = o.astype(BF16)
    part = jnp.broadcast_to(jnp.sum(o * o, axis=-1, keepdims=True), ssq_ref.shape)

    @pl.when(pl.program_id(1) == 0)
    def _():
        ssq_ref[...] = part

    @pl.when(pl.program_id(1) > 0)
    def _():
        ssq_ref[...] += part


def _matmul_res(a, w_tiles, r, *, tm, name):
    m, k = a.shape
    ntiles, _, tn = w_tiles.shape
    n = ntiles * tn
    return pl.pallas_call(
        _matmul_res_kernel,
        grid=(m // tm, ntiles),
        in_specs=[pl.BlockSpec((tm, k), lambda i, j: (i, 0)),
                  pl.BlockSpec((None, k, tn), lambda i, j: (j, 0, 0)),
                  pl.BlockSpec((tm, tn), lambda i, j: (i, j))],
        out_specs=[pl.BlockSpec((tm, tn), lambda i, j: (i, j)),
                   pl.BlockSpec((tm, tn), lambda i, j: (i, j)),
                   pl.BlockSpec((tm, LANES), lambda i, j: (i, 0))],
        out_shape=_stream_shapes(m, n),
        compiler_params=_params("parallel", "arbitrary"),
        name=name,
    )(a, w_tiles, r)


def _cast_kernel(w_ref, o_ref):
    o_ref[...] = w_ref[...].astype(o_ref.dtype)


def _cast_bf16(w_stack, layer, *, rows=512):
    k, n = w_stack.shape[1:]
    return pl.pallas_call(
        _cast_kernel,
        grid=(k // rows,),
        in_specs=[pl.BlockSpec((None, rows, n), lambda i: (layer, i, 0))],
        out_specs=pl.BlockSpec((rows, n), lambda i: (i, 0)),
        out_shape=jax.ShapeDtypeStruct((k, n), BF16),
        compiler_params=_params("parallel"),
        name="out_proj_weight_bf16",
    )(w_stack)


def _final_norm_kernel(x_ref, g_ref, o_ref):
    o_ref[...] = _rmsnorm(x_ref[...], g_ref[...])


def _final_norm(x, g, *, rows, block0, nblocks):
    d = x.shape[1]
    return pl.pallas_call(
        _final_norm_kernel,
        grid=(nblocks,),
        in_specs=[pl.BlockSpec((rows, d), lambda i: (block0 + i, 0)),
                  pl.BlockSpec((1, d), lambda i: (0, 0))],
        out_specs=pl.BlockSpec((rows, d), lambda i: (i, 0)),
        out_shape=jax.ShapeDtypeStruct((rows * nblocks, d), F32),
        compiler_params=_params("parallel"),
        name="final_norm",
    )(x, g.reshape(1, d))


PROJ_CHUNK = 256


def _pipelined_tail(mixer_stages, cat_a, cat_b, wob_ref, xres_ref, stream_outs, *,
                    interleave):
    s = pl.program_id(0)
    x_out, xb_out, ssq_out = stream_outs
    nchunks = wob_ref.shape[1] // PROJ_CHUNK

    @pl.when(s == 0)
    def _():
        cat_b[...] = jnp.zeros_like(cat_b)

    def step(cat_w, cat_r):
        stages = mixer_stages(cat_w)
        if not interleave:
            for stage in stages:
                stage()
            o = xres_ref[...] + _dot(cat_r[...], wob_ref[...])
            _emit_stream(o, *stream_outs)
            return
        ssq = jnp.zeros((x_out.shape[0], 1), F32)
        done = 0
        for k, stage in enumerate(stages):
            upto = (k + 1) * nchunks // len(stages)
            for c in range(done, upto):
                sl = slice(c * PROJ_CHUNK, (c + 1) * PROJ_CHUNK)
                o = xres_ref[:, sl] + _dot(cat_r[...], wob_ref[:, sl])
                x_out[:, sl] = o
                xb_out[:, sl] = o.astype(BF16)
                ssq = ssq + jnp.sum(o * o, axis=-1, keepdims=True)
            done = upto
            stage()
        ssq_out[...] = jnp.broadcast_to(ssq, ssq_out.shape)

    @pl.when(s % 2 == 0)
    def _():
        step(cat_a, cat_b)

    @pl.when(s % 2 == 1)
    def _():
        step(cat_b, cat_a)


def _tail_specs(ntiles, tt, d):
    def mixed(s):
        return jnp.minimum(s, ntiles - 1)

    def projected(s):
        return jnp.maximum(s - 1, 0)

    wob = pl.BlockSpec((d, d), lambda s: (0, 0), pipeline_mode=pl.Buffered(1))
    xres = pl.BlockSpec((tt, d), lambda s: (projected(s), 0))
    outs = [pl.BlockSpec((tt, d), lambda s: (projected(s), 0)),
            pl.BlockSpec((tt, d), lambda s: (projected(s), 0)),
            pl.BlockSpec((tt, LANES), lambda s: (projected(s), 0))]
    return mixed, wob, xres, outs


def _pool_project(pooled, w_pool_ref, scale_ref, cat_ref, col0):
    gw = w_pool_ref.shape[1]
    for g in range(len(POOL_WINDOWS)):
        sl = slice(g * gw, (g + 1) * gw)
        yb = _dot(pooled[:, sl].astype(BF16), w_pool_ref[g].astype(BF16))
        cat_ref[:, col0 + g * gw:col0 + (g + 1) * gw] = (
            yb * scale_ref[:, sl]).astype(cat_ref.dtype)


def _even_prompt_kernel(nt, ntiles,
                        xin_ref, gpre_ref, gpost_ref, p_ref,
                        hxin_ref, hgpre_ref, hp_ref,
                        wconv_ref, wpool_ref, scale_ref, wob_ref, xres_ref,
                        x_out, xb_out, ssq_out, conv_state_ref, pool_state_ref,
                        gx_ext, p_ext, cat_a, cat_b):
    tt, da = xin_ref.shape
    h = POOL_HALO
    i = jnp.minimum(pl.program_id(0), ntiles - 1) % nt
    live = (i > 0).astype(F32)

    def mixer_stages(cat_ref):
        def short_conv():
            gx = gpre_ref[...].astype(F32) * xin_ref[...].astype(F32)
            gx_ext[0:h, :] = (hgpre_ref[...].astype(F32) * hxin_ref[...].astype(F32)
                              * live)
            gx_ext[h:h + tt, :] = gx
            conv = (gx_ext[pl.ds(h - 2, tt), :] * wconv_ref[0:1, :]
                    + gx_ext[pl.ds(h - 1, tt), :] * wconv_ref[1:2, :]
                    + gx * wconv_ref[2:3, :])
            cat_ref[:, 0:da] = (gpost_ref[...].astype(F32) * conv).astype(cat_ref.dtype)
            conv_state_ref[...] = gx_ext[pl.ds(h + tt - 2, 2), :]
            p_ext[0:h, :] = hp_ref[...].astype(F32) * live
            p_ext[h:h + tt, :] = p_ref[...].astype(F32)
            nh = pool_state_ref.shape[0]
            pool_state_ref[...] = p_ext[pl.ds(h + tt - nh, nh), :]

        def pool_group(g, w):
            def run():
                gw = da // len(POOL_WINDOWS)
                sl = slice(g * gw, (g + 1) * gw)
                pg = p_ext[pl.ds(h, tt), sl]
                acc = pg
                for k in range(1, w):
                    acc = acc + p_ext[pl.ds(h - k, tt), sl]
                pos = i * tt + lax.broadcasted_iota(jnp.int32, (tt, 1), 0)
                cnt = jnp.minimum(w, pos + 1).astype(F32)
                pooled = acc / cnt - pg
                yb = _dot(pooled.astype(BF16), wpool_ref[g].astype(BF16))
                cat_ref[:, da + g * gw:da + (g + 1) * gw] = (
                    yb * scale_ref[:, sl]).astype(cat_ref.dtype)
            return run

        return [short_conv] + [pool_group(g, w) for g, w in enumerate(POOL_WINDOWS)]

    _pipelined_tail(mixer_stages, cat_a, cat_b, wob_ref, xres_ref,
                    (x_out, xb_out, ssq_out), interleave=True)


def _even_prompt(proj, x, w_conv, w_pool, pool_scale, wob, layer, *, batch, seq,
                 total_rows):
    da = w_conv.shape[2]
    d = x.shape[1]
    tt = EVEN_MIX_TILE
    nt = seq // tt
    ntiles = batch * nt
    hb = tt // POOL_HALO
    nwin = len(POOL_WINDOWS)
    nh = POOL_WINDOWS[-1] - 1
    mixed, wob_spec, xres_spec, stream_specs = _tail_specs(ntiles, tt, d)

    def main(c):
        return pl.BlockSpec((tt, da), lambda s: (mixed(s), c))

    def halo(c):
        return pl.BlockSpec((POOL_HALO, da),
                            lambda s: (jnp.maximum(mixed(s) * hb - 1, 0), c))

    return pl.pallas_call(
        functools.partial(_even_prompt_kernel, nt, ntiles),
        grid=(ntiles + 1,),
        in_specs=[main(0), main(1), main(2), main(3), halo(0), halo(1), halo(3),
                  pl.BlockSpec((None, 3, da), lambda s: (layer, 0, 0)),
                  pl.BlockSpec((None, nwin, da // nwin, da // nwin),
                               lambda s: (layer, 0, 0, 0)),
                  pl.BlockSpec((None, 1, da), lambda s: (layer, 0, 0)),
                  wob_spec, xres_spec],
        out_specs=stream_specs + [
            pl.BlockSpec((None, 2, da), lambda s: (mixed(s) // nt, 0, 0)),
            pl.BlockSpec((None, nh, da), lambda s: (mixed(s) // nt, 0, 0))],
        out_shape=_stream_shapes(total_rows, d) + [
            jax.ShapeDtypeStruct((batch, 2, da), F32),
            jax.ShapeDtypeStruct((batch, nh, da), F32)],
        scratch_shapes=[pltpu.VMEM((POOL_HALO + tt, da), F32),
                        pltpu.VMEM((POOL_HALO + tt, da), F32),
                        pltpu.VMEM((tt, d), BF16),
                        pltpu.VMEM((tt, d), BF16)],
        compiler_params=_params("arbitrary"),
        name="even_mixer_prompt",
    )(proj, proj, proj, proj, proj, proj, proj,
      w_conv, w_pool, pool_scale.reshape(pool_scale.shape[0], 1, da), wob, x)


def _odd_prompt_kernel(nt, ntiles,
                       u_ref, v_ref, a_ref, g_ref, ha_ref, hg_ref,
                       vng_ref, vnb_ref, ws_ref, bst_ref,
                       wconv_ref, bconv_ref, cng_ref, cnb_ref, wob_ref, xres_ref,
                       x_out, xb_out, ssq_out, chunk_v_ref, conf_state_ref,
                       glu_ext, shifted, cat_a, cat_b):
    tt, dc = u_ref.shape
    h = CONF_HALO
    i = jnp.minimum(pl.program_id(0), ntiles - 1) % nt
    live = (i > 0).astype(F32)

    nheads = ws_ref.shape[0]
    hd = dc // nheads
    nw = wconv_ref.shape[0]

    def mixer_stages(cat_ref):
        vals = {}

        def gate_inputs():
            vals["u"] = jax.nn.gelu(u_ref[...].astype(F32))
            vn = _layernorm(jax.nn.gelu(v_ref[...].astype(F32)), vng_ref[...],
                            vnb_ref[...])
            chunk_v_ref[...] = vn[tt - CHUNK:tt, :]
            vals["vb"] = vn.astype(BF16)

        def gate_heads(heads):
            def run():
                row = lax.broadcasted_iota(jnp.int32, (CHUNK, CHUNK), 0)
                colm = lax.broadcasted_iota(jnp.int32, (CHUNK, CHUNK), 1)
                for hh in heads:
                    ws = jnp.where(colm <= row, ws_ref[hh], 0.0).astype(BF16)
                    bias = bst_ref[:, hh:hh + 1]
                    for c in range(tt // CHUNK):
                        rs = slice(c * CHUNK, (c + 1) * CHUNK)
                        cs = slice(hh * hd, (hh + 1) * hd)
                        mixed = _dot(ws, vals["vb"][rs, cs]) + bias
                        cat_ref[rs, cs] = (vals["u"][rs, cs] * mixed).astype(cat_ref.dtype)
            return run

        def glu_rows():
            glu = a_ref[...].astype(F32) * jax.nn.sigmoid(g_ref[...].astype(F32))
            glu_ext[0:h, :] = (ha_ref[...].astype(F32)
                               * jax.nn.sigmoid(hg_ref[...].astype(F32)) * live)
            glu_ext[h:h + tt, :] = glu
            conf_state_ref[...] = glu_ext[pl.ds(h + tt - (nw - 1), nw - 1), :]
            span = shifted.shape[1]
            for sh in range(1, SUBLANES):
                shifted[sh - 1] = glu_ext[pl.ds(sh, span), :]
            vals["acc"] = glu * wconv_ref[nw - 1:nw, :]

        def taps(ks):
            def run():
                acc = vals["acc"]
                for k in ks:
                    q, sh = divmod(h - (nw - 1) + k, SUBLANES)
                    if sh == 0:
                        tap = glu_ext[pl.ds(q * SUBLANES, tt), :]
                    else:
                        tap = shifted[sh - 1, pl.ds(q * SUBLANES, tt), :]
                    acc = acc + tap * wconv_ref[k:k + 1, :]
                vals["acc"] = acc
            return run

        def conv_norm():
            y = _layernorm(vals["acc"] + bconv_ref[...], cng_ref[...], cnb_ref[...])
            cat_ref[:, dc:2 * dc] = (y * jax.nn.sigmoid(y)).astype(cat_ref.dtype)

        return [gate_inputs, gate_heads(range(nheads)), glu_rows, taps(range(nw - 1)),
                conv_norm]

    _pipelined_tail(mixer_stages, cat_a, cat_b, wob_ref, xres_ref,
                    (x_out, xb_out, ssq_out), interleave=False)


def _odd_prompt(proj, x, v_norm_g, v_norm_b, w_spatial, b_spatial_t, w_conv, b_conv,
                conf_norm_g, conf_norm_b, wob, layer, *, batch, seq, total_rows):
    nw, dc = w_conv.shape[1:]
    nheads = w_spatial.shape[1]
    d = x.shape[1]
    tt = MIX_TILE
    nt = seq // tt
    ntiles = batch * nt
    hb = tt // CONF_HALO
    mixed, wob_spec, xres_spec, stream_specs = _tail_specs(ntiles, tt, d)

    def main(c):
        return pl.BlockSpec((tt, dc), lambda s: (mixed(s), c))

    def halo(c):
        return pl.BlockSpec((CONF_HALO, dc),
                            lambda s: (jnp.maximum(mixed(s) * hb - 1, 0), c))

    def vec():
        return pl.BlockSpec((None, 1, dc), lambda s: (layer, 0, 0))

    def as_rows(v):
        return v.reshape(v.shape[0], 1, dc)

    return pl.pallas_call(
        functools.partial(_odd_prompt_kernel, nt, ntiles),
        grid=(ntiles + 1,),
        in_specs=[main(0), main(1), main(2), main(3), halo(2), halo(3),
                  vec(), vec(),
                  pl.BlockSpec((None, nheads, CHUNK, CHUNK), lambda s: (layer, 0, 0, 0)),
                  pl.BlockSpec((None, CHUNK, nheads), lambda s: (layer, 0, 0)),
                  pl.BlockSpec((None, nw, dc), lambda s: (layer, 0, 0)),
                  vec(), vec(), vec(), wob_spec, xres_spec],
        out_specs=stream_specs + [
            pl.BlockSpec((None, CHUNK, dc), lambda s: (mixed(s) // nt, 0, 0)),
            pl.BlockSpec((None, nw - 1, dc), lambda s: (mixed(s) // nt, 0, 0))],
        out_shape=_stream_shapes(total_rows, d) + [
            jax.ShapeDtypeStruct((batch, CHUNK, dc), F32),
            jax.ShapeDtypeStruct((batch, nw - 1, dc), F32)],
        scratch_shapes=[pltpu.VMEM((CONF_HALO + tt, dc), F32),
                        pltpu.VMEM((SUBLANES - 1, CONF_HALO - SUBLANES + tt, dc), F32),
                        pltpu.VMEM((tt, d), BF16),
                        pltpu.VMEM((tt, d), BF16)],
        compiler_params=_params("arbitrary"),
        name="odd_mixer_prompt",
    )(proj, proj, proj, proj, proj, proj,
      as_rows(v_norm_g), as_rows(v_norm_b), w_spatial, b_spatial_t,
      w_conv, as_rows(b_conv), as_rows(conf_norm_g), as_rows(conf_norm_b), wob, x)


def _push_copies(hist_ref, new_any, layer, new_row, sems, slot):
    k = hist_ref.shape[1]
    shift = pltpu.make_async_copy(hist_ref.at[0, pl.ds(1, k - 1)],
                                  new_any.at[layer, pl.ds(0, k - 1)], sems.at[slot])
    last = pltpu.make_async_copy(new_row, new_any.at[layer, k - 1], sems.at[slot + 1])
    return shift, last


def _even_sample_kernel(layer, nprev,
                        xin_ref, gpre_ref, gpost_ref, p_ref,
                        conv_hist_ref, pool_hist_ref,
                        wconv_ref, wpool_ref, scale_ref, wob_ref, xres_ref,
                        x_in, xb_in, ssq_in, *refs):
    del x_in, xb_in, ssq_in
    x_out, xb_out, ssq_out, new_conv, new_pool, cat_ref, gx_row, p_row, sems = refs[nprev:]
    s, da = xin_ref.shape
    conv_shift, conv_last = _push_copies(conv_hist_ref, new_conv, layer, gx_row, sems, 0)
    pool_shift, pool_last = _push_copies(pool_hist_ref, new_pool, layer, p_row, sems, 2)
    conv_shift.start()
    pool_shift.start()

    gx = gpre_ref[...].astype(F32) * xin_ref[...].astype(F32)
    gx_row[...] = gx
    conv_last.start()
    nc = conv_hist_ref.shape[1]
    conv = gx * wconv_ref[nc:nc + 1, :]
    for k in range(nc):
        conv = conv + conv_hist_ref[0, k] * wconv_ref[k:k + 1, :]
    cat_ref[:, 0:da] = (gpost_ref[...].astype(F32) * conv).astype(cat_ref.dtype)

    p = p_ref[...].astype(F32)
    p_row[...] = p
    pool_last.start()
    nh = pool_hist_ref.shape[1]
    gw = da // len(POOL_WINDOWS)
    pooled = []
    for g, w in enumerate(POOL_WINDOWS):
        sl = slice(g * gw, (g + 1) * gw)
        acc = p[:, sl]
        for back in range(1, w):
            acc = acc + pool_hist_ref[0, nh - back, :, sl]
        cnt = float(min(w, PAST_LEN + 1))
        pooled.append(acc / cnt - p[:, sl])
    pooled = jnp.concatenate(pooled, axis=-1)
    _pool_project(pooled, wpool_ref, scale_ref, cat_ref, da)
    o = xres_ref[...] + _dot(cat_ref[...], wob_ref[...])
    _emit_stream(o, x_out, xb_out, ssq_out)
    conv_shift.wait()
    pool_shift.wait()
    conv_last.wait()
    pool_last.wait()


def _sample_stream_specs(s, d, xres_block, out_block):
    wob = pl.BlockSpec((d, d), lambda i: (0, 0), pipeline_mode=pl.Buffered(1))
    xres = pl.BlockSpec((s, d), lambda i: (xres_block, 0))
    anys = [pl.BlockSpec(memory_space=pl.ANY)] * 3
    outs = [pl.BlockSpec((s, d), lambda i: (out_block, 0)),
            pl.BlockSpec((s, d), lambda i: (out_block, 0)),
            pl.BlockSpec((s, LANES), lambda i: (out_block, 0))]
    return wob, xres, anys, outs


def _history_block(hist_t, layer):
    return pl.BlockSpec((1,) + hist_t.shape[1:], lambda i: (layer, 0, 0, 0),
                        pipeline_mode=pl.Buffered(1))


def _even_sample(proj, xres, xres_block, stream, conv_hist_t, pool_hist_t, new_states,
                 w_conv, w_pool, pool_scale, wob, layer, *, row0):
    nc, s, da = conv_hist_t.shape[1:]
    total_rows, d = stream[0].shape
    rb = row0 // s
    nwin = len(POOL_WINDOWS)
    wob_spec, xres_spec, any_specs, stream_specs = _sample_stream_specs(
        s, d, xres_block, rb)
    nprev = len(new_states)
    any_spec = pl.BlockSpec(memory_space=pl.ANY)

    def seg(c):
        return pl.BlockSpec((s, da), lambda i: (rb, c))

    in_specs = [seg(0), seg(1), seg(2), seg(3),
                _history_block(conv_hist_t, layer), _history_block(pool_hist_t, layer),
                pl.BlockSpec((None, nc + 1, da), lambda i: (layer, 0, 0)),
                pl.BlockSpec((None, nwin, da // nwin, da // nwin),
                             lambda i: (layer, 0, 0, 0)),
                pl.BlockSpec((None, 1, da), lambda i: (layer, 0, 0)),
                wob_spec, xres_spec] + any_specs + [any_spec] * nprev
    first_prev = len(in_specs) - nprev
    aliases = {11: 0, 12: 1, 13: 2}
    aliases.update({first_prev + k: 3 + k for k in range(nprev)})
    return pl.pallas_call(
        functools.partial(_even_sample_kernel, layer, nprev),
        grid=(1,),
        in_specs=in_specs,
        out_specs=stream_specs + [any_spec, any_spec],
        out_shape=_stream_shapes(total_rows, d) + [
            jax.ShapeDtypeStruct(conv_hist_t.shape, F32),
            jax.ShapeDtypeStruct(pool_hist_t.shape, F32)],
        scratch_shapes=[pltpu.VMEM((s, d), BF16), pltpu.VMEM((s, da), F32),
                        pltpu.VMEM((s, da), F32), pltpu.SemaphoreType.DMA((4,))],
        input_output_aliases=aliases,
        compiler_params=_params("arbitrary"),
        name="even_mixer_sample",
    )(proj, proj, proj, proj, conv_hist_t, pool_hist_t,
      w_conv, w_pool, pool_scale.reshape(pool_scale.shape[0], 1, da), wob, xres, *stream,
      *new_states)


def _odd_sample_kernel(layer, nprev,
                       u_ref, v_ref, a_ref, g_ref, hist_ref,
                       vng_ref, vnb_ref, ws0_ref, b0_ref,
                       wconv_ref, bconv_ref, cng_ref, cnb_ref, wob_ref, xres_ref,
                       x_in, xb_in, ssq_in, *refs):
    del x_in, xb_in, ssq_in
    x_out, xb_out, ssq_out, vn_ref, new_hist, cat_ref, glu_row, sems = refs[nprev:]
    s, dc = u_ref.shape
    shift, last = _push_copies(hist_ref, new_hist, layer, glu_row, sems, 0)
    shift.start()
    glu = a_ref[...].astype(F32) * jax.nn.sigmoid(g_ref[...].astype(F32))
    glu_row[...] = glu
    last.start()

    u = jax.nn.gelu(u_ref[...].astype(F32))
    vn = _layernorm(jax.nn.gelu(v_ref[...].astype(F32)), vng_ref[...], vnb_ref[...])
    vn_ref[...] = vn
    mixed = ws0_ref[...] * vn + b0_ref[...]
    cat_ref[:, 0:dc] = (u * mixed).astype(cat_ref.dtype)

    nh = hist_ref.shape[1]
    acc = glu * wconv_ref[nh:nh + 1, :]
    for k in range(nh):
        acc = acc + hist_ref[0, k] * wconv_ref[k:k + 1, :]
    y = _layernorm(acc + bconv_ref[...], cng_ref[...], cnb_ref[...])
    cat_ref[:, dc:2 * dc] = (y * jax.nn.sigmoid(y)).astype(cat_ref.dtype)
    o = xres_ref[...] + _dot(cat_ref[...], wob_ref[...])
    _emit_stream(o, x_out, xb_out, ssq_out)
    shift.wait()
    last.wait()


def _odd_sample(proj, xres, xres_block, stream, hist_t, new_states, v_norm_g, v_norm_b,
                w_spatial, b_spatial, w_conv, b_conv, conf_norm_g, conf_norm_b, wob, layer,
                *, row0):
    nh, s, dc = hist_t.shape[1:]
    nheads = w_spatial.shape[1]
    hd = dc // nheads
    total_rows, d = stream[0].shape
    rb = row0 // s
    ws0 = jnp.repeat(w_spatial[layer, :, 0, 0], hd).reshape(1, dc)
    b0 = jnp.repeat(b_spatial[layer, :, 0], hd).reshape(1, dc)
    wob_spec, xres_spec, any_specs, stream_specs = _sample_stream_specs(
        s, d, xres_block, rb)
    nprev = len(new_states)
    any_spec = pl.BlockSpec(memory_space=pl.ANY)

    def seg(c):
        return pl.BlockSpec((s, dc), lambda i: (rb, c))

    def vec():
        return pl.BlockSpec((None, 1, dc), lambda i: (layer, 0, 0))

    def row():
        return pl.BlockSpec((1, dc), lambda i: (0, 0))

    def as_rows(v):
        return v.reshape(v.shape[0], 1, dc)

    in_specs = [seg(0), seg(1), seg(2), seg(3), _history_block(hist_t, layer),
                vec(), vec(), row(), row(),
                pl.BlockSpec((None, nh + 1, dc), lambda i: (layer, 0, 0)),
                vec(), vec(), vec(), wob_spec, xres_spec] + any_specs + (
                    [any_spec] * nprev)
    first_prev = len(in_specs) - nprev
    aliases = {15: 0, 16: 1, 17: 2}
    aliases.update({first_prev + k: 4 + k for k in range(nprev)})
    return pl.pallas_call(
        functools.partial(_odd_sample_kernel, layer, nprev),
        grid=(1,),
        in_specs=in_specs,
        out_specs=stream_specs + [pl.BlockSpec((s, dc), lambda i: (0, 0)), any_spec],
        out_shape=_stream_shapes(total_rows, d) + [
            jax.ShapeDtypeStruct((s, dc), F32),
            jax.ShapeDtypeStruct(hist_t.shape, F32)],
        scratch_shapes=[pltpu.VMEM((s, d), BF16), pltpu.VMEM((s, dc), F32),
                        pltpu.SemaphoreType.DMA((2,))],
        input_output_aliases=aliases,
        compiler_params=_params("arbitrary"),
        name="odd_mixer_sample",
    )(proj, proj, proj, proj, hist_t,
      as_rows(v_norm_g), as_rows(v_norm_b), ws0, b0,
      w_conv, as_rows(b_conv), as_rows(conf_norm_g), as_rows(conf_norm_b), wob, xres,
      *stream, *new_states)


def kernel(x_prompt, x_sample, state_conv_a, state_pool, state_conformer, norm_mix, norm_ffn, w_in_even, w_conv_a, w_pool, pool_scale, w_out_even, w_in_odd, v_norm_g, v_norm_b, w_spatial, b_spatial, w_conv_d, b_conv_d, conf_norm_g, conf_norm_b, w_out_odd, w_ffn_gate, w_ffn_up, w_ffn_down, norm_final):
    batch, seq, d = x_prompt.shape
    ns = x_sample.shape[0]
    mp = batch * seq
    m = mp + ns
    depth = norm_mix.shape[0]
    da = w_conv_a.shape[2]
    assert x_sample.shape[1] == 1 and mp % ns == 0 and mp % PROMPT_ROWS == 0
    assert m % WIDE_ROW_TILE == 0 and m % DOWN_ROW_TILE == 0
    assert seq % MIX_TILE == 0 and MIX_TILE % CHUNK == 0 and seq % EVEN_MIX_TILE == 0

    x_prompt2d = x_prompt.reshape(mp, d)
    x_sample2d = x_sample.reshape(ns, d)
    entry = _prep(x_prompt2d, rows=PROMPT_ROWS, block0=0, total_rows=m)
    xb, ssq = _prep(x_sample2d, rows=ns, block0=mp // ns, total_rows=m, prev=entry)
    xres_p, xres_s, xres_s_block = x_prompt2d, x_sample2d, 0
    b_spatial_t = jnp.swapaxes(b_spatial, 1, 2)
    conv_hist_t = jnp.swapaxes(state_conv_a, 1, 2)
    pool_hist_t = jnp.swapaxes(state_pool, 1, 2)
    conf_hist_t = jnp.swapaxes(state_conformer, 1, 2)
    even_new, odd_new = [], []
    conv_p, pool_p, chunk_p, chunk_s, conf_p = [], [], [], [], []
    for l in range(depth):
        i = l // 2
        if l % 2 == 0:
            proj = _proj(xb, ssq, norm_mix[l], w_in_even, i)
            wob = _cast_bf16(w_out_even, i)
            *stream, cs_p, ps_p = _even_prompt(proj, xres_p, w_conv_a, w_pool, pool_scale,
                                               wob, i, batch=batch, seq=seq, total_rows=m)
            x, xb, ssq, *even_new = _even_sample(
                proj, xres_s, xres_s_block, stream, conv_hist_t, pool_hist_t, even_new,
                w_conv_a, w_pool, pool_scale, wob, i, row0=mp)
            conv_p.append(cs_p)
            pool_p.append(ps_p)
        else:
            proj = _proj(xb, ssq, norm_mix[l], w_in_odd, i)
            wob = _cast_bf16(w_out_odd, i)
            *stream, cv_p, cf_p = _odd_prompt(
                proj, xres_p, v_norm_g, v_norm_b, w_spatial, b_spatial_t, w_conv_d, b_conv_d,
                conf_norm_g, conf_norm_b, wob, i, batch=batch, seq=seq, total_rows=m)
            x, xb, ssq, vn_s, *odd_new = _odd_sample(
                proj, xres_s, xres_s_block, stream, conf_hist_t, odd_new, v_norm_g, v_norm_b,
                w_spatial, b_spatial, w_conv_d, b_conv_d, conf_norm_g, conf_norm_b, wob, i,
                row0=mp)
            chunk_p.append(cv_p)
            conf_p.append(cf_p)
            chunk_s.append(vn_s[:, None, :])
        h, wd_tiles = _gate_up(xb, ssq, norm_ffn[l], w_ffn_gate, w_ffn_up, w_ffn_down, l)
        x, xb, ssq = _matmul_res(h, wd_tiles, x, tm=DOWN_ROW_TILE, name="ffn_down")
        xres_p, xres_s, xres_s_block = x, x, mp // ns
    y_p = _final_norm(x, norm_final, rows=PROMPT_ROWS, block0=0, nblocks=mp // PROMPT_ROWS)
    y_s = _final_norm(x, norm_final, rows=ns, block0=mp // ns, nblocks=1)
    conv_s, pool_s = (jnp.swapaxes(t, 1, 2) for t in even_new)
    (conf_s,) = (jnp.swapaxes(t, 1, 2) for t in odd_new)
    return (y_p.reshape(batch, seq, d), y_s.reshape(ns, 1, d),
            jnp.stack(conv_p), conv_s, jnp.stack(pool_p), pool_s,
            jnp.stack(chunk_p), jnp.stack(chunk_s), jnp.stack(conf_p), conf_s)
```

```python
import functools

import jax
import jax.numpy as jnp
from jax import lax
from jax.experimental import pallas as pl
from jax.experimental.pallas import tpu as pltpu

F32 = jnp.float32
BF16 = jnp.bfloat16

EPS = 1e-6
PAST_LEN = 16384
POOL_WINDOWS = (2, 4, 8, 16)
CHUNK = 128

LANES = 128
SUBLANES = 8
VMEM_LIMIT_BYTES = 56 * 1024 * 1024

WIDE_ROW_TILE = 4160
DOWN_ROW_TILE = 1040
PROMPT_ROWS = 1024
MIX_TILE = 256
EVEN_MIX_TILE = 256
POOL_HALO = 16
CONF_HALO = 32


def _params(*sem):
    return pltpu.CompilerParams(dimension_semantics=sem,
                                vmem_limit_bytes=VMEM_LIMIT_BYTES)


def _rmsnorm(x, g):
    ms = jnp.mean(x * x, axis=-1, keepdims=True)
    return x * lax.rsqrt(ms + EPS) * g


def _layernorm(x, g, b):
    mu = jnp.mean(x, axis=-1, keepdims=True)
    xc = x - mu
    var = jnp.mean(xc * xc, axis=-1, keepdims=True)
    return xc * lax.rsqrt(var + EPS) * g + b


def _dot(a, b):
    return jnp.dot(a, b, preferred_element_type=F32)


def _lane_chunks(n):
    return [slice(c * LANES, (c + 1) * LANES) for c in range(n // LANES)]


def _gain_columns(g):
    return jnp.broadcast_to(g[:, None], (g.shape[0], LANES))


def _emit_stream(o, x_out, xb_out, ssq_out):
    x_out[...] = o
    xb_out[...] = o.astype(BF16)
    ssq_out[...] = jnp.broadcast_to(jnp.sum(o * o, axis=-1, keepdims=True),
                                    ssq_out.shape)


def _stream_shapes(total_rows, d):
    return [jax.ShapeDtypeStruct((total_rows, d), F32),
            jax.ShapeDtypeStruct((total_rows, d), BF16),
            jax.ShapeDtypeStruct((total_rows, LANES), F32)]


def _prep_kernel(x_ref, *refs):
    xb_out, ssq_out = refs[-2:]
    x = x_ref[...]
    xb_out[...] = x.astype(BF16)
    ssq_out[...] = jnp.broadcast_to(jnp.sum(x * x, axis=-1, keepdims=True),
                                    ssq_out.shape)


def _prep(x2d, *, rows, block0, total_rows, prev=None):
    n, d = x2d.shape
    in_specs = [pl.BlockSpec((rows, d), lambda i: (i, 0))]
    args = [x2d]
    aliases = {}
    if prev is not None:
        in_specs += [pl.BlockSpec(memory_space=pl.ANY)] * 2
        args += list(prev)
        aliases = {1: 0, 2: 1}
    return pl.pallas_call(
        _prep_kernel,
        grid=(n // rows,),
        in_specs=in_specs,
        out_specs=[pl.BlockSpec((rows, d), lambda i: (block0 + i, 0)),
                   pl.BlockSpec((rows, LANES), lambda i: (block0 + i, 0))],
        out_shape=_stream_shapes(total_rows, d)[1:],
        input_output_aliases=aliases,
        compiler_params=_params("arbitrary"),
        name="stream_entry",
    )(*args)


def _resident_rows(tm, k):
    return pl.BlockSpec((tm, k), lambda i, j: (i, 0), pipeline_mode=pl.Buffered(1))


def _scaled_weight(w_ref, g_ref, wb_ref):
    for sl in _lane_chunks(w_ref.shape[1]):
        wb_ref[:, sl] = (w_ref[:, sl] * g_ref[...]).astype(BF16)


def _row_scale(ssq_ref, d):
    return lax.rsqrt(ssq_ref[...] / d + EPS)


def _proj_kernel(xb_ref, ssq_ref, g_ref, w_ref, o_ref, wb_ref):
    _scaled_weight(w_ref, g_ref, wb_ref)
    r = _row_scale(ssq_ref, xb_ref.shape[1])
    acc = _dot(xb_ref[...], wb_ref[...])
    for sl in _lane_chunks(o_ref.shape[1]):
        o_ref[:, sl] = (acc[:, sl] * r).astype(o_ref.dtype)


def _proj(xb, ssq, g, w_stack, layer, *, tn=512):
    m, d = xb.shape
    n = w_stack.shape[2]
    return pl.pallas_call(
        _proj_kernel,
        grid=(m // WIDE_ROW_TILE, n // tn),
        in_specs=[_resident_rows(WIDE_ROW_TILE, d),
                  pl.BlockSpec((WIDE_ROW_TILE, LANES), lambda i, j: (i, 0)),
                  pl.BlockSpec((d, LANES), lambda i, j: (0, 0)),
                  pl.BlockSpec((None, d, tn), lambda i, j: (layer, 0, j))],
        out_specs=pl.BlockSpec((WIDE_ROW_TILE, tn), lambda i, j: (i, j)),
        out_shape=jax.ShapeDtypeStruct((m, n), BF16),
        scratch_shapes=[pltpu.VMEM((d, tn), BF16)],
        compiler_params=_params("parallel", "arbitrary"),
        name="mixer_in_proj",
    )(xb, ssq, _gain_columns(g), w_stack)


DOWN_COLS = 512
CAST_ROWS = 128


def _gate_up_kernel(cast_next, xb_ref, ssq_ref, g_ref, wg_ref, wu_ref, wd_ref, *refs):
    if cast_next:
        wo_ref, h_ref, wdb_ref, wob_ref, wgb_ref, wub_ref = refs
    else:
        h_ref, wdb_ref, wgb_ref, wub_ref = refs
    _scaled_weight(wg_ref, g_ref, wgb_ref)
    _scaled_weight(wu_ref, g_ref, wub_ref)
    r = _row_scale(ssq_ref, xb_ref.shape[1])
    xb = xb_ref[...]
    gate = _dot(xb, wgb_ref[...])
    up = _dot(xb, wub_ref[...])
    for sl in _lane_chunks(h_ref.shape[1]):
        gt = gate[:, sl] * r
        h_ref[:, sl] = (gt * jax.nn.sigmoid(gt) * (up[:, sl] * r)).astype(BF16)
    @pl.when(pl.program_id(0) == 0)
    def _():
        for c in range(wdb_ref.shape[0]):
            wdb_ref[c] = wd_ref[:, c * DOWN_COLS:(c + 1) * DOWN_COLS].astype(BF16)

    if cast_next:
        @pl.when((pl.program_id(0) == 0) & (pl.program_id(1) < cast_next))
        def _():
            wob_ref[...] = wo_ref[...].astype(BF16)


def _gate_up(xb, ssq, g, wg_stack, wu_stack, wd_stack, layer, next_out=None, *, tn=256):
    m, d = xb.shape
    f = wg_stack.shape[2]
    n_down = wd_stack.shape[2]

    def wspec():
        return pl.BlockSpec((None, d, tn), lambda i, j: (layer, 0, j))

    def parked(i, j, nblocks):
        return jnp.where((i == 0) & (j < nblocks), j, nblocks - 1)

    in_specs = [_resident_rows(WIDE_ROW_TILE, d),
                pl.BlockSpec((WIDE_ROW_TILE, LANES), lambda i, j: (i, 0)),
                pl.BlockSpec((d, LANES), lambda i, j: (0, 0)),
                wspec(), wspec(),
                pl.BlockSpec((None, tn, n_down),
                             lambda i, j: (layer, parked(i, j, f // tn), 0))]
    out_specs = [pl.BlockSpec((WIDE_ROW_TILE, tn), lambda i, j: (i, j)),
                 pl.BlockSpec((n_down // DOWN_COLS, tn, DOWN_COLS),
                              lambda i, j: (0, parked(i, j, f // tn), 0))]
    out_shape = [jax.ShapeDtypeStruct((m, f), BF16),
                 jax.ShapeDtypeStruct((n_down // DOWN_COLS, f, DOWN_COLS), BF16)]
    args = [xb, ssq, _gain_columns(g), wg_stack, wu_stack, wd_stack]
    cast_blocks = 0
    if next_out is not None:
        wo_stack, wo_layer = next_out
        ko, no = wo_stack.shape[1:]
        assert ko % CAST_ROWS == 0 and ko // CAST_ROWS <= f // tn
        cast_blocks = ko // CAST_ROWS
        in_specs.append(pl.BlockSpec(
            (None, CAST_ROWS, no), lambda i, j: (wo_layer, parked(i, j, cast_blocks), 0)))
        out_specs.append(pl.BlockSpec((CAST_ROWS, no),
                                      lambda i, j: (parked(i, j, cast_blocks), 0)))
        out_shape.append(jax.ShapeDtypeStruct((ko, no), BF16))
        args.append(wo_stack)
    return pl.pallas_call(
        functools.partial(_gate_up_kernel, cast_blocks),
        grid=(m // WIDE_ROW_TILE, f // tn),
        in_specs=in_specs,
        out_specs=out_specs,
        out_shape=out_shape,
        scratch_shapes=[pltpu.VMEM((d, tn), BF16), pltpu.VMEM((d, tn), BF16)],
        compiler_params=_params("arbitrary", "arbitrary"),
        name="ffn_gate_up",
    )(*args)


def _matmul_res_kernel(a_ref, w_ref, r_ref, o_ref, *stream_refs):
    o = r_ref[...] + _dot(a_ref[...], w_ref[...])
    o_ref[...] = o
    if not stream_refs:
        return
    ob_ref, ssq_ref = stream_refs
    ob_ref[...] = o.astype(BF16)
    part = jnp.broadcast_to(jnp.sum(o * o, axis=-1, keepdims=True), ssq_ref.shape)

    @pl.when(pl.program_id(1) == 0)
    def _():
        ssq_ref[...] = part

    @pl.when(pl.program_id(1) > 0)
    def _():
        ssq_ref[...] += part


def _matmul_res(a, w_tiles, r, *, tm, name, emit_stream=True):
    m, k = a.shape
    ntiles, _, tn = w_tiles.shape
    n = ntiles * tn
    nouts = 3 if emit_stream else 1
    return pl.pallas_call(
        _matmul_res_kernel,
        grid=(m // tm, ntiles),
        in_specs=[pl.BlockSpec((tm, k), lambda i, j: (i, 0)),
                  pl.BlockSpec((None, k, tn), lambda i, j: (j, 0, 0)),
                  pl.BlockSpec((tm, tn), lambda i, j: (i, j))],
        out_specs=[pl.BlockSpec((tm, tn), lambda i, j: (i, j)),
                   pl.BlockSpec((tm, tn), lambda i, j: (i, j)),
                   pl.BlockSpec((tm, LANES), lambda i, j: (i, 0))][:nouts],
        out_shape=_stream_shapes(m, n)[:nouts],
        compiler_params=_params("parallel", "arbitrary"),
        name=name,
    )(a, w_tiles, r)


def _cast_kernel(w_ref, o_ref):
    o_ref[...] = w_ref[...].astype(o_ref.dtype)


def _cast_bf16(w_stack, layer, *, rows=512):
    k, n = w_stack.shape[1:]
    return pl.pallas_call(
        _cast_kernel,
        grid=(k // rows,),
        in_specs=[pl.BlockSpec((None, rows, n), lambda i: (layer, i, 0))],
        out_specs=pl.BlockSpec((rows, n), lambda i: (i, 0)),
        out_shape=jax.ShapeDtypeStruct((k, n), BF16),
        compiler_params=_params("parallel"),
        name="out_proj_weight_bf16",
    )(w_stack)


def _final_norm_kernel(x_ref, g_ref, o_ref):
    o_ref[...] = _rmsnorm(x_ref[...], g_ref[...])


def _final_norm(x, g, *, rows, block0, nblocks):
    d = x.shape[1]
    return pl.pallas_call(
        _final_norm_kernel,
        grid=(nblocks,),
        in_specs=[pl.BlockSpec((rows, d), lambda i: (block0 + i, 0)),
                  pl.BlockSpec((1, d), lambda i: (0, 0))],
        out_specs=pl.BlockSpec((rows, d), lambda i: (i, 0)),
        out_shape=jax.ShapeDtypeStruct((rows * nblocks, d), F32),
        compiler_params=_params("parallel"),
        name="final_norm",
    )(x, g.reshape(1, d))


PROJ_CHUNK = 256


def _pipelined_tail(mixer_stages, cat_a, cat_b, wob_ref, xres_ref, stream_outs, *,
                    interleave):
    s = pl.program_id(0)
    x_out, xb_out, ssq_out = stream_outs
    nchunks = wob_ref.shape[1] // PROJ_CHUNK

    @pl.when(s == 0)
    def _():
        cat_b[...] = jnp.zeros_like(cat_b)

    def step(cat_w, cat_r):
        stages = mixer_stages(cat_w)
        if not interleave:
            for stage in stages:
                stage()
            o = xres_ref[...] + _dot(cat_r[...], wob_ref[...])
            _emit_stream(o, *stream_outs)
            return
        ssq = jnp.zeros((x_out.shape[0], 1), F32)
        done = 0
        for k, stage in enumerate(stages):
            upto = (k + 1) * nchunks // len(stages)
            for c in range(done, upto):
                sl = slice(c * PROJ_CHUNK, (c + 1) * PROJ_CHUNK)
                o = xres_ref[:, sl] + _dot(cat_r[...], wob_ref[:, sl])
                x_out[:, sl] = o
                xb_out[:, sl] = o.astype(BF16)
                ssq = ssq + jnp.sum(o * o, axis=-1, keepdims=True)
            done = upto
            stage()
        ssq_out[...] = jnp.broadcast_to(ssq, ssq_out.shape)

    @pl.when(s % 2 == 0)
    def _():
        step(cat_a, cat_b)

    @pl.when(s % 2 == 1)
    def _():
        step(cat_b, cat_a)


def _tail_specs(ntiles, tt, d):
    def mixed(s):
        return jnp.minimum(s, ntiles - 1)

    def projected(s):
        return jnp.maximum(s - 1, 0)

    wob = pl.BlockSpec((d, d), lambda s: (0, 0), pipeline_mode=pl.Buffered(1))
    xres = pl.BlockSpec((tt, d), lambda s: (projected(s), 0))
    outs = [pl.BlockSpec((tt, d), lambda s: (projected(s), 0)),
            pl.BlockSpec((tt, d), lambda s: (projected(s), 0)),
            pl.BlockSpec((tt, LANES), lambda s: (projected(s), 0))]
    return mixed, wob, xres, outs


def _pool_project(pooled, w_pool_ref, scale_ref, cat_ref, col0):
    gw = w_pool_ref.shape[1]
    for g in range(len(POOL_WINDOWS)):
        sl = slice(g * gw, (g + 1) * gw)
        yb = _dot(pooled[:, sl].astype(BF16), w_pool_ref[g].astype(BF16))
        cat_ref[:, col0 + g * gw:col0 + (g + 1) * gw] = (
            yb * scale_ref[:, sl]).astype(cat_ref.dtype)


def _even_prompt_kernel(nt, ntiles,
                        xin_ref, gpre_ref, gpost_ref, p_ref,
                        hxin_ref, hgpre_ref, hp_ref,
                        wconv_ref, wpool_ref, scale_ref, wob_ref, xres_ref,
                        x_out, xb_out, ssq_out, conv_state_ref, pool_state_ref,
                        gx_ext, p_ext, cat_a, cat_b):
    tt, da = xin_ref.shape
    h = POOL_HALO
    i = jnp.minimum(pl.program_id(0), ntiles - 1) % nt
    live = (i > 0).astype(F32)

    def mixer_stages(cat_ref):
        def short_conv():
            gx = gpre_ref[...].astype(F32) * xin_ref[...].astype(F32)
            gx_ext[0:h, :] = (hgpre_ref[...].astype(F32) * hxin_ref[...].astype(F32)
                              * live)
            gx_ext[h:h + tt, :] = gx
            conv = (gx_ext[pl.ds(h - 2, tt), :] * wconv_ref[0:1, :]
                    + gx_ext[pl.ds(h - 1, tt), :] * wconv_ref[1:2, :]
                    + gx * wconv_ref[2:3, :])
            cat_ref[:, 0:da] = (gpost_ref[...].astype(F32) * conv).astype(cat_ref.dtype)
            conv_state_ref[...] = gx_ext[pl.ds(h + tt - 2, 2), :]
            p_ext[0:h, :] = hp_ref[...].astype(F32) * live
            p_ext[h:h + tt, :] = p_ref[...].astype(F32)
            nh = pool_state_ref.shape[0]
            pool_state_ref[...] = p_ext[pl.ds(h + tt - nh, nh), :]

        def pool_group(g, w):
            def run():
                gw = da // len(POOL_WINDOWS)
                sl = slice(g * gw, (g + 1) * gw)
                pg = p_ext[pl.ds(h, tt), sl]
                acc = pg
                for k in range(1, w):
                    acc = acc + p_ext[pl.ds(h - k, tt), sl]
                pos = i * tt + lax.broadcasted_iota(jnp.int32, (tt, 1), 0)
                cnt = jnp.minimum(w, pos + 1).astype(F32)
                pooled = acc / cnt - pg
                yb = _dot(pooled.astype(BF16), wpool_ref[g].astype(BF16))
                cat_ref[:, da + g * gw:da + (g + 1) * gw] = (
                    yb * scale_ref[:, sl]).astype(cat_ref.dtype)
            return run

        return [short_conv] + [pool_group(g, w) for g, w in enumerate(POOL_WINDOWS)]

    _pipelined_tail(mixer_stages, cat_a, cat_b, wob_ref, xres_ref,
                    (x_out, xb_out, ssq_out), interleave=True)


def _even_prompt(proj, x, w_conv, w_pool, pool_scale, wob, layer, *, batch, seq,
                 total_rows):
    da = w_conv.shape[2]
    d = x.shape[1]
    tt = EVEN_MIX_TILE
    nt = seq // tt
    ntiles = batch * nt
    hb = tt // POOL_HALO
    nwin = len(POOL_WINDOWS)
    nh = POOL_WINDOWS[-1] - 1
    mixed, wob_spec, xres_spec, stream_specs = _tail_specs(ntiles, tt, d)

    def main(c):
        return pl.BlockSpec((tt, da), lambda s: (mixed(s), c))

    def halo(c):
        return pl.BlockSpec((POOL_HALO, da),
                            lambda s: (jnp.maximum(mixed(s) * hb - 1, 0), c))

    return pl.pallas_call(
        functools.partial(_even_prompt_kernel, nt, ntiles),
        grid=(ntiles + 1,),
        in_specs=[main(0), main(1), main(2), main(3), halo(0), halo(1), halo(3),
                  pl.BlockSpec((None, 3, da), lambda s: (layer, 0, 0)),
                  pl.BlockSpec((None, nwin, da // nwin, da // nwin),
                               lambda s: (layer, 0, 0, 0)),
                  pl.BlockSpec((None, 1, da), lambda s: (layer, 0, 0)),
                  wob_spec, xres_spec],
        out_specs=stream_specs + [
            pl.BlockSpec((None, 2, da), lambda s: (mixed(s) // nt, 0, 0)),
            pl.BlockSpec((None, nh, da), lambda s: (mixed(s) // nt, 0, 0))],
        out_shape=_stream_shapes(total_rows, d) + [
            jax.ShapeDtypeStruct((batch, 2, da), F32),
            jax.ShapeDtypeStruct((batch, nh, da), F32)],
        scratch_shapes=[pltpu.VMEM((POOL_HALO + tt, da), F32),
                        pltpu.VMEM((POOL_HALO + tt, da), F32),
                        pltpu.VMEM((tt, d), BF16),
                        pltpu.VMEM((tt, d), BF16)],
        compiler_params=_params("arbitrary"),
        name="even_mixer_prompt",
    )(proj, proj, proj, proj, proj, proj, proj,
      w_conv, w_pool, pool_scale.reshape(pool_scale.shape[0], 1, da), wob, x)


def _odd_prompt_kernel(nt, ntiles,
                       u_ref, v_ref, a_ref, g_ref, ha_ref, hg_ref,
                       vng_ref, vnb_ref, ws_ref, bst_ref,
                       wconv_ref, bconv_ref, cng_ref, cnb_ref, wob_ref, xres_ref,
                       x_out, xb_out, ssq_out, chunk_v_ref, conf_state_ref,
                       glu_ext, shifted, cat_a, cat_b):
    tt, dc = u_ref.shape
    h = CONF_HALO
    i = jnp.minimum(pl.program_id(0), ntiles - 1) % nt
    live = (i > 0).astype(F32)

    nheads = ws_ref.shape[0]
    hd = dc // nheads
    nw = wconv_ref.shape[0]

    def mixer_stages(cat_ref):
        vals = {}

        def gate_inputs():
            vals["u"] = jax.nn.gelu(u_ref[...].astype(F32))
            vn = _layernorm(jax.nn.gelu(v_ref[...].astype(F32)), vng_ref[...],
                            vnb_ref[...])
            chunk_v_ref[...] = vn[tt - CHUNK:tt, :]
            vals["vb"] = vn.astype(BF16)

        def gate_heads(heads):
            def run():
                row = lax.broadcasted_iota(jnp.int32, (CHUNK, CHUNK), 0)
                colm = lax.broadcasted_iota(jnp.int32, (CHUNK, CHUNK), 1)
                for hh in heads:
                    ws = jnp.where(colm <= row, ws_ref[hh], 0.0).astype(BF16)
                    bias = bst_ref[:, hh:hh + 1]
                    for c in range(tt // CHUNK):
                        rs = slice(c * CHUNK, (c + 1) * CHUNK)
                        cs = slice(hh * hd, (hh + 1) * hd)
                        mixed = _dot(ws, vals["vb"][rs, cs]) + bias
                        cat_ref[rs, cs] = (vals["u"][rs, cs] * mixed).astype(cat_ref.dtype)
            return run

        def glu_rows():
            glu = a_ref[...].astype(F32) * jax.nn.sigmoid(g_ref[...].astype(F32))
            glu_ext[0:h, :] = (ha_ref[...].astype(F32)
                               * jax.nn.sigmoid(hg_ref[...].astype(F32)) * live)
            glu_ext[h:h + tt, :] = glu
            conf_state_ref[...] = glu_ext[pl.ds(h + tt - (nw - 1), nw - 1), :]
            span = shifted.shape[1]
            for sh in range(1, SUBLANES):
                shifted[sh - 1] = glu_ext[pl.ds(sh, span), :]
            vals["acc"] = glu * wconv_ref[nw - 1:nw, :]

        def taps(ks):
            def run():
                acc = vals["acc"]
                for k in ks:
                    q, sh = divmod(h - (nw - 1) + k, SUBLANES)
                    if sh == 0:
                        tap = glu_ext[pl.ds(q * SUBLANES, tt), :]
                    else:
                        tap = shifted[sh - 1, pl.ds(q * SUBLANES, tt), :]
                    acc = acc + tap * wconv_ref[k:k + 1, :]
                vals["acc"] = acc
            return run

        def conv_norm():
            y = _layernorm(vals["acc"] + bconv_ref[...], cng_ref[...], cnb_ref[...])
            cat_ref[:, dc:2 * dc] = (y * jax.nn.sigmoid(y)).astype(cat_ref.dtype)

        return [gate_inputs, gate_heads(range(nheads)), glu_rows, taps(range(nw - 1)),
                conv_norm]

    _pipelined_tail(mixer_stages, cat_a, cat_b, wob_ref, xres_ref,
                    (x_out, xb_out, ssq_out), interleave=False)


def _odd_prompt(proj, x, v_norm_g, v_norm_b, w_spatial, b_spatial_t, w_conv, b_conv,
                conf_norm_g, conf_norm_b, wob, layer, *, batch, seq, total_rows):
    nw, dc = w_conv.shape[1:]
    nheads = w_spatial.shape[1]
    d = x.shape[1]
    tt = MIX_TILE
    nt = seq // tt
    ntiles = batch * nt
    hb = tt // CONF_HALO
    mixed, wob_spec, xres_spec, stream_specs = _tail_specs(ntiles, tt, d)

    def main(c):
        return pl.BlockSpec((tt, dc), lambda s: (mixed(s), c))

    def halo(c):
        return pl.BlockSpec((CONF_HALO, dc),
                            lambda s: (jnp.maximum(mixed(s) * hb - 1, 0), c))

    def vec():
        return pl.BlockSpec((None, 1, dc), lambda s: (layer, 0, 0))

    def as_rows(v):
        return v.reshape(v.shape[0], 1, dc)

    return pl.pallas_call(
        functools.partial(_odd_prompt_kernel, nt, ntiles),
        grid=(ntiles + 1,),
        in_specs=[main(0), main(1), main(2), main(3), halo(2), halo(3),
                  vec(), vec(),
                  pl.BlockSpec((None, nheads, CHUNK, CHUNK), lambda s: (layer, 0, 0, 0)),
                  pl.BlockSpec((None, CHUNK, nheads), lambda s: (layer, 0, 0)),
                  pl.BlockSpec((None, nw, dc), lambda s: (layer, 0, 0)),
                  vec(), vec(), vec(), wob_spec, xres_spec],
        out_specs=stream_specs + [
            pl.BlockSpec((None, CHUNK, dc), lambda s: (mixed(s) // nt, 0, 0)),
            pl.BlockSpec((None, nw - 1, dc), lambda s: (mixed(s) // nt, 0, 0))],
        out_shape=_stream_shapes(total_rows, d) + [
            jax.ShapeDtypeStruct((batch, CHUNK, dc), F32),
            jax.ShapeDtypeStruct((batch, nw - 1, dc), F32)],
        scratch_shapes=[pltpu.VMEM((CONF_HALO + tt, dc), F32),
                        pltpu.VMEM((SUBLANES - 1, CONF_HALO - SUBLANES + tt, dc), F32),
                        pltpu.VMEM((tt, d), BF16),
                        pltpu.VMEM((tt, d), BF16)],
        compiler_params=_params("arbitrary"),
        name="odd_mixer_prompt",
    )(proj, proj, proj, proj, proj, proj,
      as_rows(v_norm_g), as_rows(v_norm_b), w_spatial, b_spatial_t,
      w_conv, as_rows(b_conv), as_rows(conf_norm_g), as_rows(conf_norm_b), wob, x)


def _push_copies(hist_ref, new_any, layer, new_row, sems, slot):
    k = hist_ref.shape[1]
    shift = pltpu.make_async_copy(hist_ref.at[0, pl.ds(1, k - 1)],
                                  new_any.at[layer, pl.ds(0, k - 1)], sems.at[slot])
    last = pltpu.make_async_copy(new_row, new_any.at[layer, k - 1], sems.at[slot + 1])
    return shift, last


def _even_sample_kernel(layer, nprev,
                        xin_ref, gpre_ref, gpost_ref, p_ref,
                        conv_hist_ref, pool_hist_ref,
                        wconv_ref, wpool_ref, scale_ref, wob_ref, xres_ref,
                        x_in, xb_in, ssq_in, *refs):
    del x_in, xb_in, ssq_in
    x_out, xb_out, ssq_out, new_conv, new_pool, cat_ref, gx_row, p_row, sems = refs[nprev:]
    s, da = xin_ref.shape
    conv_shift, conv_last = _push_copies(conv_hist_ref, new_conv, layer, gx_row, sems, 0)
    pool_shift, pool_last = _push_copies(pool_hist_ref, new_pool, layer, p_row, sems, 2)
    conv_shift.start()
    pool_shift.start()

    gx = gpre_ref[...].astype(F32) * xin_ref[...].astype(F32)
    gx_row[...] = gx
    conv_last.start()
    nc = conv_hist_ref.shape[1]
    conv = gx * wconv_ref[nc:nc + 1, :]
    for k in range(nc):
        conv = conv + conv_hist_ref[0, k] * wconv_ref[k:k + 1, :]
    cat_ref[:, 0:da] = (gpost_ref[...].astype(F32) * conv).astype(cat_ref.dtype)

    p = p_ref[...].astype(F32)
    p_row[...] = p
    pool_last.start()
    nh = pool_hist_ref.shape[1]
    gw = da // len(POOL_WINDOWS)
    pooled = []
    for g, w in enumerate(POOL_WINDOWS):
        sl = slice(g * gw, (g + 1) * gw)
        acc = p[:, sl]
        for back in range(1, w):
            acc = acc + pool_hist_ref[0, nh - back, :, sl]
        cnt = float(min(w, PAST_LEN + 1))
        pooled.append(acc / cnt - p[:, sl])
    pooled = jnp.concatenate(pooled, axis=-1)
    _pool_project(pooled, wpool_ref, scale_ref, cat_ref, da)
    o = xres_ref[...] + _dot(cat_ref[...], wob_ref[...])
    _emit_stream(o, x_out, xb_out, ssq_out)
    conv_shift.wait()
    pool_shift.wait()
    conv_last.wait()
    pool_last.wait()


def _sample_stream_specs(s, d, xres_block, out_block):
    wob = pl.BlockSpec((d, d), lambda i: (0, 0), pipeline_mode=pl.Buffered(1))
    xres = pl.BlockSpec((s, d), lambda i: (xres_block, 0))
    anys = [pl.BlockSpec(memory_space=pl.ANY)] * 3
    outs = [pl.BlockSpec((s, d), lambda i: (out_block, 0)),
            pl.BlockSpec((s, d), lambda i: (out_block, 0)),
            pl.BlockSpec((s, LANES), lambda i: (out_block, 0))]
    return wob, xres, anys, outs


def _history_block(hist_t, layer):
    return pl.BlockSpec((1,) + hist_t.shape[1:], lambda i: (layer, 0, 0, 0),
                        pipeline_mode=pl.Buffered(1))


def _even_sample(proj, xres, xres_block, stream, conv_hist_t, pool_hist_t, new_states,
                 w_conv, w_pool, pool_scale, wob, layer, *, row0):
    nc, s, da = conv_hist_t.shape[1:]
    total_rows, d = stream[0].shape
    rb = row0 // s
    nwin = len(POOL_WINDOWS)
    wob_spec, xres_spec, any_specs, stream_specs = _sample_stream_specs(
        s, d, xres_block, rb)
    nprev = len(new_states)
    any_spec = pl.BlockSpec(memory_space=pl.ANY)

    def seg(c):
        return pl.BlockSpec((s, da), lambda i: (rb, c))

    in_specs = [seg(0), seg(1), seg(2), seg(3),
                _history_block(conv_hist_t, layer), _history_block(pool_hist_t, layer),
                pl.BlockSpec((None, nc + 1, da), lambda i: (layer, 0, 0)),
                pl.BlockSpec((None, nwin, da // nwin, da // nwin),
                             lambda i: (layer, 0, 0, 0)),
                pl.BlockSpec((None, 1, da), lambda i: (layer, 0, 0)),
                wob_spec, xres_spec] + any_specs + [any_spec] * nprev
    first_prev = len(in_specs) - nprev
    aliases = {11: 0, 12: 1, 13: 2}
    aliases.update({first_prev + k: 3 + k for k in range(nprev)})
    return pl.pallas_call(
        functools.partial(_even_sample_kernel, layer, nprev),
        grid=(1,),
        in_specs=in_specs,
        out_specs=stream_specs + [any_spec, any_spec],
        out_shape=_stream_shapes(total_rows, d) + [
            jax.ShapeDtypeStruct(conv_hist_t.shape, F32),
            jax.ShapeDtypeStruct(pool_hist_t.shape, F32)],
        scratch_shapes=[pltpu.VMEM((s, d), BF16), pltpu.VMEM((s, da), F32),
                        pltpu.VMEM((s, da), F32), pltpu.SemaphoreType.DMA((4,))],
        input_output_aliases=aliases,
        compiler_params=_params("arbitrary"),
        name="even_mixer_sample",
    )(proj, proj, proj, proj, conv_hist_t, pool_hist_t,
      w_conv, w_pool, pool_scale.reshape(pool_scale.shape[0], 1, da), wob, xres, *stream,
      *new_states)


def _odd_sample_kernel(layer, nprev,
                       u_ref, v_ref, a_ref, g_ref, hist_ref,
                       vng_ref, vnb_ref, ws0_ref, b0_ref,
                       wconv_ref, bconv_ref, cng_ref, cnb_ref, wob_ref, xres_ref,
                       x_in, xb_in, ssq_in, *refs):
    del x_in, xb_in, ssq_in
    x_out, xb_out, ssq_out, vn_ref, new_hist, cat_ref, glu_row, sems = refs[nprev:]
    s, dc = u_ref.shape
    shift, last = _push_copies(hist_ref, new_hist, layer, glu_row, sems, 0)
    shift.start()
    glu = a_ref[...].astype(F32) * jax.nn.sigmoid(g_ref[...].astype(F32))
    glu_row[...] = glu
    last.start()

    u = jax.nn.gelu(u_ref[...].astype(F32))
    vn = _layernorm(jax.nn.gelu(v_ref[...].astype(F32)), vng_ref[...], vnb_ref[...])
    vn_ref[...] = vn
    mixed = ws0_ref[...] * vn + b0_ref[...]
    cat_ref[:, 0:dc] = (u * mixed).astype(cat_ref.dtype)

    nh = hist_ref.shape[1]
    acc = glu * wconv_ref[nh:nh + 1, :]
    for k in range(nh):
        acc = acc + hist_ref[0, k] * wconv_ref[k:k + 1, :]
    y = _layernorm(acc + bconv_ref[...], cng_ref[...], cnb_ref[...])
    cat_ref[:, dc:2 * dc] = (y * jax.nn.sigmoid(y)).astype(cat_ref.dtype)
    o = xres_ref[...] + _dot(cat_ref[...], wob_ref[...])
    _emit_stream(o, x_out, xb_out, ssq_out)
    shift.wait()
    last.wait()


def _odd_sample(proj, xres, xres_block, stream, hist_t, new_states, v_norm_g, v_norm_b,
                w_spatial, b_spatial, w_conv, b_conv, conf_norm_g, conf_norm_b, wob, layer,
                *, row0):
    nh, s, dc = hist_t.shape[1:]
    nheads = w_spatial.shape[1]
    hd = dc // nheads
    total_rows, d = stream[0].shape
    rb = row0 // s
    ws0 = jnp.repeat(w_spatial[layer, :, 0, 0], hd).reshape(1, dc)
    b0 = jnp.repeat(b_spatial[layer, :, 0], hd).reshape(1, dc)
    wob_spec, xres_spec, any_specs, stream_specs = _sample_stream_specs(
        s, d, xres_block, rb)
    nprev = len(new_states)
    any_spec = pl.BlockSpec(memory_space=pl.ANY)

    def seg(c):
        return pl.BlockSpec((s, dc), lambda i: (rb, c))

    def vec():
        return pl.BlockSpec((None, 1, dc), lambda i: (layer, 0, 0))

    def row():
        return pl.BlockSpec((1, dc), lambda i: (0, 0))

    def as_rows(v):
        return v.reshape(v.shape[0], 1, dc)

    in_specs = [seg(0), seg(1), seg(2), seg(3), _history_block(hist_t, layer),
                vec(), vec(), row(), row(),
                pl.BlockSpec((None, nh + 1, dc), lambda i: (layer, 0, 0)),
                vec(), vec(), vec(), wob_spec, xres_spec] + any_specs + (
                    [any_spec] * nprev)
    first_prev = len(in_specs) - nprev
    aliases = {15: 0, 16: 1, 17: 2}
    aliases.update({first_prev + k: 4 + k for k in range(nprev)})
    return pl.pallas_call(
        functools.partial(_odd_sample_kernel, layer, nprev),
        grid=(1,),
        in_specs=in_specs,
        out_specs=stream_specs + [pl.BlockSpec((s, dc), lambda i: (0, 0)), any_spec],
        out_shape=_stream_shapes(total_rows, d) + [
            jax.ShapeDtypeStruct((s, dc), F32),
            jax.ShapeDtypeStruct(hist_t.shape, F32)],
        scratch_shapes=[pltpu.VMEM((s, d), BF16), pltpu.VMEM((s, dc), F32),
                        pltpu.SemaphoreType.DMA((2,))],
        input_output_aliases=aliases,
        compiler_params=_params("arbitrary"),
        name="odd_mixer_sample",
    )(proj, proj, proj, proj, hist_t,
      as_rows(v_norm_g), as_rows(v_norm_b), ws0, b0,
      w_conv, as_rows(b_conv), as_rows(conf_norm_g), as_rows(conf_norm_b), wob, xres,
      *stream, *new_states)


def kernel(x_prompt, x_sample, state_conv_a, state_pool, state_conformer, norm_mix, norm_ffn, w_in_even, w_conv_a, w_pool, pool_scale, w_out_even, w_in_odd, v_norm_g, v_norm_b, w_spatial, b_spatial, w_conv_d, b_conv_d, conf_norm_g, conf_norm_b, w_out_odd, w_ffn_gate, w_ffn_up, w_ffn_down, norm_final):
    batch, seq, d = x_prompt.shape
    ns = x_sample.shape[0]
    mp = batch * seq
    m = mp + ns
    depth = norm_mix.shape[0]
    assert x_sample.shape[1] == 1 and mp % ns == 0 and mp % PROMPT_ROWS == 0
    assert m % WIDE_ROW_TILE == 0 and m % DOWN_ROW_TILE == 0
    assert seq % MIX_TILE == 0 and MIX_TILE % CHUNK == 0 and seq % EVEN_MIX_TILE == 0

    x_prompt2d = x_prompt.reshape(mp, d)
    x_sample2d = x_sample.reshape(ns, d)
    entry = _prep(x_prompt2d, rows=PROMPT_ROWS, block0=0, total_rows=m)
    xb, ssq = _prep(x_sample2d, rows=ns, block0=mp // ns, total_rows=m, prev=entry)
    xres_p, xres_s, xres_s_block = x_prompt2d, x_sample2d, 0
    b_spatial_t = jnp.swapaxes(b_spatial, 1, 2)
    conv_hist_t = jnp.swapaxes(state_conv_a, 1, 2)
    pool_hist_t = jnp.swapaxes(state_pool, 1, 2)
    conf_hist_t = jnp.swapaxes(state_conformer, 1, 2)
    even_new, odd_new = [], []
    conv_p, pool_p, chunk_p, chunk_s, conf_p = [], [], [], [], []
    w_outs = [(w_out_even if l % 2 == 0 else w_out_odd, l // 2) for l in range(depth)]
    wob = _cast_bf16(*w_outs[0])
    for l in range(depth):
        i = l // 2
        if l % 2 == 0:
            proj = _proj(xb, ssq, norm_mix[l], w_in_even, i)
            *stream, cs_p, ps_p = _even_prompt(proj, xres_p, w_conv_a, w_pool, pool_scale,
                                               wob, i, batch=batch, seq=seq, total_rows=m)
            x, xb, ssq, *even_new = _even_sample(
                proj, xres_s, xres_s_block, stream, conv_hist_t, pool_hist_t, even_new,
                w_conv_a, w_pool, pool_scale, wob, i, row0=mp)
            conv_p.append(cs_p)
            pool_p.append(ps_p)
        else:
            proj = _proj(xb, ssq, norm_mix[l], w_in_odd, i)
            *stream, cv_p, cf_p = _odd_prompt(
                proj, xres_p, v_norm_g, v_norm_b, w_spatial, b_spatial_t, w_conv_d, b_conv_d,
                conf_norm_g, conf_norm_b, wob, i, batch=batch, seq=seq, total_rows=m)
            x, xb, ssq, vn_s, *odd_new = _odd_sample(
                proj, xres_s, xres_s_block, stream, conf_hist_t, odd_new, v_norm_g, v_norm_b,
                w_spatial, b_spatial, w_conv_d, b_conv_d, conf_norm_g, conf_norm_b, wob, i,
                row0=mp)
            chunk_p.append(cv_p)
            conf_p.append(cf_p)
            chunk_s.append(vn_s[:, None, :])
        last = l == depth - 1
        h, wd_tiles, *next_wob = _gate_up(xb, ssq, norm_ffn[l], w_ffn_gate, w_ffn_up,
                                          w_ffn_down, l, None if last else w_outs[l + 1])
        x, *rest = _matmul_res(h, wd_tiles, x, tm=DOWN_ROW_TILE, name="ffn_down",
                               emit_stream=not last)
        if not last:
            (wob,), (xb, ssq) = next_wob, rest
        xres_p, xres_s, xres_s_block = x, x, mp // ns
    y_p = _final_norm(x, norm_final, rows=PROMPT_ROWS, block0=0, nblocks=mp // PROMPT_ROWS)
    y_s = _final_norm(x, norm_final, rows=ns, block0=mp // ns, nblocks=1)
    conv_s, pool_s = (jnp.swapaxes(t, 1, 2) for t in even_new)
    (conf_s,) = (jnp.swapaxes(t, 1, 2) for t in odd_new)
    return (y_p.reshape(batch, seq, d), y_s.reshape(ns, 1, d),
            jnp.stack(conv_p), conv_s, jnp.stack(pool_p), pool_s,
            jnp.stack(chunk_p), jnp.stack(chunk_s), jnp.stack(conf_p), conf_s)
```

```python
import functools

import jax
import jax.numpy as jnp
from jax import lax
from jax.experimental import pallas as pl
from jax.experimental.pallas import tpu as pltpu

F32 = jnp.float32
BF16 = jnp.bfloat16

EPS = 1e-6
PAST_LEN = 16384
POOL_WINDOWS = (2, 4, 8, 16)
CHUNK = 128

LANES = 128
SUBLANES = 8
VMEM_LIMIT_BYTES = 56 * 1024 * 1024

WIDE_ROW_TILE = 4160
DOWN_ROW_TILE = 1040
PROMPT_ROWS = 1024
MIX_TILE = 256
EVEN_MIX_TILE = 512
POOL_HALO = 16
CONF_HALO = 32


def _params(*sem):
    return pltpu.CompilerParams(dimension_semantics=sem,
                                vmem_limit_bytes=VMEM_LIMIT_BYTES)


def _rmsnorm(x, g):
    ms = jnp.mean(x * x, axis=-1, keepdims=True)
    return x * lax.rsqrt(ms + EPS) * g


def _layernorm(x, g, b):
    mu = jnp.mean(x, axis=-1, keepdims=True)
    xc = x - mu
    var = jnp.mean(xc * xc, axis=-1, keepdims=True)
    return xc * lax.rsqrt(var + EPS) * g + b


def _dot(a, b):
    return jnp.dot(a, b, preferred_element_type=F32)


def _lane_chunks(n):
    return [slice(c * LANES, (c + 1) * LANES) for c in range(n // LANES)]


def _gain_columns(g):
    return jnp.broadcast_to(g[:, None], (g.shape[0], LANES))


def _emit_stream(o, x_out, xb_out, ssq_out):
    x_out[...] = o
    xb_out[...] = o.astype(BF16)
    ssq_out[...] = jnp.broadcast_to(jnp.sum(o * o, axis=-1, keepdims=True),
                                    ssq_out.shape)


def _stream_shapes(total_rows, d):
    return [jax.ShapeDtypeStruct((total_rows, d), F32),
            jax.ShapeDtypeStruct((total_rows, d), BF16),
            jax.ShapeDtypeStruct((total_rows, LANES), F32)]


def _prep_kernel(x_ref, *refs):
    xb_out, ssq_out = refs[-2:]
    x = x_ref[...]
    xb_out[...] = x.astype(BF16)
    ssq_out[...] = jnp.broadcast_to(jnp.sum(x * x, axis=-1, keepdims=True),
                                    ssq_out.shape)


def _prep(x2d, *, rows, block0, total_rows, prev=None):
    n, d = x2d.shape
    in_specs = [pl.BlockSpec((rows, d), lambda i: (i, 0))]
    args = [x2d]
    aliases = {}
    if prev is not None:
        in_specs += [pl.BlockSpec(memory_space=pl.ANY)] * 2
        args += list(prev)
        aliases = {1: 0, 2: 1}
    return pl.pallas_call(
        _prep_kernel,
        grid=(n // rows,),
        in_specs=in_specs,
        out_specs=[pl.BlockSpec((rows, d), lambda i: (block0 + i, 0)),
                   pl.BlockSpec((rows, LANES), lambda i: (block0 + i, 0))],
        out_shape=_stream_shapes(total_rows, d)[1:],
        input_output_aliases=aliases,
        compiler_params=_params("arbitrary"),
        name="stream_entry",
    )(*args)


def _resident_rows(tm, k):
    return pl.BlockSpec((tm, k), lambda i, j: (i, 0), pipeline_mode=pl.Buffered(1))


def _scaled_weight(w_ref, g_ref, wb_ref):
    for sl in _lane_chunks(w_ref.shape[1]):
        wb_ref[:, sl] = (w_ref[:, sl] * g_ref[...]).astype(BF16)


def _row_scale(ssq_ref, d):
    return lax.rsqrt(ssq_ref[...] / d + EPS)


def _proj_kernel(xb_ref, ssq_ref, g_ref, w_ref, o_ref, wb_ref):
    _scaled_weight(w_ref, g_ref, wb_ref)
    r = _row_scale(ssq_ref, xb_ref.shape[1])
    acc = _dot(xb_ref[...], wb_ref[...])
    for sl in _lane_chunks(o_ref.shape[1]):
        o_ref[:, sl] = (acc[:, sl] * r).astype(o_ref.dtype)


def _proj(xb, ssq, g, w_stack, layer, *, tn=512):
    m, d = xb.shape
    n = w_stack.shape[2]
    return pl.pallas_call(
        _proj_kernel,
        grid=(m // WIDE_ROW_TILE, n // tn),
        in_specs=[_resident_rows(WIDE_ROW_TILE, d),
                  pl.BlockSpec((WIDE_ROW_TILE, LANES), lambda i, j: (i, 0)),
                  pl.BlockSpec((d, LANES), lambda i, j: (0, 0)),
                  pl.BlockSpec((None, d, tn), lambda i, j: (layer, 0, j))],
        out_specs=pl.BlockSpec((WIDE_ROW_TILE, tn), lambda i, j: (i, j)),
        out_shape=jax.ShapeDtypeStruct((m, n), BF16),
        scratch_shapes=[pltpu.VMEM((d, tn), BF16)],
        compiler_params=_params("parallel", "arbitrary"),
        name="mixer_in_proj",
    )(xb, ssq, _gain_columns(g), w_stack)


DOWN_COLS = 512
CAST_ROWS = 128


def _gate_up_kernel(cast_next, xb_ref, ssq_ref, g_ref, wg_ref, wu_ref, wd_ref, *refs):
    if cast_next:
        wo_ref, h_ref, wdb_ref, wob_ref, wgb_ref, wub_ref = refs
    else:
        h_ref, wdb_ref, wgb_ref, wub_ref = refs
    _scaled_weight(wg_ref, g_ref, wgb_ref)
    r = _row_scale(ssq_ref, xb_ref.shape[1])
    xb = xb_ref[...]
    gate = _dot(xb, wgb_ref[...])
    _scaled_weight(wu_ref, g_ref, wub_ref)
    up = _dot(xb, wub_ref[...])
    for sl in _lane_chunks(h_ref.shape[1]):
        gt = gate[:, sl] * r
        h_ref[:, sl] = (gt * jax.nn.sigmoid(gt) * (up[:, sl] * r)).astype(BF16)
    @pl.when(pl.program_id(0) == 0)
    def _():
        for c in range(wdb_ref.shape[0]):
            wdb_ref[c] = wd_ref[:, c * DOWN_COLS:(c + 1) * DOWN_COLS].astype(BF16)

    if cast_next:
        @pl.when((pl.program_id(0) == 0) & (pl.program_id(1) < cast_next))
        def _():
            wob_ref[...] = wo_ref[...].astype(BF16)


def _gate_up(xb, ssq, g, wg_stack, wu_stack, wd_stack, layer, next_out=None, *, tn=256):
    m, d = xb.shape
    f = wg_stack.shape[2]
    n_down = wd_stack.shape[2]

    def wspec():
        return pl.BlockSpec((None, d, tn), lambda i, j: (layer, 0, j))

    def parked(i, j, nblocks):
        return jnp.where((i == 0) & (j < nblocks), j, nblocks - 1)

    in_specs = [_resident_rows(WIDE_ROW_TILE, d),
                pl.BlockSpec((WIDE_ROW_TILE, LANES), lambda i, j: (i, 0)),
                pl.BlockSpec((d, LANES), lambda i, j: (0, 0)),
                wspec(), wspec(),
                pl.BlockSpec((None, tn, n_down),
                             lambda i, j: (layer, parked(i, j, f // tn), 0))]
    out_specs = [pl.BlockSpec((WIDE_ROW_TILE, tn), lambda i, j: (i, j)),
                 pl.BlockSpec((n_down // DOWN_COLS, tn, DOWN_COLS),
                              lambda i, j: (0, parked(i, j, f // tn), 0))]
    out_shape = [jax.ShapeDtypeStruct((m, f), BF16),
                 jax.ShapeDtypeStruct((n_down // DOWN_COLS, f, DOWN_COLS), BF16)]
    args = [xb, ssq, _gain_columns(g), wg_stack, wu_stack, wd_stack]
    cast_blocks = 0
    if next_out is not None:
        wo_stack, wo_layer = next_out
        ko, no = wo_stack.shape[1:]
        assert ko % CAST_ROWS == 0 and ko // CAST_ROWS <= f // tn
        cast_blocks = ko // CAST_ROWS
        in_specs.append(pl.BlockSpec(
            (None, CAST_ROWS, no), lambda i, j: (wo_layer, parked(i, j, cast_blocks), 0)))
        out_specs.append(pl.BlockSpec((CAST_ROWS, no),
                                      lambda i, j: (parked(i, j, cast_blocks), 0)))
        out_shape.append(jax.ShapeDtypeStruct((ko, no), BF16))
        args.append(wo_stack)
    return pl.pallas_call(
        functools.partial(_gate_up_kernel, cast_blocks),
        grid=(m // WIDE_ROW_TILE, f // tn),
        in_specs=in_specs,
        out_specs=out_specs,
        out_shape=out_shape,
        scratch_shapes=[pltpu.VMEM((d, tn), BF16), pltpu.VMEM((d, tn), BF16)],
        compiler_params=_params("arbitrary", "arbitrary"),
        name="ffn_gate_up",
    )(*args)


def _matmul_res_kernel(a_ref, w_ref, r_ref, o_ref, *stream_refs):
    o = r_ref[...] + _dot(a_ref[...], w_ref[...])
    o_ref[...] = o
    if not stream_refs:
        return
    ob_ref, ssq_ref = stream_refs
    ob_ref[...] = o.astype(BF16)
    part = jnp.broadcast_to(jnp.sum(o * o, axis=-1, keepdims=True), ssq_ref.shape)

    @pl.when(pl.program_id(1) == 0)
    def _():
        ssq_ref[...] = part

    @pl.when(pl.program_id(1) > 0)
    def _():
        ssq_ref[...] += part


def _matmul_res(a, w_tiles, r, *, tm, name, emit_stream=True):
    m, k = a.shape
    ntiles, _, tn = w_tiles.shape
    n = ntiles * tn
    nouts = 3 if emit_stream else 1
    return pl.pallas_call(
        _matmul_res_kernel,
        grid=(m // tm, ntiles),
        in_specs=[pl.BlockSpec((tm, k), lambda i, j: (i, 0)),
                  pl.BlockSpec((None, k, tn), lambda i, j: (j, 0, 0)),
                  pl.BlockSpec((tm, tn), lambda i, j: (i, j))],
        out_specs=[pl.BlockSpec((tm, tn), lambda i, j: (i, j)),
                   pl.BlockSpec((tm, tn), lambda i, j: (i, j)),
                   pl.BlockSpec((tm, LANES), lambda i, j: (i, 0))][:nouts],
        out_shape=_stream_shapes(m, n)[:nouts],
        compiler_params=_params("parallel", "arbitrary"),
        name=name,
    )(a, w_tiles, r)


def _cast_kernel(w_ref, o_ref):
    o_ref[...] = w_ref[...].astype(o_ref.dtype)


def _cast_bf16(w_stack, layer, *, rows=512):
    k, n = w_stack.shape[1:]
    return pl.pallas_call(
        _cast_kernel,
        grid=(k // rows,),
        in_specs=[pl.BlockSpec((None, rows, n), lambda i: (layer, i, 0))],
        out_specs=pl.BlockSpec((rows, n), lambda i: (i, 0)),
        out_shape=jax.ShapeDtypeStruct((k, n), BF16),
        compiler_params=_params("parallel"),
        name="out_proj_weight_bf16",
    )(w_stack)


def _final_norm_kernel(x_ref, g_ref, o_ref):
    o_ref[...] = _rmsnorm(x_ref[...], g_ref[...])


def _final_norm(x, g, *, rows, block0, nblocks):
    d = x.shape[1]
    return pl.pallas_call(
        _final_norm_kernel,
        grid=(nblocks,),
        in_specs=[pl.BlockSpec((rows, d), lambda i: (block0 + i, 0)),
                  pl.BlockSpec((1, d), lambda i: (0, 0))],
        out_specs=pl.BlockSpec((rows, d), lambda i: (i, 0)),
        out_shape=jax.ShapeDtypeStruct((rows * nblocks, d), F32),
        compiler_params=_params("parallel"),
        name="final_norm",
    )(x, g.reshape(1, d))


PROJ_CHUNK = 256


def _pipelined_tail(mixer_stages, cat_a, cat_b, wob_ref, xres_ref, stream_outs, *,
                    interleave):
    s = pl.program_id(0)
    x_out, xb_out, ssq_out = stream_outs
    nchunks = wob_ref.shape[1] // PROJ_CHUNK

    @pl.when(s == 0)
    def _():
        cat_b[...] = jnp.zeros_like(cat_b)

    def step(cat_w, cat_r):
        stages = mixer_stages(cat_w)
        if not interleave:
            for stage in stages:
                stage()
            o = xres_ref[...] + _dot(cat_r[...], wob_ref[...])
            _emit_stream(o, *stream_outs)
            return
        ssq = jnp.zeros((x_out.shape[0], 1), F32)
        done = 0
        for k, stage in enumerate(stages):
            upto = (k + 1) * nchunks // len(stages)
            for c in range(done, upto):
                sl = slice(c * PROJ_CHUNK, (c + 1) * PROJ_CHUNK)
                o = xres_ref[:, sl] + _dot(cat_r[...], wob_ref[:, sl])
                x_out[:, sl] = o
                xb_out[:, sl] = o.astype(BF16)
                ssq = ssq + jnp.sum(o * o, axis=-1, keepdims=True)
            done = upto
            stage()
        ssq_out[...] = jnp.broadcast_to(ssq, ssq_out.shape)

    @pl.when(s % 2 == 0)
    def _():
        step(cat_a, cat_b)

    @pl.when(s % 2 == 1)
    def _():
        step(cat_b, cat_a)


def _tail_specs(ntiles, tt, d):
    def mixed(s):
        return jnp.minimum(s, ntiles - 1)

    def projected(s):
        return jnp.maximum(s - 1, 0)

    wob = pl.BlockSpec((d, d), lambda s: (0, 0), pipeline_mode=pl.Buffered(1))
    xres = pl.BlockSpec((tt, d), lambda s: (projected(s), 0))
    outs = [pl.BlockSpec((tt, d), lambda s: (projected(s), 0)),
            pl.BlockSpec((tt, d), lambda s: (projected(s), 0)),
            pl.BlockSpec((tt, LANES), lambda s: (projected(s), 0))]
    return mixed, wob, xres, outs


def _pool_project(pooled, w_pool_ref, scale_ref, cat_ref, col0):
    gw = w_pool_ref.shape[1]
    for g in range(len(POOL_WINDOWS)):
        sl = slice(g * gw, (g + 1) * gw)
        yb = _dot(pooled[:, sl].astype(BF16), w_pool_ref[g].astype(BF16))
        cat_ref[:, col0 + g * gw:col0 + (g + 1) * gw] = (
            yb * scale_ref[:, sl]).astype(cat_ref.dtype)


def _even_prompt_kernel(nt, ntiles,
                        xin_ref, gpre_ref, gpost_ref, p_ref,
                        hxin_ref, hgpre_ref, hp_ref,
                        wconv_ref, wpool_ref, scale_ref, wob_ref, xres_ref,
                        x_out, xb_out, ssq_out, conv_state_ref, pool_state_ref,
                        gx_ext, p_ext, cat_a, cat_b):
    tt, da = xin_ref.shape
    h = POOL_HALO
    i = jnp.minimum(pl.program_id(0), ntiles - 1) % nt
    live = (i > 0).astype(F32)

    def mixer_stages(cat_ref):
        def short_conv():
            gx = gpre_ref[...].astype(F32) * xin_ref[...].astype(F32)
            gx_ext[0:h, :] = (hgpre_ref[...].astype(F32) * hxin_ref[...].astype(F32)
                              * live)
            gx_ext[h:h + tt, :] = gx
            conv = (gx_ext[pl.ds(h - 2, tt), :] * wconv_ref[0:1, :]
                    + gx_ext[pl.ds(h - 1, tt), :] * wconv_ref[1:2, :]
                    + gx * wconv_ref[2:3, :])
            cat_ref[:, 0:da] = (gpost_ref[...].astype(F32) * conv).astype(cat_ref.dtype)
            conv_state_ref[...] = gx_ext[pl.ds(h + tt - 2, 2), :]
            p_ext[0:h, :] = hp_ref[...].astype(F32) * live
            p_ext[h:h + tt, :] = p_ref[...].astype(F32)
            nh = pool_state_ref.shape[0]
            pool_state_ref[...] = p_ext[pl.ds(h + tt - nh, nh), :]

        def pool_group(g, w):
            def run():
                gw = da // len(POOL_WINDOWS)
                sl = slice(g * gw, (g + 1) * gw)
                pg = p_ext[pl.ds(h, tt), sl]
                acc = pg
                for k in range(1, w):
                    acc = acc + p_ext[pl.ds(h - k, tt), sl]
                pos = i * tt + lax.broadcasted_iota(jnp.int32, (tt, 1), 0)
                cnt = jnp.minimum(w, pos + 1).astype(F32)
                pooled = acc / cnt - pg
                yb = _dot(pooled.astype(BF16), wpool_ref[g].astype(BF16))
                cat_ref[:, da + g * gw:da + (g + 1) * gw] = (
                    yb * scale_ref[:, sl]).astype(cat_ref.dtype)
            return run

        return [short_conv] + [pool_group(g, w) for g, w in enumerate(POOL_WINDOWS)]

    _pipelined_tail(mixer_stages, cat_a, cat_b, wob_ref, xres_ref,
                    (x_out, xb_out, ssq_out), interleave=True)


def _even_prompt(proj, x, w_conv, w_pool, pool_scale, wob, layer, *, batch, seq,
                 total_rows):
    da = w_conv.shape[2]
    d = x.shape[1]
    tt = EVEN_MIX_TILE
    nt = seq // tt
    ntiles = batch * nt
    hb = tt // POOL_HALO
    nwin = len(POOL_WINDOWS)
    nh = POOL_WINDOWS[-1] - 1
    mixed, wob_spec, xres_spec, stream_specs = _tail_specs(ntiles, tt, d)

    def main(c):
        return pl.BlockSpec((tt, da), lambda s: (mixed(s), c))

    def halo(c):
        return pl.BlockSpec((POOL_HALO, da),
                            lambda s: (jnp.maximum(mixed(s) * hb - 1, 0), c))

    return pl.pallas_call(
        functools.partial(_even_prompt_kernel, nt, ntiles),
        grid=(ntiles + 1,),
        in_specs=[main(0), main(1), main(2), main(3), halo(0), halo(1), halo(3),
                  pl.BlockSpec((None, 3, da), lambda s: (layer, 0, 0)),
                  pl.BlockSpec((None, nwin, da // nwin, da // nwin),
                               lambda s: (layer, 0, 0, 0)),
                  pl.BlockSpec((None, 1, da), lambda s: (layer, 0, 0)),
                  wob_spec, xres_spec],
        out_specs=stream_specs + [
            pl.BlockSpec((None, 2, da), lambda s: (mixed(s) // nt, 0, 0)),
            pl.BlockSpec((None, nh, da), lambda s: (mixed(s) // nt, 0, 0))],
        out_shape=_stream_shapes(total_rows, d) + [
            jax.ShapeDtypeStruct((batch, 2, da), F32),
            jax.ShapeDtypeStruct((batch, nh, da), F32)],
        scratch_shapes=[pltpu.VMEM((POOL_HALO + tt, da), F32),
                        pltpu.VMEM((POOL_HALO + tt, da), F32),
                        pltpu.VMEM((tt, d), BF16),
                        pltpu.VMEM((tt, d), BF16)],
        compiler_params=_params("arbitrary"),
        name="even_mixer_prompt",
    )(proj, proj, proj, proj, proj, proj, proj,
      w_conv, w_pool, pool_scale.reshape(pool_scale.shape[0], 1, da), wob, x)


def _odd_prompt_kernel(nt, ntiles,
                       u_ref, v_ref, a_ref, g_ref, ha_ref, hg_ref,
                       vng_ref, vnb_ref, ws_ref, bst_ref,
                       wconv_ref, bconv_ref, cng_ref, cnb_ref, wob_ref, xres_ref,
                       x_out, xb_out, ssq_out, chunk_v_ref, conf_state_ref,
                       glu_ext, shifted, cat_a, cat_b):
    tt, dc = u_ref.shape
    h = CONF_HALO
    i = jnp.minimum(pl.program_id(0), ntiles - 1) % nt
    live = (i > 0).astype(F32)

    nheads = ws_ref.shape[0]
    hd = dc // nheads
    nw = wconv_ref.shape[0]

    def mixer_stages(cat_ref):
        vals = {}

        def gate_inputs():
            vals["u"] = jax.nn.gelu(u_ref[...].astype(F32))
            vn = _layernorm(jax.nn.gelu(v_ref[...].astype(F32)), vng_ref[...],
                            vnb_ref[...])
            chunk_v_ref[...] = vn[tt - CHUNK:tt, :]
            vals["vb"] = vn.astype(BF16)

        def gate_heads(heads):
            def run():
                row = lax.broadcasted_iota(jnp.int32, (CHUNK, CHUNK), 0)
                colm = lax.broadcasted_iota(jnp.int32, (CHUNK, CHUNK), 1)
                for hh in heads:
                    ws = jnp.where(colm <= row, ws_ref[hh], 0.0).astype(BF16)
                    bias = bst_ref[:, hh:hh + 1]
                    for c in range(tt // CHUNK):
                        rs = slice(c * CHUNK, (c + 1) * CHUNK)
                        cs = slice(hh * hd, (hh + 1) * hd)
                        mixed = _dot(ws, vals["vb"][rs, cs]) + bias
                        cat_ref[rs, cs] = (vals["u"][rs, cs] * mixed).astype(cat_ref.dtype)
            return run

        def glu_rows():
            glu = a_ref[...].astype(F32) * jax.nn.sigmoid(g_ref[...].astype(F32))
            glu_ext[0:h, :] = (ha_ref[...].astype(F32)
                               * jax.nn.sigmoid(hg_ref[...].astype(F32)) * live)
            glu_ext[h:h + tt, :] = glu
            conf_state_ref[...] = glu_ext[pl.ds(h + tt - (nw - 1), nw - 1), :]
            span = shifted.shape[1]
            for sh in range(1, SUBLANES):
                shifted[sh - 1] = glu_ext[pl.ds(sh, span), :]
            vals["acc"] = glu * wconv_ref[nw - 1:nw, :]

        def taps(ks):
            def run():
                acc = vals["acc"]
                for k in ks:
                    q, sh = divmod(h - (nw - 1) + k, SUBLANES)
                    if sh == 0:
                        tap = glu_ext[pl.ds(q * SUBLANES, tt), :]
                    else:
                        tap = shifted[sh - 1, pl.ds(q * SUBLANES, tt), :]
                    acc = acc + tap * wconv_ref[k:k + 1, :]
                vals["acc"] = acc
            return run

        def conv_norm():
            y = _layernorm(vals["acc"] + bconv_ref[...], cng_ref[...], cnb_ref[...])
            cat_ref[:, dc:2 * dc] = (y * jax.nn.sigmoid(y)).astype(cat_ref.dtype)

        return [gate_inputs, gate_heads(range(nheads)), glu_rows, taps(range(nw - 1)),
                conv_norm]

    _pipelined_tail(mixer_stages, cat_a, cat_b, wob_ref, xres_ref,
                    (x_out, xb_out, ssq_out), interleave=False)


def _odd_prompt(proj, x, v_norm_g, v_norm_b, w_spatial, b_spatial_t, w_conv, b_conv,
                conf_norm_g, conf_norm_b, wob, layer, *, batch, seq, total_rows):
    nw, dc = w_conv.shape[1:]
    nheads = w_spatial.shape[1]
    d = x.shape[1]
    tt = MIX_TILE
    nt = seq // tt
    ntiles = batch * nt
    hb = tt // CONF_HALO
    mixed, wob_spec, xres_spec, stream_specs = _tail_specs(ntiles, tt, d)

    def main(c):
        return pl.BlockSpec((tt, dc), lambda s: (mixed(s), c))

    def halo(c):
        return pl.BlockSpec((CONF_HALO, dc),
                            lambda s: (jnp.maximum(mixed(s) * hb - 1, 0), c))

    def vec():
        return pl.BlockSpec((None, 1, dc), lambda s: (layer, 0, 0))

    def as_rows(v):
        return v.reshape(v.shape[0], 1, dc)

    return pl.pallas_call(
        functools.partial(_odd_prompt_kernel, nt, ntiles),
        grid=(ntiles + 1,),
        in_specs=[main(0), main(1), main(2), main(3), halo(2), halo(3),
                  vec(), vec(),
                  pl.BlockSpec((None, nheads, CHUNK, CHUNK), lambda s: (layer, 0, 0, 0)),
                  pl.BlockSpec((None, CHUNK, nheads), lambda s: (layer, 0, 0)),
                  pl.BlockSpec((None, nw, dc), lambda s: (layer, 0, 0)),
                  vec(), vec(), vec(), wob_spec, xres_spec],
        out_specs=stream_specs + [
            pl.BlockSpec((None, CHUNK, dc), lambda s: (mixed(s) // nt, 0, 0)),
            pl.BlockSpec((None, nw - 1, dc), lambda s: (mixed(s) // nt, 0, 0))],
        out_shape=_stream_shapes(total_rows, d) + [
            jax.ShapeDtypeStruct((batch, CHUNK, dc), F32),
            jax.ShapeDtypeStruct((batch, nw - 1, dc), F32)],
        scratch_shapes=[pltpu.VMEM((CONF_HALO + tt, dc), F32),
                        pltpu.VMEM((SUBLANES - 1, CONF_HALO - SUBLANES + tt, dc), F32),
                        pltpu.VMEM((tt, d), BF16),
                        pltpu.VMEM((tt, d), BF16)],
        compiler_params=_params("arbitrary"),
        name="odd_mixer_prompt",
    )(proj, proj, proj, proj, proj, proj,
      as_rows(v_norm_g), as_rows(v_norm_b), w_spatial, b_spatial_t,
      w_conv, as_rows(b_conv), as_rows(conf_norm_g), as_rows(conf_norm_b), wob, x)


def _push_copies(hist_ref, new_any, layer, new_row, sems, slot):
    k = hist_ref.shape[1]
    shift = pltpu.make_async_copy(hist_ref.at[0, pl.ds(1, k - 1)],
                                  new_any.at[layer, pl.ds(0, k - 1)], sems.at[slot])
    last = pltpu.make_async_copy(new_row, new_any.at[layer, k - 1], sems.at[slot + 1])
    return shift, last


def _even_sample_kernel(layer, nprev,
                        xin_ref, gpre_ref, gpost_ref, p_ref,
                        conv_hist_ref, pool_hist_ref,
                        wconv_ref, wpool_ref, scale_ref, wob_ref, xres_ref,
                        x_in, xb_in, ssq_in, *refs):
    del x_in, xb_in, ssq_in
    x_out, xb_out, ssq_out, new_conv, new_pool, cat_ref, gx_row, p_row, sems = refs[nprev:]
    s, da = xin_ref.shape
    conv_shift, conv_last = _push_copies(conv_hist_ref, new_conv, layer, gx_row, sems, 0)
    pool_shift, pool_last = _push_copies(pool_hist_ref, new_pool, layer, p_row, sems, 2)
    conv_shift.start()
    pool_shift.start()

    gx = gpre_ref[...].astype(F32) * xin_ref[...].astype(F32)
    gx_row[...] = gx
    conv_last.start()
    nc = conv_hist_ref.shape[1]
    conv = gx * wconv_ref[nc:nc + 1, :]
    for k in range(nc):
        conv = conv + conv_hist_ref[0, k] * wconv_ref[k:k + 1, :]
    cat_ref[:, 0:da] = (gpost_ref[...].astype(F32) * conv).astype(cat_ref.dtype)

    p = p_ref[...].astype(F32)
    p_row[...] = p
    pool_last.start()
    nh = pool_hist_ref.shape[1]
    gw = da // len(POOL_WINDOWS)
    pooled = []
    for g, w in enumerate(POOL_WINDOWS):
        sl = slice(g * gw, (g + 1) * gw)
        acc = p[:, sl]
        for back in range(1, w):
            acc = acc + pool_hist_ref[0, nh - back, :, sl]
        cnt = float(min(w, PAST_LEN + 1))
        pooled.append(acc / cnt - p[:, sl])
    pooled = jnp.concatenate(pooled, axis=-1)
    _pool_project(pooled, wpool_ref, scale_ref, cat_ref, da)
    o = xres_ref[...] + _dot(cat_ref[...], wob_ref[...])
    _emit_stream(o, x_out, xb_out, ssq_out)
    conv_shift.wait()
    pool_shift.wait()
    conv_last.wait()
    pool_last.wait()


def _sample_stream_specs(s, d, xres_block, out_block):
    wob = pl.BlockSpec((d, d), lambda i: (0, 0), pipeline_mode=pl.Buffered(1))
    xres = pl.BlockSpec((s, d), lambda i: (xres_block, 0))
    anys = [pl.BlockSpec(memory_space=pl.ANY)] * 3
    outs = [pl.BlockSpec((s, d), lambda i: (out_block, 0)),
            pl.BlockSpec((s, d), lambda i: (out_block, 0)),
            pl.BlockSpec((s, LANES), lambda i: (out_block, 0))]
    return wob, xres, anys, outs


def _history_block(hist_t, layer):
    return pl.BlockSpec((1,) + hist_t.shape[1:], lambda i: (layer, 0, 0, 0),
                        pipeline_mode=pl.Buffered(1))


def _even_sample(proj, xres, xres_block, stream, conv_hist_t, pool_hist_t, new_states,
                 w_conv, w_pool, pool_scale, wob, layer, *, row0):
    nc, s, da = conv_hist_t.shape[1:]
    total_rows, d = stream[0].shape
    rb = row0 // s
    nwin = len(POOL_WINDOWS)
    wob_spec, xres_spec, any_specs, stream_specs = _sample_stream_specs(
        s, d, xres_block, rb)
    nprev = len(new_states)
    any_spec = pl.BlockSpec(memory_space=pl.ANY)

    def seg(c):
        return pl.BlockSpec((s, da), lambda i: (rb, c))

    in_specs = [seg(0), seg(1), seg(2), seg(3),
                _history_block(conv_hist_t, layer), _history_block(pool_hist_t, layer),
                pl.BlockSpec((None, nc + 1, da), lambda i: (layer, 0, 0)),
                pl.BlockSpec((None, nwin, da // nwin, da // nwin),
                             lambda i: (layer, 0, 0, 0)),
                pl.BlockSpec((None, 1, da), lambda i: (layer, 0, 0)),
                wob_spec, xres_spec] + any_specs + [any_spec] * nprev
    first_prev = len(in_specs) - nprev
    aliases = {11: 0, 12: 1, 13: 2}
    aliases.update({first_prev + k: 3 + k for k in range(nprev)})
    return pl.pallas_call(
        functools.partial(_even_sample_kernel, layer, nprev),
        grid=(1,),
        in_specs=in_specs,
        out_specs=stream_specs + [any_spec, any_spec],
        out_shape=_stream_shapes(total_rows, d) + [
            jax.ShapeDtypeStruct(conv_hist_t.shape, F32),
            jax.ShapeDtypeStruct(pool_hist_t.shape, F32)],
        scratch_shapes=[pltpu.VMEM((s, d), BF16), pltpu.VMEM((s, da), F32),
                        pltpu.VMEM((s, da), F32), pltpu.SemaphoreType.DMA((4,))],
        input_output_aliases=aliases,
        compiler_params=_params("arbitrary"),
        name="even_mixer_sample",
    )(proj, proj, proj, proj, conv_hist_t, pool_hist_t,
      w_conv, w_pool, pool_scale.reshape(pool_scale.shape[0], 1, da), wob, xres, *stream,
      *new_states)


def _odd_sample_kernel(layer, nprev,
                       u_ref, v_ref, a_ref, g_ref, hist_ref,
                       vng_ref, vnb_ref, ws0_ref, b0_ref,
                       wconv_ref, bconv_ref, cng_ref, cnb_ref, wob_ref, xres_ref,
                       x_in, xb_in, ssq_in, *refs):
    del x_in, xb_in, ssq_in
    x_out, xb_out, ssq_out, vn_ref, new_hist, cat_ref, glu_row, sems = refs[nprev:]
    s, dc = u_ref.shape
    shift, last = _push_copies(hist_ref, new_hist, layer, glu_row, sems, 0)
    shift.start()
    glu = a_ref[...].astype(F32) * jax.nn.sigmoid(g_ref[...].astype(F32))
    glu_row[...] = glu
    last.start()

    u = jax.nn.gelu(u_ref[...].astype(F32))
    vn = _layernorm(jax.nn.gelu(v_ref[...].astype(F32)), vng_ref[...], vnb_ref[...])
    vn_ref[...] = vn
    mixed = ws0_ref[...] * vn + b0_ref[...]
    cat_ref[:, 0:dc] = (u * mixed).astype(cat_ref.dtype)

    nh = hist_ref.shape[1]
    acc = glu * wconv_ref[nh:nh + 1, :]
    for k in range(nh):
        acc = acc + hist_ref[0, k] * wconv_ref[k:k + 1, :]
    y = _layernorm(acc + bconv_ref[...], cng_ref[...], cnb_ref[...])
    cat_ref[:, dc:2 * dc] = (y * jax.nn.sigmoid(y)).astype(cat_ref.dtype)
    o = xres_ref[...] + _dot(cat_ref[...], wob_ref[...])
    _emit_stream(o, x_out, xb_out, ssq_out)
    shift.wait()
    last.wait()


def _odd_sample(proj, xres, xres_block, stream, hist_t, new_states, v_norm_g, v_norm_b,
                w_spatial, b_spatial, w_conv, b_conv, conf_norm_g, conf_norm_b, wob, layer,
                *, row0):
    nh, s, dc = hist_t.shape[1:]
    nheads = w_spatial.shape[1]
    hd = dc // nheads
    total_rows, d = stream[0].shape
    rb = row0 // s
    ws0 = jnp.repeat(w_spatial[layer, :, 0, 0], hd).reshape(1, dc)
    b0 = jnp.repeat(b_spatial[layer, :, 0], hd).reshape(1, dc)
    wob_spec, xres_spec, any_specs, stream_specs = _sample_stream_specs(
        s, d, xres_block, rb)
    nprev = len(new_states)
    any_spec = pl.BlockSpec(memory_space=pl.ANY)

    def seg(c):
        return pl.BlockSpec((s, dc), lambda i: (rb, c))

    def vec():
        return pl.BlockSpec((None, 1, dc), lambda i: (layer, 0, 0))

    def row():
        return pl.BlockSpec((1, dc), lambda i: (0, 0))

    def as_rows(v):
        return v.reshape(v.shape[0], 1, dc)

    in_specs = [seg(0), seg(1), seg(2), seg(3), _history_block(hist_t, layer),
                vec(), vec(), row(), row(),
                pl.BlockSpec((None, nh + 1, dc), lambda i: (layer, 0, 0)),
                vec(), vec(), vec(), wob_spec, xres_spec] + any_specs + (
                    [any_spec] * nprev)
    first_prev = len(in_specs) - nprev
    aliases = {15: 0, 16: 1, 17: 2}
    aliases.update({first_prev + k: 4 + k for k in range(nprev)})
    return pl.pallas_call(
        functools.partial(_odd_sample_kernel, layer, nprev),
        grid=(1,),
        in_specs=in_specs,
        out_specs=stream_specs + [pl.BlockSpec((s, dc), lambda i: (0, 0)), any_spec],
        out_shape=_stream_shapes(total_rows, d) + [
            jax.ShapeDtypeStruct((s, dc), F32),
            jax.ShapeDtypeStruct(hist_t.shape, F32)],
        scratch_shapes=[pltpu.VMEM((s, d), BF16), pltpu.VMEM((s, dc), F32),
                        pltpu.SemaphoreType.DMA((2,))],
        input_output_aliases=aliases,
        compiler_params=_params("arbitrary"),
        name="odd_mixer_sample",
    )(proj, proj, proj, proj, hist_t,
      as_rows(v_norm_g), as_rows(v_norm_b), ws0, b0,
      w_conv, as_rows(b_conv), as_rows(conf_norm_g), as_rows(conf_norm_b), wob, xres,
      *stream, *new_states)


def kernel(x_prompt, x_sample, state_conv_a, state_pool, state_conformer, norm_mix, norm_ffn, w_in_even, w_conv_a, w_pool, pool_scale, w_out_even, w_in_odd, v_norm_g, v_norm_b, w_spatial, b_spatial, w_conv_d, b_conv_d, conf_norm_g, conf_norm_b, w_out_odd, w_ffn_gate, w_ffn_up, w_ffn_down, norm_final):
    batch, seq, d = x_prompt.shape
    ns = x_sample.shape[0]
    mp = batch * seq
    m = mp + ns
    depth = norm_mix.shape[0]
    assert x_sample.shape[1] == 1 and mp % ns == 0 and mp % PROMPT_ROWS == 0
    assert m % WIDE_ROW_TILE == 0 and m % DOWN_ROW_TILE == 0
    assert seq % MIX_TILE == 0 and MIX_TILE % CHUNK == 0 and seq % EVEN_MIX_TILE == 0

    x_prompt2d = x_prompt.reshape(mp, d)
    x_sample2d = x_sample.reshape(ns, d)
    entry = _prep(x_prompt2d, rows=PROMPT_ROWS, block0=0, total_rows=m)
    xb, ssq = _prep(x_sample2d, rows=ns, block0=mp // ns, total_rows=m, prev=entry)
    xres_p, xres_s, xres_s_block = x_prompt2d, x_sample2d, 0
    b_spatial_t = jnp.swapaxes(b_spatial, 1, 2)
    conv_hist_t = jnp.swapaxes(state_conv_a, 1, 2)
    pool_hist_t = jnp.swapaxes(state_pool, 1, 2)
    conf_hist_t = jnp.swapaxes(state_conformer, 1, 2)
    even_new, odd_new = [], []
    conv_p, pool_p, chunk_p, chunk_s, conf_p = [], [], [], [], []
    w_outs = [(w_out_even if l % 2 == 0 else w_out_odd, l // 2) for l in range(depth)]
    wob = _cast_bf16(*w_outs[0])
    for l in range(depth):
        i = l // 2
        if l % 2 == 0:
            proj = _proj(xb, ssq, norm_mix[l], w_in_even, i)
            *stream, cs_p, ps_p = _even_prompt(proj, xres_p, w_conv_a, w_pool, pool_scale,
                                               wob, i, batch=batch, seq=seq, total_rows=m)
            x, xb, ssq, *even_new = _even_sample(
                proj, xres_s, xres_s_block, stream, conv_hist_t, pool_hist_t, even_new,
                w_conv_a, w_pool, pool_scale, wob, i, row0=mp)
            conv_p.append(cs_p)
            pool_p.append(ps_p)
        else:
            proj = _proj(xb, ssq, norm_mix[l], w_in_odd, i)
            *stream, cv_p, cf_p = _odd_prompt(
                proj, xres_p, v_norm_g, v_norm_b, w_spatial, b_spatial_t, w_conv_d, b_conv_d,
                conf_norm_g, conf_norm_b, wob, i, batch=batch, seq=seq, total_rows=m)
            x, xb, ssq, vn_s, *odd_new = _odd_sample(
                proj, xres_s, xres_s_block, stream, conf_hist_t, odd_new, v_norm_g, v_norm_b,
                w_spatial, b_spatial, w_conv_d, b_conv_d, conf_norm_g, conf_norm_b, wob, i,
                row0=mp)
            chunk_p.append(cv_p)
            conf_p.append(cf_p)
            chunk_s.append(vn_s[:, None, :])
        last = l == depth - 1
        h, wd_tiles, *next_wob = _gate_up(xb, ssq, norm_ffn[l], w_ffn_gate, w_ffn_up,
                                          w_ffn_down, l, None if last else w_outs[l + 1])
        x, *rest = _matmul_res(h, wd_tiles, x, tm=DOWN_ROW_TILE, name="ffn_down",
                               emit_stream=not last)
        if not last:
            (wob,), (xb, ssq) = next_wob, rest
        xres_p, xres_s, xres_s_block = x, x, mp // ns
    y_p = _final_norm(x, norm_final, rows=PROMPT_ROWS, block0=0, nblocks=mp // PROMPT_ROWS)
    y_s = _final_norm(x, norm_final, rows=ns, block0=mp // ns, nblocks=1)
    conv_s, pool_s = (jnp.swapaxes(t, 1, 2) for t in even_new)
    (conf_s,) = (jnp.swapaxes(t, 1, 2) for t in odd_new)
    return (y_p.reshape(batch, seq, d), y_s.reshape(ns, 1, d),
            jnp.stack(conv_p), conv_s, jnp.stack(pool_p), pool_s,
            jnp.stack(chunk_p), jnp.stack(chunk_s), jnp.stack(conf_p), conf_s)
```

```python
import functools

import jax
import jax.numpy as jnp
from jax import lax
from jax.experimental import pallas as pl
from jax.experimental.pallas import tpu as pltpu

F32 = jnp.float32
BF16 = jnp.bfloat16

EPS = 1e-6
PAST_LEN = 16384
POOL_WINDOWS = (2, 4, 8, 16)
CHUNK = 128

LANES = 128
SUBLANES = 8
VMEM_LIMIT_BYTES = 56 * 1024 * 1024

WIDE_ROW_TILE = 4160
DOWN_ROW_TILE = 1040
PROMPT_ROWS = 1024
MIX_TILE = 256
EVEN_MIX_TILE = 256
POOL_HALO = 16
CONF_HALO = 32


def _params(*sem):
    return pltpu.CompilerParams(dimension_semantics=sem,
                                vmem_limit_bytes=VMEM_LIMIT_BYTES)


def _rmsnorm(x, g):
    ms = jnp.mean(x * x, axis=-1, keepdims=True)
    return x * lax.rsqrt(ms + EPS) * g


def _layernorm(x, g, b):
    mu = jnp.mean(x, axis=-1, keepdims=True)
    xc = x - mu
    var = jnp.mean(xc * xc, axis=-1, keepdims=True)
    return xc * lax.rsqrt(var + EPS) * g + b


def _dot(a, b):
    return jnp.dot(a, b, preferred_element_type=F32)


def _lane_chunks(n):
    return [slice(c * LANES, (c + 1) * LANES) for c in range(n // LANES)]


def _gain_columns(g):
    return jnp.broadcast_to(g[:, None], (g.shape[0], LANES))


def _emit_stream(o, x_out, xb_out, ssq_out):
    x_out[...] = o
    xb_out[...] = o.astype(BF16)
    ssq_out[...] = jnp.broadcast_to(jnp.sum(o * o, axis=-1, keepdims=True),
                                    ssq_out.shape)


def _stream_shapes(total_rows, d):
    return [jax.ShapeDtypeStruct((total_rows, d), F32),
            jax.ShapeDtypeStruct((total_rows, d), BF16),
            jax.ShapeDtypeStruct((total_rows, LANES), F32)]


def _prep_kernel(x_ref, *refs):
    xb_out, ssq_out = refs[-2:]
    x = x_ref[...]
    xb_out[...] = x.astype(BF16)
    ssq_out[...] = jnp.broadcast_to(jnp.sum(x * x, axis=-1, keepdims=True),
                                    ssq_out.shape)


def _prep(x2d, *, rows, block0, total_rows, prev=None):
    n, d = x2d.shape
    in_specs = [pl.BlockSpec((rows, d), lambda i: (i, 0))]
    args = [x2d]
    aliases = {}
    if prev is not None:
        in_specs += [pl.BlockSpec(memory_space=pl.ANY)] * 2
        args += list(prev)
        aliases = {1: 0, 2: 1}
    return pl.pallas_call(
        _prep_kernel,
        grid=(n // rows,),
        in_specs=in_specs,
        out_specs=[pl.BlockSpec((rows, d), lambda i: (block0 + i, 0)),
                   pl.BlockSpec((rows, LANES), lambda i: (block0 + i, 0))],
        out_shape=_stream_shapes(total_rows, d)[1:],
        input_output_aliases=aliases,
        compiler_params=_params("arbitrary"),
        name="stream_entry",
    )(*args)


def _resident_rows(tm, k):
    return pl.BlockSpec((tm, k), lambda i, j: (i, 0), pipeline_mode=pl.Buffered(1))


def _scaled_weight(w_ref, g_ref, wb_ref):
    for sl in _lane_chunks(w_ref.shape[1]):
        wb_ref[:, sl] = (w_ref[:, sl] * g_ref[...]).astype(BF16)


def _row_scale(ssq_ref, d):
    return lax.rsqrt(ssq_ref[...] / d + EPS)


def _proj_kernel(xb_ref, ssq_ref, g_ref, w_ref, o_ref, wb_ref):
    _scaled_weight(w_ref, g_ref, wb_ref)
    r = _row_scale(ssq_ref, xb_ref.shape[1])
    acc = _dot(xb_ref[...], wb_ref[...])
    for sl in _lane_chunks(o_ref.shape[1]):
        o_ref[:, sl] = (acc[:, sl] * r).astype(o_ref.dtype)


def _proj(xb, ssq, g, w_stack, layer, *, tn=512):
    m, d = xb.shape
    n = w_stack.shape[2]
    return pl.pallas_call(
        _proj_kernel,
        grid=(m // WIDE_ROW_TILE, n // tn),
        in_specs=[_resident_rows(WIDE_ROW_TILE, d),
                  pl.BlockSpec((WIDE_ROW_TILE, LANES), lambda i, j: (i, 0)),
                  pl.BlockSpec((d, LANES), lambda i, j: (0, 0)),
                  pl.BlockSpec((None, d, tn), lambda i, j: (layer, 0, j))],
        out_specs=pl.BlockSpec((WIDE_ROW_TILE, tn), lambda i, j: (i, j)),
        out_shape=jax.ShapeDtypeStruct((m, n), BF16),
        scratch_shapes=[pltpu.VMEM((d, tn), BF16)],
        compiler_params=_params("parallel", "arbitrary"),
        name="mixer_in_proj",
    )(xb, ssq, _gain_columns(g), w_stack)


DOWN_COLS = 512
CAST_ROWS = 128


def _gate_up_kernel(cast_next, xb_ref, ssq_ref, g_ref, wg_ref, wu_ref, wd_ref, *refs):
    if cast_next:
        wo_ref, h_ref, wdb_ref, wob_ref, wgb_ref, wub_ref = refs
    else:
        h_ref, wdb_ref, wgb_ref, wub_ref = refs
    _scaled_weight(wg_ref, g_ref, wgb_ref)
    _scaled_weight(wu_ref, g_ref, wub_ref)
    r = _row_scale(ssq_ref, xb_ref.shape[1])
    xb = xb_ref[...]
    gate = _dot(xb, wgb_ref[...])
    up = _dot(xb, wub_ref[...])
    for sl in _lane_chunks(h_ref.shape[1]):
        gt = gate[:, sl] * r
        h_ref[:, sl] = (gt * jax.nn.sigmoid(gt) * (up[:, sl] * r)).astype(BF16)
    @pl.when(pl.program_id(0) == 0)
    def _():
        for c in range(wdb_ref.shape[0]):
            wdb_ref[c] = wd_ref[:, c * DOWN_COLS:(c + 1) * DOWN_COLS].astype(BF16)

    if cast_next:
        @pl.when((pl.program_id(0) == 0) & (pl.program_id(1) < cast_next))
        def _():
            wob_ref[...] = wo_ref[...].astype(BF16)


def _gate_up(xb, ssq, g, wg_stack, wu_stack, wd_stack, layer, next_out=None, *, tn=256):
    m, d = xb.shape
    f = wg_stack.shape[2]
    n_down = wd_stack.shape[2]

    def wspec():
        return pl.BlockSpec((None, d, tn), lambda i, j: (layer, 0, j))

    def parked(i, j, nblocks):
        return jnp.where((i == 0) & (j < nblocks), j, nblocks - 1)

    in_specs = [_resident_rows(WIDE_ROW_TILE, d),
                pl.BlockSpec((WIDE_ROW_TILE, LANES), lambda i, j: (i, 0)),
                pl.BlockSpec((d, LANES), lambda i, j: (0, 0)),
                wspec(), wspec(),
                pl.BlockSpec((None, tn, n_down),
                             lambda i, j: (layer, parked(i, j, f // tn), 0))]
    out_specs = [pl.BlockSpec((WIDE_ROW_TILE, tn), lambda i, j: (i, j)),
                 pl.BlockSpec((n_down // DOWN_COLS, tn, DOWN_COLS),
                              lambda i, j: (0, parked(i, j, f // tn), 0))]
    out_shape = [jax.ShapeDtypeStruct((m, f), BF16),
                 jax.ShapeDtypeStruct((n_down // DOWN_COLS, f, DOWN_COLS), BF16)]
    args = [xb, ssq, _gain_columns(g), wg_stack, wu_stack, wd_stack]
    cast_blocks = 0
    if next_out is not None:
        wo_stack, wo_layer = next_out
        ko, no = wo_stack.shape[1:]
        assert ko % CAST_ROWS == 0 and ko // CAST_ROWS <= f // tn
        cast_blocks = ko // CAST_ROWS
        in_specs.append(pl.BlockSpec(
            (None, CAST_ROWS, no), lambda i, j: (wo_layer, parked(i, j, cast_blocks), 0)))
        out_specs.append(pl.BlockSpec((CAST_ROWS, no),
                                      lambda i, j: (parked(i, j, cast_blocks), 0)))
        out_shape.append(jax.ShapeDtypeStruct((ko, no), BF16))
        args.append(wo_stack)
    return pl.pallas_call(
        functools.partial(_gate_up_kernel, cast_blocks),
        grid=(m // WIDE_ROW_TILE, f // tn),
        in_specs=in_specs,
        out_specs=out_specs,
        out_shape=out_shape,
        scratch_shapes=[pltpu.VMEM((d, tn), BF16), pltpu.VMEM((d, tn), BF16)],
        compiler_params=_params("arbitrary", "arbitrary"),
        name="ffn_gate_up",
    )(*args)


def _matmul_res_kernel(a_ref, w_ref, r_ref, o_ref, *stream_refs):
    o = r_ref[...] + _dot(a_ref[...], w_ref[...])
    o_ref[...] = o
    if not stream_refs:
        return
    ob_ref, ssq_ref = stream_refs
    ob_ref[...] = o.astype(BF16)
    part = jnp.broadcast_to(jnp.sum(o * o, axis=-1, keepdims=True), ssq_ref.shape)

    @pl.when(pl.program_id(1) == 0)
    def _():
        ssq_ref[...] = part

    @pl.when(pl.program_id(1) > 0)
    def _():
        ssq_ref[...] += part


def _matmul_res(a, w_tiles, r, *, tm, name, emit_stream=True):
    m, k = a.shape
    ntiles, _, tn = w_tiles.shape
    n = ntiles * tn
    nouts = 3 if emit_stream else 1
    return pl.pallas_call(
        _matmul_res_kernel,
        grid=(m // tm, ntiles),
        in_specs=[pl.BlockSpec((tm, k), lambda i, j: (i, 0)),
                  pl.BlockSpec((None, k, tn), lambda i, j: (j, 0, 0)),
                  pl.BlockSpec((tm, tn), lambda i, j: (i, j))],
        out_specs=[pl.BlockSpec((tm, tn), lambda i, j: (i, j)),
                   pl.BlockSpec((tm, tn), lambda i, j: (i, j)),
                   pl.BlockSpec((tm, LANES), lambda i, j: (i, 0))][:nouts],
        out_shape=_stream_shapes(m, n)[:nouts],
        compiler_params=_params("parallel", "arbitrary"),
        name=name,
    )(a, w_tiles, r)


def _cast_kernel(w_ref, o_ref):
    o_ref[...] = w_ref[...].astype(o_ref.dtype)


def _cast_bf16(w_stack, layer, *, rows=512):
    k, n = w_stack.shape[1:]
    return pl.pallas_call(
        _cast_kernel,
        grid=(k // rows,),
        in_specs=[pl.BlockSpec((None, rows, n), lambda i: (layer, i, 0))],
        out_specs=pl.BlockSpec((rows, n), lambda i: (i, 0)),
        out_shape=jax.ShapeDtypeStruct((k, n), BF16),
        compiler_params=_params("parallel"),
        name="out_proj_weight_bf16",
    )(w_stack)


def _final_norm_kernel(x_ref, g_ref, o_ref):
    o_ref[...] = _rmsnorm(x_ref[...], g_ref[...])


def _final_norm(x, g, *, rows, block0, nblocks):
    d = x.shape[1]
    return pl.pallas_call(
        _final_norm_kernel,
        grid=(nblocks,),
        in_specs=[pl.BlockSpec((rows, d), lambda i: (block0 + i, 0)),
                  pl.BlockSpec((1, d), lambda i: (0, 0))],
        out_specs=pl.BlockSpec((rows, d), lambda i: (i, 0)),
        out_shape=jax.ShapeDtypeStruct((rows * nblocks, d), F32),
        compiler_params=_params("parallel"),
        name="final_norm",
    )(x, g.reshape(1, d))


PROJ_CHUNK = 256
COL_CHUNK = 256


def _pipelined_tail(mixer_stages, cat_a, cat_b, wob_ref, xres_ref, stream_outs, *,
                    interleave):
    s = pl.program_id(0)
    x_out, xb_out, ssq_out = stream_outs
    nchunks = wob_ref.shape[1] // PROJ_CHUNK

    @pl.when(s == 0)
    def _():
        cat_b[...] = jnp.zeros_like(cat_b)

    def step(cat_w, cat_r):
        stages = mixer_stages(cat_w)
        if not interleave:
            for stage in stages:
                stage()
            o = xres_ref[...] + _dot(cat_r[...], wob_ref[...])
            _emit_stream(o, *stream_outs)
            return
        ssq = jnp.zeros((x_out.shape[0], 1), F32)
        done = 0
        for k, stage in enumerate(stages):
            upto = (k + 1) * nchunks // len(stages)
            for c in range(done, upto):
                sl = slice(c * PROJ_CHUNK, (c + 1) * PROJ_CHUNK)
                o = xres_ref[:, sl] + _dot(cat_r[...], wob_ref[:, sl])
                x_out[:, sl] = o
                xb_out[:, sl] = o.astype(BF16)
                ssq = ssq + jnp.sum(o * o, axis=-1, keepdims=True)
            done = upto
            stage()
        ssq_out[...] = jnp.broadcast_to(ssq, ssq_out.shape)

    @pl.when(s % 2 == 0)
    def _():
        step(cat_a, cat_b)

    @pl.when(s % 2 == 1)
    def _():
        step(cat_b, cat_a)


def _tail_specs(ntiles, tt, d):
    def mixed(s):
        return jnp.minimum(s, ntiles - 1)

    def projected(s):
        return jnp.maximum(s - 1, 0)

    wob = pl.BlockSpec((d, d), lambda s: (0, 0), pipeline_mode=pl.Buffered(1))
    xres = pl.BlockSpec((tt, d), lambda s: (projected(s), 0))
    outs = [pl.BlockSpec((tt, d), lambda s: (projected(s), 0)),
            pl.BlockSpec((tt, d), lambda s: (projected(s), 0)),
            pl.BlockSpec((tt, LANES), lambda s: (projected(s), 0))]
    return mixed, wob, xres, outs


def _pool_project(pooled, w_pool_ref, scale_ref, cat_ref, col0):
    gw = w_pool_ref.shape[1]
    for g in range(len(POOL_WINDOWS)):
        sl = slice(g * gw, (g + 1) * gw)
        yb = _dot(pooled[:, sl].astype(BF16), w_pool_ref[g].astype(BF16))
        cat_ref[:, col0 + g * gw:col0 + (g + 1) * gw] = (
            yb * scale_ref[:, sl]).astype(cat_ref.dtype)


def _even_prompt_kernel(nt, ntiles,
                        xin_ref, gpre_ref, gpost_ref, p_ref,
                        hxin_ref, hgpre_ref, hp_ref,
                        wconv_ref, wpool_ref, scale_ref, wob_ref, xres_ref,
                        x_out, xb_out, ssq_out, conv_state_ref, pool_state_ref,
                        gx_ext, p_ext, cat_a, cat_b):
    tt, da = xin_ref.shape
    h = POOL_HALO
    i = jnp.minimum(pl.program_id(0), ntiles - 1) % nt
    live = (i > 0).astype(F32)

    def mixer_stages(cat_ref):
        def short_conv():
            for c0 in range(0, da, COL_CHUNK):
                cs = slice(c0, c0 + COL_CHUNK)
                gx = gpre_ref[:, cs].astype(F32) * xin_ref[:, cs].astype(F32)
                gx_ext[0:h, cs] = (hgpre_ref[:, cs].astype(F32)
                                   * hxin_ref[:, cs].astype(F32) * live)
                gx_ext[h:h + tt, cs] = gx
                conv = (gx_ext[pl.ds(h - 2, tt), cs] * wconv_ref[0:1, cs]
                        + gx_ext[pl.ds(h - 1, tt), cs] * wconv_ref[1:2, cs]
                        + gx * wconv_ref[2:3, cs])
                cat_ref[:, cs] = (gpost_ref[:, cs].astype(F32) * conv).astype(cat_ref.dtype)
                p_ext[0:h, cs] = hp_ref[:, cs].astype(F32) * live
                p_ext[h:h + tt, cs] = p_ref[:, cs].astype(F32)
            conv_state_ref[...] = gx_ext[pl.ds(h + tt - 2, 2), :]
            nh = pool_state_ref.shape[0]
            pool_state_ref[...] = p_ext[pl.ds(h + tt - nh, nh), :]

        def pool_group(g, w):
            def run():
                gw = da // len(POOL_WINDOWS)
                sl = slice(g * gw, (g + 1) * gw)
                pg = p_ext[pl.ds(h, tt), sl]
                acc = pg
                for k in range(1, w):
                    acc = acc + p_ext[pl.ds(h - k, tt), sl]
                pos = i * tt + lax.broadcasted_iota(jnp.int32, (tt, 1), 0)
                cnt = jnp.minimum(w, pos + 1).astype(F32)
                pooled = acc / cnt - pg
                yb = _dot(pooled.astype(BF16), wpool_ref[g].astype(BF16))
                cat_ref[:, da + g * gw:da + (g + 1) * gw] = (
                    yb * scale_ref[:, sl]).astype(cat_ref.dtype)
            return run

        return [short_conv] + [pool_group(g, w) for g, w in enumerate(POOL_WINDOWS)]

    _pipelined_tail(mixer_stages, cat_a, cat_b, wob_ref, xres_ref,
                    (x_out, xb_out, ssq_out), interleave=True)


def _even_prompt(proj, x, w_conv, w_pool, pool_scale, wob, layer, *, batch, seq,
                 total_rows):
    da = w_conv.shape[2]
    d = x.shape[1]
    tt = EVEN_MIX_TILE
    nt = seq // tt
    ntiles = batch * nt
    hb = tt // POOL_HALO
    nwin = len(POOL_WINDOWS)
    nh = POOL_WINDOWS[-1] - 1
    mixed, wob_spec, xres_spec, stream_specs = _tail_specs(ntiles, tt, d)

    def main(c):
        return pl.BlockSpec((tt, da), lambda s: (mixed(s), c))

    def halo(c):
        return pl.BlockSpec((POOL_HALO, da),
                            lambda s: (jnp.maximum(mixed(s) * hb - 1, 0), c))

    return pl.pallas_call(
        functools.partial(_even_prompt_kernel, nt, ntiles),
        grid=(ntiles + 1,),
        in_specs=[main(0), main(1), main(2), main(3), halo(0), halo(1), halo(3),
                  pl.BlockSpec((None, 3, da), lambda s: (layer, 0, 0)),
                  pl.BlockSpec((None, nwin, da // nwin, da // nwin),
                               lambda s: (layer, 0, 0, 0)),
                  pl.BlockSpec((None, 1, da), lambda s: (layer, 0, 0)),
                  wob_spec, xres_spec],
        out_specs=stream_specs + [
            pl.BlockSpec((None, 2, da), lambda s: (mixed(s) // nt, 0, 0)),
            pl.BlockSpec((None, nh, da), lambda s: (mixed(s) // nt, 0, 0))],
        out_shape=_stream_shapes(total_rows, d) + [
            jax.ShapeDtypeStruct((batch, 2, da), F32),
            jax.ShapeDtypeStruct((batch, nh, da), F32)],
        scratch_shapes=[pltpu.VMEM((POOL_HALO + tt, da), F32),
                        pltpu.VMEM((POOL_HALO + tt, da), F32),
                        pltpu.VMEM((tt, d), BF16),
                        pltpu.VMEM((tt, d), BF16)],
        compiler_params=_params("arbitrary"),
        name="even_mixer_prompt",
    )(proj, proj, proj, proj, proj, proj, proj,
      w_conv, w_pool, pool_scale.reshape(pool_scale.shape[0], 1, da), wob, x)


def _odd_prompt_kernel(nt, ntiles,
                       u_ref, v_ref, a_ref, g_ref, ha_ref, hg_ref,
                       vng_ref, vnb_ref, ws_ref, bst_ref,
                       wconv_ref, bconv_ref, cng_ref, cnb_ref, wob_ref, xres_ref,
                       x_out, xb_out, ssq_out, chunk_v_ref, conf_state_ref,
                       glu_ext, shifted, cat_a, cat_b):
    tt, dc = u_ref.shape
    h = CONF_HALO
    i = jnp.minimum(pl.program_id(0), ntiles - 1) % nt
    live = (i > 0).astype(F32)

    nheads = ws_ref.shape[0]
    hd = dc // nheads
    nw = wconv_ref.shape[0]

    def mixer_stages(cat_ref):
        vals = {}

        def gate_inputs():
            vals["u"] = jax.nn.gelu(u_ref[...].astype(F32))
            vn = _layernorm(jax.nn.gelu(v_ref[...].astype(F32)), vng_ref[...],
                            vnb_ref[...])
            chunk_v_ref[...] = vn[tt - CHUNK:tt, :]
            vals["vb"] = vn.astype(BF16)

        def gate_heads(heads):
            def run():
                row = lax.broadcasted_iota(jnp.int32, (CHUNK, CHUNK), 0)
                colm = lax.broadcasted_iota(jnp.int32, (CHUNK, CHUNK), 1)
                for hh in heads:
                    ws = jnp.where(colm <= row, ws_ref[hh], 0.0).astype(BF16)
                    bias = bst_ref[:, hh:hh + 1]
                    for c in range(tt // CHUNK):
                        rs = slice(c * CHUNK, (c + 1) * CHUNK)
                        cs = slice(hh * hd, (hh + 1) * hd)
                        mixed = _dot(ws, vals["vb"][rs, cs]) + bias
                        cat_ref[rs, cs] = (vals["u"][rs, cs] * mixed).astype(cat_ref.dtype)
            return run

        def glu_rows():
            glu = a_ref[...].astype(F32) * jax.nn.sigmoid(g_ref[...].astype(F32))
            glu_ext[0:h, :] = (ha_ref[...].astype(F32)
                               * jax.nn.sigmoid(hg_ref[...].astype(F32)) * live)
            glu_ext[h:h + tt, :] = glu
            conf_state_ref[...] = glu_ext[pl.ds(h + tt - (nw - 1), nw - 1), :]
            span = shifted.shape[1]
            for sh in range(1, SUBLANES):
                shifted[sh - 1] = glu_ext[pl.ds(sh, span), :]

            vals["acc"] = glu * wconv_ref[nw - 1:nw, :]

        def taps(ks):
            def run():
                acc = vals["acc"]
                for k in ks:
                    q, sh = divmod(h - (nw - 1) + k, SUBLANES)
                    if sh == 0:
                        tap = glu_ext[pl.ds(q * SUBLANES, tt), :]
                    else:
                        tap = shifted[sh - 1, pl.ds(q * SUBLANES, tt), :]
                    acc = acc + tap * wconv_ref[k:k + 1, :]
                vals["acc"] = acc
            return run

        def conv_norm():
            y = _layernorm(vals["acc"] + bconv_ref[...], cng_ref[...], cnb_ref[...])
            cat_ref[:, dc:2 * dc] = (y * jax.nn.sigmoid(y)).astype(cat_ref.dtype)

        return [gate_inputs, gate_heads(range(nheads)), glu_rows, taps(range(nw - 1)),
                conv_norm]

    _pipelined_tail(mixer_stages, cat_a, cat_b, wob_ref, xres_ref,
                    (x_out, xb_out, ssq_out), interleave=False)


def _odd_prompt(proj, x, v_norm_g, v_norm_b, w_spatial, b_spatial_t, w_conv, b_conv,
                conf_norm_g, conf_norm_b, wob, layer, *, batch, seq, total_rows):
    nw, dc = w_conv.shape[1:]
    nheads = w_spatial.shape[1]
    d = x.shape[1]
    tt = MIX_TILE
    nt = seq // tt
    ntiles = batch * nt
    hb = tt // CONF_HALO
    mixed, wob_spec, xres_spec, stream_specs = _tail_specs(ntiles, tt, d)

    def main(c):
        return pl.BlockSpec((tt, dc), lambda s: (mixed(s), c))

    def halo(c):
        return pl.BlockSpec((CONF_HALO, dc),
                            lambda s: (jnp.maximum(mixed(s) * hb - 1, 0), c))

    def vec():
        return pl.BlockSpec((None, 1, dc), lambda s: (layer, 0, 0))

    def as_rows(v):
        return v.reshape(v.shape[0], 1, dc)

    return pl.pallas_call(
        functools.partial(_odd_prompt_kernel, nt, ntiles),
        grid=(ntiles + 1,),
        in_specs=[main(0), main(1), main(2), main(3), halo(2), halo(3),
                  vec(), vec(),
                  pl.BlockSpec((None, nheads, CHUNK, CHUNK), lambda s: (layer, 0, 0, 0)),
                  pl.BlockSpec((None, CHUNK, nheads), lambda s: (layer, 0, 0)),
                  pl.BlockSpec((None, nw, dc), lambda s: (layer, 0, 0)),
                  vec(), vec(), vec(), wob_spec, xres_spec],
        out_specs=stream_specs + [
            pl.BlockSpec((None, CHUNK, dc), lambda s: (mixed(s) // nt, 0, 0)),
            pl.BlockSpec((None, nw - 1, dc), lambda s: (mixed(s) // nt, 0, 0))],
        out_shape=_stream_shapes(total_rows, d) + [
            jax.ShapeDtypeStruct((batch, CHUNK, dc), F32),
            jax.ShapeDtypeStruct((batch, nw - 1, dc), F32)],
        scratch_shapes=[pltpu.VMEM((CONF_HALO + tt, dc), F32),
                        pltpu.VMEM((SUBLANES - 1, CONF_HALO - SUBLANES + tt, dc), F32),
                        pltpu.VMEM((tt, d), BF16),
                        pltpu.VMEM((tt, d), BF16)],
        compiler_params=_params("arbitrary"),
        name="odd_mixer_prompt",
    )(proj, proj, proj, proj, proj, proj,
      as_rows(v_norm_g), as_rows(v_norm_b), w_spatial, b_spatial_t,
      w_conv, as_rows(b_conv), as_rows(conf_norm_g), as_rows(conf_norm_b), wob, x)


def _push_copies(hist_ref, new_any, layer, new_row, sems, slot):
    k = hist_ref.shape[1]
    shift = pltpu.make_async_copy(hist_ref.at[0, pl.ds(1, k - 1)],
                                  new_any.at[layer, pl.ds(0, k - 1)], sems.at[slot])
    last = pltpu.make_async_copy(new_row, new_any.at[layer, k - 1], sems.at[slot + 1])
    return shift, last


def _even_sample_kernel(layer, nprev,
                        xin_ref, gpre_ref, gpost_ref, p_ref,
                        conv_hist_ref, pool_hist_ref,
                        wconv_ref, wpool_ref, scale_ref, wob_ref, xres_ref,
                        x_in, xb_in, ssq_in, *refs):
    del x_in, xb_in, ssq_in
    x_out, xb_out, ssq_out, new_conv, new_pool, cat_ref, gx_row, p_row, sems = refs[nprev:]
    s, da = xin_ref.shape
    conv_shift, conv_last = _push_copies(conv_hist_ref, new_conv, layer, gx_row, sems, 0)
    pool_shift, pool_last = _push_copies(pool_hist_ref, new_pool, layer, p_row, sems, 2)
    conv_shift.start()
    pool_shift.start()

    gx = gpre_ref[...].astype(F32) * xin_ref[...].astype(F32)
    gx_row[...] = gx
    conv_last.start()
    nc = conv_hist_ref.shape[1]
    conv = gx * wconv_ref[nc:nc + 1, :]
    for k in range(nc):
        conv = conv + conv_hist_ref[0, k] * wconv_ref[k:k + 1, :]
    cat_ref[:, 0:da] = (gpost_ref[...].astype(F32) * conv).astype(cat_ref.dtype)

    p = p_ref[...].astype(F32)
    p_row[...] = p
    pool_last.start()
    nh = pool_hist_ref.shape[1]
    gw = da // len(POOL_WINDOWS)
    pooled = []
    for g, w in enumerate(POOL_WINDOWS):
        sl = slice(g * gw, (g + 1) * gw)
        acc = p[:, sl]
        for back in range(1, w):
            acc = acc + pool_hist_ref[0, nh - back, :, sl]
        cnt = float(min(w, PAST_LEN + 1))
        pooled.append(acc / cnt - p[:, sl])
    pooled = jnp.concatenate(pooled, axis=-1)
    _pool_project(pooled, wpool_ref, scale_ref, cat_ref, da)
    o = xres_ref[...] + _dot(cat_ref[...], wob_ref[...])
    _emit_stream(o, x_out, xb_out, ssq_out)
    conv_shift.wait()
    pool_shift.wait()
    conv_last.wait()
    pool_last.wait()


def _sample_stream_specs(s, d, xres_block, out_block):
    wob = pl.BlockSpec((d, d), lambda i: (0, 0), pipeline_mode=pl.Buffered(1))
    xres = pl.BlockSpec((s, d), lambda i: (xres_block, 0))
    anys = [pl.BlockSpec(memory_space=pl.ANY)] * 3
    outs = [pl.BlockSpec((s, d), lambda i: (out_block, 0)),
            pl.BlockSpec((s, d), lambda i: (out_block, 0)),
            pl.BlockSpec((s, LANES), lambda i: (out_block, 0))]
    return wob, xres, anys, outs


def _history_block(hist_t, layer):
    return pl.BlockSpec((1,) + hist_t.shape[1:], lambda i: (layer, 0, 0, 0),
                        pipeline_mode=pl.Buffered(1))


def _even_sample(proj, xres, xres_block, stream, conv_hist_t, pool_hist_t, new_states,
                 w_conv, w_pool, pool_scale, wob, layer, *, row0):
    nc, s, da = conv_hist_t.shape[1:]
    total_rows, d = stream[0].shape
    rb = row0 // s
    nwin = len(POOL_WINDOWS)
    wob_spec, xres_spec, any_specs, stream_specs = _sample_stream_specs(
        s, d, xres_block, rb)
    nprev = len(new_states)
    any_spec = pl.BlockSpec(memory_space=pl.ANY)

    def seg(c):
        return pl.BlockSpec((s, da), lambda i: (rb, c))

    in_specs = [seg(0), seg(1), seg(2), seg(3),
                _history_block(conv_hist_t, layer), _history_block(pool_hist_t, layer),
                pl.BlockSpec((None, nc + 1, da), lambda i: (layer, 0, 0)),
                pl.BlockSpec((None, nwin, da // nwin, da // nwin),
                             lambda i: (layer, 0, 0, 0)),
                pl.BlockSpec((None, 1, da), lambda i: (layer, 0, 0)),
                wob_spec, xres_spec] + any_specs + [any_spec] * nprev
    first_prev = len(in_specs) - nprev
    aliases = {11: 0, 12: 1, 13: 2}
    aliases.update({first_prev + k: 3 + k for k in range(nprev)})
    return pl.pallas_call(
        functools.partial(_even_sample_kernel, layer, nprev),
        grid=(1,),
        in_specs=in_specs,
        out_specs=stream_specs + [any_spec, any_spec],
        out_shape=_stream_shapes(total_rows, d) + [
            jax.ShapeDtypeStruct(conv_hist_t.shape, F32),
            jax.ShapeDtypeStruct(pool_hist_t.shape, F32)],
        scratch_shapes=[pltpu.VMEM((s, d), BF16), pltpu.VMEM((s, da), F32),
                        pltpu.VMEM((s, da), F32), pltpu.SemaphoreType.DMA((4,))],
        input_output_aliases=aliases,
        compiler_params=_params("arbitrary"),
        name="even_mixer_sample",
    )(proj, proj, proj, proj, conv_hist_t, pool_hist_t,
      w_conv, w_pool, pool_scale.reshape(pool_scale.shape[0], 1, da), wob, xres, *stream,
      *new_states)


def _odd_sample_kernel(layer, nprev,
                       u_ref, v_ref, a_ref, g_ref, hist_ref,
                       vng_ref, vnb_ref, ws0_ref, b0_ref,
                       wconv_ref, bconv_ref, cng_ref, cnb_ref, wob_ref, xres_ref,
                       x_in, xb_in, ssq_in, *refs):
    del x_in, xb_in, ssq_in
    x_out, xb_out, ssq_out, vn_ref, new_hist, cat_ref, glu_row, sems = refs[nprev:]
    s, dc = u_ref.shape
    shift, last = _push_copies(hist_ref, new_hist, layer, glu_row, sems, 0)
    shift.start()
    glu = a_ref[...].astype(F32) * jax.nn.sigmoid(g_ref[...].astype(F32))
    glu_row[...] = glu
    last.start()

    u = jax.nn.gelu(u_ref[...].astype(F32))
    vn = _layernorm(jax.nn.gelu(v_ref[...].astype(F32)), vng_ref[...], vnb_ref[...])
    vn_ref[...] = vn
    mixed = ws0_ref[...] * vn + b0_ref[...]
    cat_ref[:, 0:dc] = (u * mixed).astype(cat_ref.dtype)

    nh = hist_ref.shape[1]
    acc = glu * wconv_ref[nh:nh + 1, :]
    for k in range(nh):
        acc = acc + hist_ref[0, k] * wconv_ref[k:k + 1, :]
    y = _layernorm(acc + bconv_ref[...], cng_ref[...], cnb_ref[...])
    cat_ref[:, dc:2 * dc] = (y * jax.nn.sigmoid(y)).astype(cat_ref.dtype)
    o = xres_ref[...] + _dot(cat_ref[...], wob_ref[...])
    _emit_stream(o, x_out, xb_out, ssq_out)
    shift.wait()
    last.wait()


def _odd_sample(proj, xres, xres_block, stream, hist_t, new_states, v_norm_g, v_norm_b,
                w_spatial, b_spatial, w_conv, b_conv, conf_norm_g, conf_norm_b, wob, layer,
                *, row0):
    nh, s, dc = hist_t.shape[1:]
    nheads = w_spatial.shape[1]
    hd = dc // nheads
    total_rows, d = stream[0].shape
    rb = row0 // s
    ws0 = jnp.repeat(w_spatial[layer, :, 0, 0], hd).reshape(1, dc)
    b0 = jnp.repeat(b_spatial[layer, :, 0], hd).reshape(1, dc)
    wob_spec, xres_spec, any_specs, stream_specs = _sample_stream_specs(
        s, d, xres_block, rb)
    nprev = len(new_states)
    any_spec = pl.BlockSpec(memory_space=pl.ANY)

    def seg(c):
        return pl.BlockSpec((s, dc), lambda i: (rb, c))

    def vec():
        return pl.BlockSpec((None, 1, dc), lambda i: (layer, 0, 0))

    def row():
        return pl.BlockSpec((1, dc), lambda i: (0, 0))

    def as_rows(v):
        return v.reshape(v.shape[0], 1, dc)

    in_specs = [seg(0), seg(1), seg(2), seg(3), _history_block(hist_t, layer),
                vec(), vec(), row(), row(),
                pl.BlockSpec((None, nh + 1, dc), lambda i: (layer, 0, 0)),
                vec(), vec(), vec(), wob_spec, xres_spec] + any_specs + (
                    [any_spec] * nprev)
    first_prev = len(in_specs) - nprev
    aliases = {15: 0, 16: 1, 17: 2}
    aliases.update({first_prev + k: 4 + k for k in range(nprev)})
    return pl.pallas_call(
        functools.partial(_odd_sample_kernel, layer, nprev),
        grid=(1,),
        in_specs=in_specs,
        out_specs=stream_specs + [pl.BlockSpec((s, dc), lambda i: (0, 0)), any_spec],
        out_shape=_stream_shapes(total_rows, d) + [
            jax.ShapeDtypeStruct((s, dc), F32),
            jax.ShapeDtypeStruct(hist_t.shape, F32)],
        scratch_shapes=[pltpu.VMEM((s, d), BF16), pltpu.VMEM((s, dc), F32),
                        pltpu.SemaphoreType.DMA((2,))],
        input_output_aliases=aliases,
        compiler_params=_params("arbitrary"),
        name="odd_mixer_sample",
    )(proj, proj, proj, proj, hist_t,
      as_rows(v_norm_g), as_rows(v_norm_b), ws0, b0,
      w_conv, as_rows(b_conv), as_rows(conf_norm_g), as_rows(conf_norm_b), wob, xres,
      *stream, *new_states)


def kernel(x_prompt, x_sample, state_conv_a, state_pool, state_conformer, norm_mix, norm_ffn, w_in_even, w_conv_a, w_pool, pool_scale, w_out_even, w_in_odd, v_norm_g, v_norm_b, w_spatial, b_spatial, w_conv_d, b_conv_d, conf_norm_g, conf_norm_b, w_out_odd, w_ffn_gate, w_ffn_up, w_ffn_down, norm_final):
    batch, seq, d = x_prompt.shape
    ns = x_sample.shape[0]
    mp = batch * seq
    m = mp + ns
    depth = norm_mix.shape[0]
    assert x_sample.shape[1] == 1 and mp % ns == 0 and mp % PROMPT_ROWS == 0
    assert m % WIDE_ROW_TILE == 0 and m % DOWN_ROW_TILE == 0
    assert seq % MIX_TILE == 0 and MIX_TILE % CHUNK == 0 and seq % EVEN_MIX_TILE == 0

    x_prompt2d = x_prompt.reshape(mp, d)
    x_sample2d = x_sample.reshape(ns, d)
    entry = _prep(x_prompt2d, rows=PROMPT_ROWS, block0=0, total_rows=m)
    xb, ssq = _prep(x_sample2d, rows=ns, block0=mp // ns, total_rows=m, prev=entry)
    xres_p, xres_s, xres_s_block = x_prompt2d, x_sample2d, 0
    b_spatial_t = jnp.swapaxes(b_spatial, 1, 2)
    conv_hist_t = jnp.swapaxes(state_conv_a, 1, 2)
    pool_hist_t = jnp.swapaxes(state_pool, 1, 2)
    conf_hist_t = jnp.swapaxes(state_conformer, 1, 2)
    even_new, odd_new = [], []
    conv_p, pool_p, chunk_p, chunk_s, conf_p = [], [], [], [], []
    w_outs = [(w_out_even if l % 2 == 0 else w_out_odd, l // 2) for l in range(depth)]
    wob = _cast_bf16(*w_outs[0])
    for l in range(depth):
        i = l // 2
        if l % 2 == 0:
            proj = _proj(xb, ssq, norm_mix[l], w_in_even, i)
            *stream, cs_p, ps_p = _even_prompt(proj, xres_p, w_conv_a, w_pool, pool_scale,
                                               wob, i, batch=batch, seq=seq, total_rows=m)
            x, xb, ssq, *even_new = _even_sample(
                proj, xres_s, xres_s_block, stream, conv_hist_t, pool_hist_t, even_new,
                w_conv_a, w_pool, pool_scale, wob, i, row0=mp)
            conv_p.append(cs_p)
            pool_p.append(ps_p)
        else:
            proj = _proj(xb, ssq, norm_mix[l], w_in_odd, i)
            *stream, cv_p, cf_p = _odd_prompt(
                proj, xres_p, v_norm_g, v_norm_b, w_spatial, b_spatial_t, w_conv_d, b_conv_d,
                conf_norm_g, conf_norm_b, wob, i, batch=batch, seq=seq, total_rows=m)
            x, xb, ssq, vn_s, *odd_new = _odd_sample(
                proj, xres_s, xres_s_block, stream, conf_hist_t, odd_new, v_norm_g, v_norm_b,
                w_spatial, b_spatial, w_conv_d, b_conv_d, conf_norm_g, conf_norm_b, wob, i,
                row0=mp)
            chunk_p.append(cv_p)
            conf_p.append(cf_p)
            chunk_s.append(vn_s[:, None, :])
        last = l == depth - 1
        h, wd_tiles, *next_wob = _gate_up(xb, ssq, norm_ffn[l], w_ffn_gate, w_ffn_up,
                                          w_ffn_down, l, None if last else w_outs[l + 1])
        x, *rest = _matmul_res(h, wd_tiles, x, tm=DOWN_ROW_TILE, name="ffn_down",
                               emit_stream=not last)
        if not last:
            (wob,), (xb, ssq) = next_wob, rest
        xres_p, xres_s, xres_s_block = x, x, mp // ns
    y_p = _final_norm(x, norm_final, rows=PROMPT_ROWS, block0=0, nblocks=mp // PROMPT_ROWS)
    y_s = _final_norm(x, norm_final, rows=ns, block0=mp // ns, nblocks=1)
    conv_s, pool_s = (jnp.swapaxes(t, 1, 2) for t in even_new)
    (conf_s,) = (jnp.swapaxes(t, 1, 2) for t in odd_new)
    return (y_p.reshape(batch, seq, d), y_s.reshape(ns, 1, d),
            jnp.stack(conv_p), conv_s, jnp.stack(pool_p), pool_s,
            jnp.stack(chunk_p), jnp.stack(chunk_s), jnp.stack(conf_p), conf_s)
```
